```python
import jax, jax.numpy as jnp
from jax import lax
import numpy as np

D_MODEL = 1024
BATCH = 8
SEQ = 4096
DEPTH = 4

GRID_W = 64
CTX_LEN = 256
HEAD_DIM = 64
A_HEADS = (D_MODEL // HEAD_DIM) // 2
A_KV_HEADS = 2
A_WIDTH = A_HEADS * HEAD_DIM
B_HEADS = (D_MODEL // HEAD_DIM) // 2
B_WIDTH = B_HEADS * HEAD_DIM
DECAY_LORA = 64
ICLR_LORA = 64
GATE_LORA = 128
C_HEADS = D_MODEL // HEAD_DIM
C_KV_HEADS = 4
C_WIDTH = C_HEADS * HEAD_DIM
WINDOW = 128
Q_BLOCK = 128
D_FF = 2816
N_MOD = 9
ROPE_THETA = 10000.0
EPS = 1e-6
GN_EPS = 64e-5
NEG_BIG = -1e30

A_COLS = A_WIDTH + 2 * A_KV_HEADS * HEAD_DIM
A_SPLITS = (A_WIDTH, A_WIDTH + A_KV_HEADS * HEAD_DIM)
RWKV_COLS = 3 * B_WIDTH + 2 * DECAY_LORA + 2 * ICLR_LORA + GATE_LORA
RWKV_SPLITS = (B_WIDTH, 2 * B_WIDTH, 3 * B_WIDTH, 3 * B_WIDTH + 2 * DECAY_LORA,
               3 * B_WIDTH + 2 * DECAY_LORA + 2 * ICLR_LORA)
EVEN_COLS = A_COLS + RWKV_COLS
ODD_COLS = C_WIDTH + 2 * C_KV_HEADS * HEAD_DIM
ODD_SPLITS = (C_WIDTH, C_WIDTH + C_KV_HEADS * HEAD_DIM)

kernel_name = "hybrid_dit_gqa_rwkv7_swa_macaron"


def rms_norm(x, gain=None):
    xf = x.astype(jnp.float32)
    y = xf * lax.rsqrt(jnp.mean(xf * xf, axis=-1, keepdims=True) + EPS)
    if gain is not None:
        y = y * gain
    return y.astype(x.dtype)


def modulate(h, shift, scale):
    return h * (1.0 + scale) + shift


def swiglu(h, w_in, w_out):
    gate, up = jnp.split(h @ w_in, 2, axis=-1)
    return (jax.nn.silu(gate) * up) @ w_out


def axial_rope_tables(n_tokens):
    rows = n_tokens // GRID_W
    row = jnp.repeat(jnp.arange(rows, dtype=jnp.float32), GRID_W)
    col = jnp.tile(jnp.arange(GRID_W, dtype=jnp.float32), rows)
    n_freq = HEAD_DIM // 4
    inv_freq = ROPE_THETA ** (-jnp.arange(n_freq, dtype=jnp.float32) / n_freq)
    ang = jnp.concatenate([row[:, None] * inv_freq, col[:, None] * inv_freq], axis=-1)
    return jnp.cos(ang), jnp.sin(ang)


def apply_rope(x, cos, sin):
    xp = x.reshape(*x.shape[:-1], HEAD_DIM // 2, 2)
    x0, x1 = xp[..., 0], xp[..., 1]
    c = cos[None, :, None, :].astype(x.dtype)
    s = sin[None, :, None, :].astype(x.dtype)
    return jnp.stack([x0 * c - x1 * s, x0 * s + x1 * c], axis=-1).reshape(x.shape)


def attend(q, k, v):
    s = jnp.einsum('bqhgd,bkhd->bhgqk', q, k).astype(jnp.float32)
    p = jax.nn.softmax(s, axis=-1).astype(v.dtype)
    return jnp.einsum('bhgqk,bkhd->bqhgd', p, v)


def global_gqa(q_l, k_l, v_l, q_c, k_c, v_c, ctx_out):
    Bsz, N = q_l.shape[:2]
    G = A_HEADS // A_KV_HEADS
    scale = HEAD_DIM ** -0.5
    k_all = jnp.concatenate([k_l, k_c], axis=1)
    v_all = jnp.concatenate([v_l, v_c], axis=1)
    qb = (q_l * scale).reshape(Bsz, N // Q_BLOCK, Q_BLOCK, A_KV_HEADS, G, HEAD_DIM).swapaxes(0, 1)
    o_lat = lax.map(lambda qblk: attend(qblk, k_all, v_all), qb)
    o_lat = o_lat.swapaxes(0, 1).reshape(Bsz, N, A_WIDTH)
    o_ctx = None
    if ctx_out:
        L = q_c.shape[1]
        qc = (q_c * scale).reshape(Bsz, L, A_KV_HEADS, G, HEAD_DIM)
        o_ctx = attend(qc, k_c, v_c).reshape(Bsz, L, A_WIDTH)
    return o_lat, o_ctx


def window_gqa(q_l, k_l, v_l, q_c, k_c, v_c, sink, ctx_out):
    Bsz, N = q_l.shape[:2]
    G = C_HEADS // C_KV_HEADS
    scale = HEAD_DIM ** -0.5
    nb = N // Q_BLOCK
    span = Q_BLOCK + 2 * WINDOW
    pad = ((0, 0), (WINDOW, WINDOW), (0, 0), (0, 0))
    k_pad = jnp.pad(k_l, pad)
    v_pad = jnp.pad(v_l, pad)
    qb = (q_l * scale).reshape(Bsz, nb, Q_BLOCK, C_KV_HEADS, G, HEAD_DIM).swapaxes(0, 1)
    sink_hg = sink.reshape(C_KV_HEADS, G).astype(jnp.float32)
    q_off = jnp.arange(Q_BLOCK)
    k_off = jnp.arange(span) - WINDOW
    in_window = jnp.abs(k_off[None, :] - q_off[:, None]) <= WINDOW

    def softmax_with_sink(s):
        sink_col = jnp.broadcast_to(sink_hg[None, :, :, None, None], s.shape[:-1] + (1,))
        return jax.nn.softmax(jnp.concatenate([s, sink_col], axis=-1), axis=-1)[..., :-1]

    def block(args):
        b, qblk = args
        start = b * Q_BLOCK
        kb = lax.dynamic_slice_in_dim(k_pad, start, span, axis=1)
        vb = lax.dynamic_slice_in_dim(v_pad, start, span, axis=1)
        kpos = start - WINDOW + k_off
        valid = in_window & ((kpos >= 0) & (kpos < N))[None, :]
        s_win = jnp.einsum('bqhgd,bkhd->bhgqk', qblk, kb).astype(jnp.float32)
        s_win = jnp.where(valid, s_win, NEG_BIG)
        s_ctx = jnp.einsum('bqhgd,bkhd->bhgqk', qblk, k_c).astype(jnp.float32)
        p = softmax_with_sink(jnp.concatenate([s_win, s_ctx], axis=-1)).astype(vb.dtype)
        return (jnp.einsum('bhgqk,bkhd->bqhgd', p[..., :span], vb)
                + jnp.einsum('bhgqk,bkhd->bqhgd', p[..., span:], v_c))

    o_lat = lax.map(block, (jnp.arange(nb), qb)).swapaxes(0, 1).reshape(Bsz, N, C_WIDTH)
    o_ctx = None
    if ctx_out:
        L = q_c.shape[1]
        qc = (q_c * scale).reshape(Bsz, L, C_KV_HEADS, G, HEAD_DIM)
        s = jnp.einsum('bqhgd,bkhd->bhgqk', qc, k_c).astype(jnp.float32)
        p = softmax_with_sink(s).astype(v_c.dtype)
        o_ctx = jnp.einsum('bhgqk,bkhd->bqhgd', p, v_c).reshape(Bsz, L, C_WIDTH)
    return o_lat, o_ctx


def centred_shift(f, mu):
    prev = jnp.pad(f[:, :-1], ((0, 0), (1, 0), (0, 0)))
    nxt = jnp.pad(f[:, 1:], ((0, 0), (0, 1), (0, 0)))
    return f + mu[0] * (prev - f) + mu[1] * (nxt - f)


def rwkv7_features(f, mu, w0, w2, a0, a2, g2, k_k, k_a):
    Bsz, T, _ = f.shape
    f = centred_shift(f, mu)
    r, k, v, wl, al, gl = jnp.split(f, RWKV_SPLITS, axis=-1)
    heads = lambda t: t.reshape(Bsz, T, B_HEADS, HEAD_DIM)
    dir_heads = lambda t: t.reshape(Bsz, T, 2, B_HEADS, HEAD_DIM)
    kk = heads(k * k_k).astype(jnp.float32)
    kk = kk * lax.rsqrt(jnp.sum(kk * kk, axis=-1, keepdims=True) + EPS)
    w_log = (w0 + jnp.einsum('btdr,drc->btdc', jnp.tanh(wl.reshape(Bsz, T, 2, DECAY_LORA)), w2)).astype(jnp.float32)
    decay = jnp.exp(-jnp.exp(-jax.nn.softplus(-w_log) - 0.5))
    a = jax.nn.sigmoid((a0 + jnp.einsum('btdr,drc->btdc', al.reshape(Bsz, T, 2, ICLR_LORA), a2)).astype(jnp.float32))
    k_dir = k[:, :, None, :].astype(jnp.float32) * (1.0 + (a - 1.0) * k_a)
    g = jax.nn.sigmoid(gl) @ g2
    return (heads(r), heads(k), heads(v), kk, g, dir_heads(decay), dir_heads(a), dir_heads(k_dir))


def wkv7_scan(S0, inputs, reverse):
    xs = tuple(jnp.swapaxes(t.astype(jnp.float32), 0, 1) for t in inputs)

    def step(S, inp):
        r, w, k, v, kk, a = inp
        S = (S * w[:, :, None, :]
             - jnp.einsum('bhvk,bhk->bhv', S, kk)[..., None] * (kk * a)[:, :, None, :]
             + v[..., None] * k[:, :, None, :])
        return S, jnp.einsum('bhvk,bhk->bhv', S, r)

    S, y = lax.scan(step, S0, xs, reverse=reverse)
    return S, jnp.swapaxes(y, 0, 1)


def rwkv7_readout(y, feat, r_k, gn_w, gn_b):
    r, k, v, _, g = feat[:5]
    Bsz, T = y.shape[:2]
    mean = jnp.mean(y, axis=-1, keepdims=True)
    var = jnp.mean(jnp.square(y - mean), axis=-1, keepdims=True)
    yn = ((y - mean) * lax.rsqrt(var + GN_EPS)).reshape(Bsz, T, B_WIDTH) * gn_w + gn_b
    bonus = (jnp.sum(r * k * r_k, axis=-1, keepdims=True) * v).reshape(Bsz, T, B_WIDTH)
    return ((yn + bonus.astype(jnp.float32)) * g.astype(jnp.float32)).astype(g.dtype)


def rwkv7_bidir(f_lat, f_ctx, r_k, gn_w, gn_b, ctx_out):
    Bsz = f_lat[0].shape[0]
    S0 = jnp.zeros((Bsz, B_HEADS, HEAD_DIM, HEAD_DIM), jnp.float32)

    def dir_inputs(f, d):
        r, _, v, kk, _, decay, a, k_dir = f
        return (r, decay[:, :, d], k_dir[:, :, d], v, kk, a[:, :, d])

    S_cf, y_cf = wkv7_scan(S0, dir_inputs(f_ctx, 0), reverse=False)
    S_cb, y_cb = wkv7_scan(S0, dir_inputs(f_ctx, 1), reverse=True)
    _, y_lf = wkv7_scan(S_cf, dir_inputs(f_lat, 0), reverse=False)
    _, y_lb = wkv7_scan(S_cb, dir_inputs(f_lat, 1), reverse=True)
    o_lat = rwkv7_readout(y_lf + y_lb, f_lat, r_k, gn_w, gn_b)
    o_ctx = rwkv7_readout(y_cf + y_cb, f_ctx, r_k, gn_w, gn_b) if ctx_out else None
    return o_lat, o_ctx


def even_mixer(h_lat, h_ctx, cos, sin, w_in, w_out, q_gain, k_gain, mu, w0, w2, a0, a2, g2,
               k_k, k_a, r_k, gn_w, gn_b, ctx_out):
    Bsz, N, _ = h_lat.shape
    L = h_ctx.shape[1]
    p_lat = h_lat @ w_in
    p_ctx = h_ctx @ w_in

    def qkv_a(p, T):
        q, k, v = jnp.split(p[..., :A_COLS], A_SPLITS, axis=-1)
        q = rms_norm(q.reshape(Bsz, T, A_HEADS, HEAD_DIM), q_gain)
        k = rms_norm(k.reshape(Bsz, T, A_KV_HEADS, HEAD_DIM), k_gain)
        return q, k, v.reshape(Bsz, T, A_KV_HEADS, HEAD_DIM)

    q_l, k_l, v_l = qkv_a(p_lat, N)
    q_c, k_c, v_c = qkv_a(p_ctx, L)
    oa_lat, oa_ctx = global_gqa(apply_rope(q_l, cos, sin), apply_rope(k_l, cos, sin), v_l,
                                q_c, k_c, v_c, ctx_out)
    f_lat = rwkv7_features(p_lat[..., A_COLS:], mu, w0, w2, a0, a2, g2, k_k, k_a)
    f_ctx = rwkv7_features(p_ctx[..., A_COLS:], mu, w0, w2, a0, a2, g2, k_k, k_a)
    ob_lat, ob_ctx = rwkv7_bidir(f_lat, f_ctx, r_k, gn_w, gn_b, ctx_out)
    o_lat = jnp.concatenate([oa_lat, ob_lat], axis=-1) @ w_out
    o_ctx = jnp.concatenate([oa_ctx, ob_ctx], axis=-1) @ w_out if ctx_out else None
    return o_lat, o_ctx


def odd_mixer(h_lat, h_ctx, cos, sin, w_in, w_out, sink, ctx_out):
    Bsz, N, _ = h_lat.shape
    L = h_ctx.shape[1]

    def qkv_c(h, T):
        q, k, v = jnp.split(h @ w_in, ODD_SPLITS, axis=-1)
        return (q.reshape(Bsz, T, C_HEADS, HEAD_DIM), k.reshape(Bsz, T, C_KV_HEADS, HEAD_DIM),
                v.reshape(Bsz, T, C_KV_HEADS, HEAD_DIM))

    q_l, k_l, v_l = qkv_c(h_lat, N)
    q_c, k_c, v_c = qkv_c(h_ctx, L)
    o_lat, o_ctx = window_gqa(apply_rope(q_l, cos, sin), apply_rope(k_l, cos, sin), v_l,
                              q_c, k_c, v_c, sink, ctx_out)
    return o_lat @ w_out, (o_ctx @ w_out if ctx_out else None)


def setup_inputs(seed: int = 0) -> dict:
    key = jax.random.key(seed)
    ks = iter(jax.random.split(key, 32))
    nrm = lambda shape, s: s * jax.random.normal(next(ks), shape, jnp.float32)
    uni = lambda shape, lo, hi: jax.random.uniform(next(ks), shape, jnp.float32, lo, hi)
    n_even = (DEPTH + 1) // 2
    n_odd = DEPTH // 2
    D = D_MODEL
    return {
        "x": nrm((BATCH, SEQ, D), 1.0),
        "c": nrm((BATCH, D), 1.0),
        "ctx": nrm((BATCH, CTX_LEN, D), 1.0),
        "c_ctx": nrm((D,), 1.0),
        "w_mod": nrm((DEPTH, D, N_MOD * D), 0.5 * D ** -0.5),
        "b_mod": nrm((DEPTH, N_MOD * D), 0.02),
        "ffn_in": nrm((DEPTH, 2, D, 2 * D_FF), D ** -0.5),
        "ffn_out": nrm((DEPTH, 2, D_FF, D), D_FF ** -0.5),
        "even_w_in": nrm((n_even, D, EVEN_COLS), D ** -0.5),
        "even_w_out": nrm((n_even, A_WIDTH + B_WIDTH, D), (A_WIDTH + B_WIDTH) ** -0.5),
        "q_gain": 1.0 + nrm((n_even, HEAD_DIM), 0.05),
        "k_gain": 1.0 + nrm((n_even, HEAD_DIM), 0.05),
        "rwkv_mu": uni((n_even, 2, RWKV_COLS), 0.0, 0.5),
        "rwkv_w0": uni((n_even, 2, B_WIDTH), -4.0, 1.0),
        "rwkv_w2": nrm((n_even, 2, DECAY_LORA, B_WIDTH), 0.1),
        "rwkv_a0": nrm((n_even, 2, B_WIDTH), 0.5),
        "rwkv_a2": nrm((n_even, 2, ICLR_LORA, B_WIDTH), 0.1),
        "rwkv_g2": nrm((n_even, GATE_LORA, B_WIDTH), GATE_LORA ** -0.5),
        "rwkv_k_k": 0.85 + nrm((n_even, B_WIDTH), 0.05),
        "rwkv_k_a": 1.0 + nrm((n_even, B_WIDTH), 0.05),
        "rwkv_r_k": nrm((n_even, B_HEADS, HEAD_DIM), 0.1),
        "rwkv_gn_w": 1.0 + nrm((n_even, B_WIDTH), 0.05),
        "rwkv_gn_b": nrm((n_even, B_WIDTH), 0.02),
        "odd_w_in": nrm((n_odd, D, ODD_COLS), D ** -0.5),
        "odd_w_out": nrm((n_odd, C_WIDTH, D), C_WIDTH ** -0.5),
        "sink": nrm((n_odd, C_HEADS), 1.0),
        "final_gain": 1.0 + nrm((D,), 0.05),
    }


def reference(x, c, ctx, c_ctx, w_mod, b_mod, ffn_in, ffn_out, even_w_in, even_w_out, q_gain, k_gain,
              rwkv_mu, rwkv_w0, rwkv_w2, rwkv_a0, rwkv_a2, rwkv_g2, rwkv_k_k, rwkv_k_a, rwkv_r_k,
              rwkv_gn_w, rwkv_gn_b, odd_w_in, odd_w_out, sink, final_gain):
    Bsz, N, D = x.shape
    cos, sin = axial_rope_tables(N)
    silu_c = jax.nn.silu(c)
    silu_cc = jax.nn.silu(c_ctx)
    for l in range(DEPTH):
        ctx_out = l < DEPTH - 1
        m_lat = (silu_c @ w_mod[l] + b_mod[l]).reshape(Bsz, N_MOD, 1, D)
        m_ctx = (silu_cc @ w_mod[l] + b_mod[l]).reshape(N_MOD, D)
        x = x + 0.5 * m_lat[:, 2] * swiglu(modulate(rms_norm(x), m_lat[:, 0], m_lat[:, 1]), ffn_in[l, 0], ffn_out[l, 0])
        ctx = ctx + 0.5 * m_ctx[2] * swiglu(modulate(rms_norm(ctx), m_ctx[0], m_ctx[1]), ffn_in[l, 0], ffn_out[l, 0])
        h_lat = modulate(rms_norm(x), m_lat[:, 3], m_lat[:, 4])
        h_ctx = modulate(rms_norm(ctx), m_ctx[3], m_ctx[4])
        if l % 2 == 0:
            e = l // 2
            o_lat, o_ctx = even_mixer(h_lat, h_ctx, cos, sin, even_w_in[e], even_w_out[e], q_gain[e], k_gain[e],
                                      rwkv_mu[e], rwkv_w0[e], rwkv_w2[e], rwkv_a0[e], rwkv_a2[e], rwkv_g2[e],
                                      rwkv_k_k[e], rwkv_k_a[e], rwkv_r_k[e], rwkv_gn_w[e], rwkv_gn_b[e], ctx_out)
        else:
            o = l // 2
            o_lat, o_ctx = odd_mixer(h_lat, h_ctx, cos, sin, odd_w_in[o], odd_w_out[o], sink[o], ctx_out)
        x = x + m_lat[:, 5] * o_lat
        x = x + 0.5 * m_lat[:, 8] * swiglu(modulate(rms_norm(x), m_lat[:, 6], m_lat[:, 7]), ffn_in[l, 1], ffn_out[l, 1])
        if ctx_out:
            ctx = ctx + m_ctx[5] * o_ctx
            ctx = ctx + 0.5 * m_ctx[8] * swiglu(modulate(rms_norm(ctx), m_ctx[6], m_ctx[7]), ffn_in[l, 1], ffn_out[l, 1])
    return rms_norm(x, final_gain)
```

```python
import functools

import jax
import jax.numpy as jnp
import numpy as np
from jax import lax
from jax.experimental import pallas as pl
from jax.experimental.pallas import tpu as pltpu

F32 = jnp.float32
BF16 = jnp.bfloat16
HIGHEST = lax.Precision.HIGHEST

HEAD_DIM = 64
GRID_W = 64
A_HEADS, A_KV_HEADS = 8, 2
B_HEADS = 8
B_WIDTH = B_HEADS * HEAD_DIM
DECAY_LORA, ICLR_LORA, GATE_LORA = 64, 64, 128
C_HEADS, C_KV_HEADS = 16, 4
GQA_GROUP = 4
WINDOW = 128
N_MOD = 9
ROPE_THETA = 10000.0
EPS = 1e-6
GN_EPS = 64e-5
NEG_BIG = -1e30

LANES = 128
SUBLANES = 8
VMEM_LIMIT_BYTES = 56 * 1024 * 1024

SCAN_CHUNK = 64
PAIR_ROWS = 2 * SCAN_CHUNK
TQ = 128
MOD_ROWS = 16


def _cparams(sem):
    return pltpu.CompilerParams(dimension_semantics=sem, vmem_limit_bytes=VMEM_LIMIT_BYTES)


def _dot(a, b, precision=None):
    return jnp.dot(a, b, preferred_element_type=F32, precision=precision)


def _dot_nt(a, b, precision=None):
    return lax.dot_general(a, b, (((1,), (1,)), ((), ())), preferred_element_type=F32, precision=precision)


def _dot_tn(a, b, precision=None):
    return lax.dot_general(a, b, (((0,), (0,)), ((), ())), preferred_element_type=F32, precision=precision)


def _rms(x):
    return x * lax.rsqrt(jnp.mean(x * x, axis=-1, keepdims=True) + EPS)


def _sigmoid(x):
    return 1.0 / (1.0 + jnp.exp(-x))


def _headsum(x, bd):
    hi = x.astype(BF16)
    lo = (x - hi.astype(F32)).astype(BF16)
    return _dot(hi, bd) + _dot(lo, bd)


def _mod_kernel(c_ref, w_ref, b_ref, o_ref):
    c = c_ref[...]
    o_ref[...] = _dot(c * _sigmoid(c), w_ref[...], HIGHEST) + b_ref[...]


def _mod_table(cvec, w_mod, b_mod):
    depth, d, nd = w_mod.shape
    tn = 1536 if nd % 1536 == 0 else nd
    return pl.pallas_call(
        _mod_kernel,
        grid=(depth, nd // tn),
        in_specs=[
            pl.BlockSpec((MOD_ROWS, d), lambda l, n: (0, 0)),
            pl.BlockSpec((None, d, tn), lambda l, n: (l, 0, n)),
            pl.BlockSpec((None, 1, tn), lambda l, n: (l, 0, n)),
        ],
        out_specs=pl.BlockSpec((None, MOD_ROWS, tn), lambda l, n: (l, 0, n)),
        out_shape=jax.ShapeDtypeStruct((depth, MOD_ROWS, nd), F32),
        compiler_params=_cparams(("parallel", "parallel")),
        name="mod_table",
    )(cvec, w_mod, b_mod.reshape(depth, 1, nd))


def _ffn_kernel(x_ref, mod_ref, wg_ref, wu_ref, wo_ref, *rest, i_shift, final):
    if final:
        fg_ref, o_ref, hn_ref, acc_ref = rest
    else:
        o_ref, hn_ref, acc_ref = rest
    f = pl.program_id(1)

    @pl.when(f == 0)
    def _():
        hn = _rms(x_ref[...]) * (1.0 + mod_ref[i_shift + 1:i_shift + 2, :]) + mod_ref[i_shift:i_shift + 1, :]
        hn_ref[...] = hn.astype(BF16)
        acc_ref[...] = jnp.zeros_like(acc_ref)

    hn = hn_ref[...]
    g = _dot(hn, wg_ref[...])
    u = _dot(hn, wu_ref[...])
    a = (g * _sigmoid(g) * u).astype(BF16)
    acc_ref[...] += _dot(a, wo_ref[...])

    @pl.when(f == pl.num_programs(1) - 1)
    def _():
        y = x_ref[...] + (0.5 * mod_ref[i_shift + 2:i_shift + 3, :]) * acc_ref[...]
        if final:
            y = _rms(y) * fg_ref[...]
        o_ref[...] = y


def _ffn(xs, mods, w_in, w_out, l, j, i_shift, dims, final_gain=None):
    rows, d = xs.shape
    dff = w_out.shape[2]
    tm, tf = dims["tm_ffn"], dims["tf"]
    nf = dff // tf
    tiles_per_batch = dims["n"] // tm
    nb = dims["b"]
    final = final_gain is not None
    if final:
        rows = nb * dims["n"]
    in_specs = [
        pl.BlockSpec((tm, d), lambda i, f: (i, 0)),
        pl.BlockSpec((None, None, N_MOD, d), lambda i, f: (l, jnp.minimum(i // tiles_per_batch, nb), 0, 0)),
        pl.BlockSpec((None, None, d, tf), lambda i, f: (l, j, 0, f)),
        pl.BlockSpec((None, None, d, tf), lambda i, f: (l, j, 0, f + nf)),
        pl.BlockSpec((None, None, tf, d), lambda i, f: (l, j, f, 0)),
    ]
    args = [xs, mods, w_in, w_in, w_out]
    if final:
        in_specs.append(pl.BlockSpec((1, d), lambda i, f: (0, 0)))
        args.append(final_gain.reshape(1, d))
    return pl.pallas_call(
        functools.partial(_ffn_kernel, i_shift=i_shift, final=final),
        grid=(rows // tm, nf),
        in_specs=in_specs,
        out_specs=pl.BlockSpec((tm, d), lambda i, f: (i, 0)),
        out_shape=jax.ShapeDtypeStruct((rows, d), F32),
        scratch_shapes=[pltpu.VMEM((tm, d), BF16), pltpu.VMEM((tm, d), F32)],
        compiler_params=_cparams(("parallel", "arbitrary")),
        name="ffn",
    )(*args)


def _rope(x, cos, sin_a, sin_b):
    outs = []
    for g in range(x.shape[1] // LANES):
        xg = x[:, g * LANES:(g + 1) * LANES]
        nxt = pltpu.roll(xg, LANES - 1, 1)
        prv = pltpu.roll(xg, 1, 1)
        outs.append(xg * cos + nxt * sin_a + prv * sin_b)
    return outs[0] if len(outs) == 1 else jnp.concatenate(outs, axis=1)


def _proj_kernel(x_ref, mod_ref, w_ref, cos_ref, sa_ref, sb_ref, *rest, qw, kw, fw, qk_norm):
    if qk_norm:
        qg_ref, kg_ref, bdq_ref, bdk_ref = rest[:4]
        rest = rest[4:]
    q_ref, k_ref, v_ref = rest[:3]
    hn = (_rms(x_ref[...]) * (1.0 + mod_ref[4:5, :]) + mod_ref[3:4, :]).astype(BF16)
    p = _dot(hn, w_ref[...])
    q = p[:, :qw]
    k = p[:, qw:qw + kw]
    if qk_norm:
        q = q * lax.rsqrt(_headsum(q * q, bdq_ref[...]) + EPS) * qg_ref[...]
        k = k * lax.rsqrt(_headsum(k * k, bdk_ref[...]) + EPS) * kg_ref[...]
    cos, sa, sb = cos_ref[...], sa_ref[...], sb_ref[...]
    q_ref[...] = (_rope(q, cos, sa, sb) * (HEAD_DIM ** -0.5)).astype(BF16)
    k_ref[...] = _rope(k, cos, sa, sb).astype(BF16)
    v_ref[...] = p[:, qw + kw:qw + 2 * kw].astype(BF16)
    if fw:
        rest[3][...] = p[:, qw + 2 * kw:]


def _proj(xs, mods, w_in, l, e, rope_tabs, dims, qw, kw, fw, norm_args=None):
    rows, d = xs.shape
    tm = dims["tm_proj"]
    cols = qw + 2 * kw + fw
    tiles_per_batch = dims["n"] // tm
    n_lat_tiles = dims["b"] * tiles_per_batch
    nb = dims["b"]

    def tab_idx(i):
        return (jnp.where(i < n_lat_tiles, i % tiles_per_batch, tiles_per_batch), 0)

    in_specs = [
        pl.BlockSpec((tm, d), lambda i: (i, 0)),
        pl.BlockSpec((None, None, N_MOD, d), lambda i: (l, jnp.minimum(i // tiles_per_batch, nb), 0, 0)),
        pl.BlockSpec((None, d, cols), lambda i: (e, 0, 0)),
        pl.BlockSpec((tm, LANES), tab_idx),
        pl.BlockSpec((tm, LANES), tab_idx),
        pl.BlockSpec((tm, LANES), tab_idx),
    ]
    args = [xs, mods, w_in, *rope_tabs]
    qk_norm = norm_args is not None
    if qk_norm:
        qg, kg, bdq, bdk = norm_args
        in_specs += [
            pl.BlockSpec((1, qw), lambda i: (0, 0)),
            pl.BlockSpec((1, kw), lambda i: (0, 0)),
            pl.BlockSpec((qw, qw), lambda i: (0, 0)),
            pl.BlockSpec((kw, kw), lambda i: (0, 0)),
        ]
        args += [qg, kg, bdq, bdk]
    out_specs = [
        pl.BlockSpec((tm, qw), lambda i: (i, 0)),
        pl.BlockSpec((tm, kw), lambda i: (i, 0)),
        pl.BlockSpec((tm, kw), lambda i: (i, 0)),
    ]
    out_shape = [
        jax.ShapeDtypeStruct((rows, qw), BF16),
        jax.ShapeDtypeStruct((rows, kw), BF16),
        jax.ShapeDtypeStruct((rows, kw), BF16),
    ]
    if fw:
        out_specs.append(pl.BlockSpec((tm, fw), lambda i: (i, 0)))
        out_shape.append(jax.ShapeDtypeStruct((rows, fw), F32))
    return pl.pallas_call(
        functools.partial(_proj_kernel, qw=qw, kw=kw, fw=fw, qk_norm=qk_norm),
        grid=(rows // tm,),
        in_specs=in_specs,
        out_specs=out_specs,
        out_shape=out_shape,
        compiler_params=_cparams(("parallel",)),
        name="mixer_proj",
    )(*args)


def _attn_kernel(q_ref, kl_ref, vl_ref, kc_ref, vc_ref, *rest, n, n_q_lat, key_chunk, window, has_sink):
    if has_sink:
        sink_ref, o_ref = rest
    else:
        (o_ref,) = rest
    i = pl.program_id(2)
    tq = q_ref.shape[0]
    rows = GQA_GROUP * tq

    def attend(latent):
        outs = []
        for jj in range(2):
            hs = slice(jj * HEAD_DIM, (jj + 1) * HEAD_DIM)
            q4 = jnp.concatenate(
                [q_ref[:, (jj * GQA_GROUP + g) * HEAD_DIM:(jj * GQA_GROUP + g + 1) * HEAD_DIM]
                 for g in range(GQA_GROUP)], axis=0)
            if has_sink:
                m = sink_ref[jj]
                l = jnp.ones((rows, 1), F32)
            else:
                m = jnp.full((rows, 1), NEG_BIG, F32)
                l = jnp.zeros((rows, 1), F32)
            acc = jnp.zeros((rows, HEAD_DIM), F32)

            segs = []
            if latent and window is None:
                for c in range(n // key_chunk):
                    ks = slice(c * key_chunk, (c + 1) * key_chunk)
                    segs.append((kl_ref[ks, hs], vl_ref[ks, hs], None))
            elif latent:
                span = tq + 2 * window
                start = pl.multiple_of(jnp.clip(i * tq - window, 0, n - span), LANES)
                qpos = i * tq + lax.broadcasted_iota(jnp.int32, (tq, span), 0)
                kpos = start + lax.broadcasted_iota(jnp.int32, (tq, span), 1)
                valid = (jnp.abs(kpos - qpos) <= window) & (kpos >= window)
                valid = jnp.concatenate([valid] * GQA_GROUP, axis=0)
                kwin = kl_ref[pl.ds(start, span), :]
                vwin = vl_ref[pl.ds(start, span), :]
                segs.append((kwin[:, hs], vwin[:, hs], valid))
                q_col = i * tq + lax.broadcasted_iota(jnp.int32, (tq, 1), 0)
                n_pad = jnp.maximum(q_col + window - n + 1, 0).astype(F32)
                n_pad = jnp.concatenate([n_pad] * GQA_GROUP, axis=0)
                m_new = jnp.maximum(m, jnp.where(n_pad > 0.0, 0.0, NEG_BIG))
                l = jnp.exp(m - m_new) * l + jnp.where(n_pad > 0.0, n_pad * jnp.exp(-m_new), 0.0)
                m = m_new
            segs.append((kc_ref[:, hs], vc_ref[:, hs], None))

            for k, v, valid in segs:
                s = _dot_nt(q4, k)
                if valid is not None:
                    s = jnp.where(valid, s, NEG_BIG)
                m_new = jnp.maximum(m, jnp.max(s, axis=-1, keepdims=True))
                alpha = jnp.exp(m - m_new)
                p = jnp.exp(s - m_new)
                l = alpha * l + jnp.sum(p, axis=-1, keepdims=True)
                acc = alpha * acc + _dot(p.astype(BF16), v)
                m = m_new
            o = acc / l
            outs += [o[g * tq:(g + 1) * tq] for g in range(GQA_GROUP)]
        o_ref[...] = jnp.concatenate(outs, axis=1).astype(o_ref.dtype)

    @pl.when(i < n_q_lat)
    def _():
        attend(True)

    @pl.when(i >= n_q_lat)
    def _():
        attend(False)


def _attention(q, k, v, dims, window=None, sink=None, ctx_queries=True):
    rows, qw = q.shape
    kw = k.shape[1]
    n, lc, nb = dims["n"], dims["l"], dims["b"]
    n_pairs = kw // LANES
    n_q_lat = n // TQ
    n_q_ctx = lc // TQ if ctx_queries else 0
    lat_blocks = nb * n_q_lat
    qcols = 2 * GQA_GROUP * HEAD_DIM

    def q_idx(b, p, i):
        return (jnp.where(i < n_q_lat, b * n_q_lat + i, lat_blocks + b * (lc // TQ) + (i - n_q_lat)), p)

    in_specs = [
        pl.BlockSpec((TQ, qcols), q_idx),
        pl.BlockSpec((n, LANES), lambda b, p, i: (b, p)),
        pl.BlockSpec((n, LANES), lambda b, p, i: (b, p)),
        pl.BlockSpec((lc, LANES), lambda b, p, i: (nb * n // lc + b, p)),
        pl.BlockSpec((lc, LANES), lambda b, p, i: (nb * n // lc + b, p)),
    ]
    args = [q, k, v, k, v]
    has_sink = sink is not None
    if has_sink:
        sink_rows = jnp.repeat(sink.astype(F32).reshape(n_pairs, 2, GQA_GROUP), TQ, axis=2)[..., None]
        in_specs.append(pl.BlockSpec((None, 2, GQA_GROUP * TQ, 1), lambda b, p, i: (p, 0, 0, 0)))
        args.append(sink_rows)
    return pl.pallas_call(
        functools.partial(_attn_kernel, n=n, n_q_lat=n_q_lat, key_chunk=dims["key_chunk"], window=window,
                          has_sink=has_sink),
        grid=(nb, n_pairs, n_q_lat + n_q_ctx),
        in_specs=in_specs,
        out_specs=pl.BlockSpec((TQ, qcols), q_idx),
        out_shape=jax.ShapeDtypeStruct((rows, qw), BF16),
        compiler_params=_cparams(("parallel", "parallel", "arbitrary")),
        name="gqa_window" if window is not None else "gqa_global",
    )(*args)


def _feat_kernel(f_ref, fp_ref, fn_ref, mu_ref, w0_ref, w2_ref, a0_ref, a2_ref, g2_ref, kk_ref, ka_ref, bd_ref,
                 r_o, k_o, v_o, kk_o, g_o, lw0_o, lw1_o, kd0_o, kd1_o, b0_o, b1_o,
                 *, n_lat_tiles, tiles_lat, tiles_ctx):
    i = pl.program_id(0)
    t = f_ref.shape[0]
    bw = B_WIDTH
    is_lat = i < n_lat_tiles
    pos = jnp.where(is_lat, i % tiles_lat, (i - n_lat_tiles) % tiles_ctx)
    last = jnp.where(is_lat, tiles_lat, tiles_ctx) - 1
    f = f_ref[...]
    prow = jnp.where(pos == 0, 0.0, fp_ref[SUBLANES - 1:SUBLANES, :])
    nrow = jnp.where(pos == last, 0.0, fn_ref[0:1, :])
    rid = lax.broadcasted_iota(jnp.int32, (t, 1), 0)
    prev = jnp.where(rid == 0, prow, pltpu.roll(f, 1, 0))
    nxt = jnp.where(rid == t - 1, nrow, pltpu.roll(f, t - 1, 0))
    fs = f + mu_ref[0:1, :] * (prev - f) + mu_ref[1:2, :] * (nxt - f)

    r = fs[:, :bw]
    k = fs[:, bw:2 * bw]
    v = fs[:, 2 * bw:3 * bw]
    wl = jnp.tanh(fs[:, 3 * bw:3 * bw + 2 * DECAY_LORA])
    al = fs[:, 3 * bw + 2 * DECAY_LORA:3 * bw + 2 * DECAY_LORA + 2 * ICLR_LORA]
    gl = fs[:, 3 * bw + 2 * DECAY_LORA + 2 * ICLR_LORA:]

    kk = k * kk_ref[...]
    kk = kk * lax.rsqrt(_headsum(kk * kk, bd_ref[...]) + EPS)
    r_o[...] = r
    k_o[...] = k
    v_o[...] = v
    kk_o[...] = kk
    g_o[...] = _dot(_sigmoid(gl), g2_ref[...], HIGHEST)
    for d, (lw_o, kd_o, b_o) in enumerate(((lw0_o, kd0_o, b0_o), (lw1_o, kd1_o, b1_o))):
        w_log = w0_ref[d:d + 1, :] + _dot(wl, w2_ref[d], HIGHEST)
        z = -w_log
        softplus = jnp.maximum(z, 0.0) + jnp.log(1.0 + jnp.exp(-jnp.abs(z)))
        lw_o[...] = -jnp.exp(-softplus - 0.5)
        a = _sigmoid(a0_ref[d:d + 1, :] + _dot(al, a2_ref[d], HIGHEST))
        kd_o[...] = k * (1.0 + (a - 1.0) * ka_ref[...])
        b_o[...] = kk * a


def _rwkv_features(f, params, dims):
    rows, fw = f.shape
    t = dims["tm_feat"]
    n, lc, nb = dims["n"], dims["l"], dims["b"]
    tiles_lat, tiles_ctx = n // t, lc // t
    n_lat_tiles = nb * tiles_lat
    hb = t // SUBLANES
    n_halo = rows // SUBLANES
    mu, w0, w2p, a0, a2p, g2, k_k, k_a, bd = params
    bw = B_WIDTH
    const = lambda shape: pl.BlockSpec(shape, lambda i: (0,) * len(shape))
    in_specs = [
        pl.BlockSpec((t, fw), lambda i: (i, 0)),
        pl.BlockSpec((SUBLANES, fw), lambda i: (jnp.maximum(i * hb - 1, 0), 0)),
        pl.BlockSpec((SUBLANES, fw), lambda i: (jnp.minimum((i + 1) * hb, n_halo - 1), 0)),
        const((2, fw)), const((2, bw)), const((2, 2 * DECAY_LORA, bw)), const((2, bw)),
        const((2, 2 * ICLR_LORA, bw)), const((GATE_LORA, bw)), const((1, bw)), const((1, bw)), const((bw, bw)),
    ]
    out = jax.ShapeDtypeStruct((rows, bw), F32)
    return pl.pallas_call(
        functools.partial(_feat_kernel, n_lat_tiles=n_lat_tiles, tiles_lat=tiles_lat, tiles_ctx=tiles_ctx),
        grid=(rows // t,),
        in_specs=in_specs,
        out_specs=[pl.BlockSpec((t, bw), lambda i: (i, 0))] * 11,
        out_shape=[out] * 11,
        compiler_params=_cparams(("parallel",)),
        name="rwkv_features",
    )(f, f, f, mu, w0, w2p, a0, a2p, g2, k_k, k_a, bd)


def _stack_heads(x, head0):
    return jnp.concatenate([jnp.where(head0, x, 0.0), jnp.where(head0, 0.0, x)], axis=0)


def _wkv_chunk(r, v, kk, lw, kd, b, h, tri, eye, m_strict, m_incl, lvl_ref):
    c = r.shape[0]
    head0 = lax.broadcasted_iota(jnp.int32, (1, LANES), 1) < HEAD_DIM
    cw = _dot(tri, lw, HIGHEST)
    tot = jnp.sum(lw, axis=0, keepdims=True)
    w_incl = jnp.exp(cw)
    w_excl = jnp.exp(cw - lw)
    w_inv = jnp.exp(-cw)
    w_rest = jnp.exp(tot - cw)
    st = lambda x: _stack_heads(x, head0)
    kq, rq = st(kk * w_excl), st(r * w_incl)
    bi, ki = st(b * w_inv), st(kd * w_inv)
    bt, kt = st(b * w_rest), st(kd * w_rest)
    vst = st(v)

    a = _dot_nt(jnp.concatenate([kq, rq], axis=0), jnp.concatenate([bi, ki], axis=0), HIGHEST)
    p2 = 2 * c
    nkb = jnp.where(m_strict, a[:p2, :p2], 0.0)
    akk = jnp.where(m_strict, a[:p2, p2:], 0.0)
    arb = jnp.where(m_incl, a[p2:, :p2], 0.0)
    ark = jnp.where(m_incl, a[p2:, p2:], 0.0)

    tinv = eye - nkb * lvl_ref[0]
    for lev in range(1, lvl_ref.shape[0]):
        lk = nkb * lvl_ref[lev]
        tinv = tinv - _dot(tinv, _dot(lk, tinv, HIGHEST), HIGHEST)

    akkv = _dot(akk, vst, HIGHEST)
    gp = _dot(tinv, jnp.concatenate([kq, akkv], axis=1), HIGHEST)
    corr = _dot(arb, gp, HIGHEST)
    yl = _dot(ark, vst, HIGHEST) - corr[:, LANES:]
    rq2 = rq - corr[:, :LANES]
    fold = lambda x: x[:c] + x[c:]
    rq2, yl = fold(rq2), fold(yl)

    btg = _dot_tn(bt, gp, HIGHEST)
    decay_diag = jnp.where(eye > 0.0, jnp.exp(tot), 0.0)
    m_state = decay_diag - btg[:, :LANES]
    n_state = _dot_tn(kt, vst, HIGHEST) - btg[:, LANES:]

    y = _dot(rq2, h, HIGHEST) + yl
    h_new = _dot(m_state, h, HIGHEST) + n_state
    return y, h_new


def _scan_kernel(rf, vf, kkf, lwf, kdf, bf, rb, vb, kkb, lwb, kdb, bb, tri_ref, msk_ref, lvl_ref,
                 yf_o, yb_o, hf_s, hb_s):
    @pl.when(pl.program_id(1) == 0)
    def _():
        hf_s[...] = jnp.zeros_like(hf_s)
        hb_s[...] = jnp.zeros_like(hb_s)

    eye = msk_ref[0]
    for d, (refs, y_o, h_s) in enumerate((((rf, vf, kkf, lwf, kdf, bf), yf_o, hf_s),
                                          ((rb, vb, kkb, lwb, kdb, bb), yb_o, hb_s))):
        m_strict = msk_ref[1 + 2 * d] > 0.0
        m_incl = msk_ref[2 + 2 * d] > 0.0
        tri = tri_ref[d]
        for p in range(B_WIDTH // LANES):
            ls = slice(p * LANES, (p + 1) * LANES)
            y, h_new = _wkv_chunk(*(ref[:, ls] for ref in refs), h_s[p], tri, eye, m_strict, m_incl, lvl_ref)
            y_o[:, ls] = y
            h_s[p] = h_new


def _scan_constants():
    c, p2 = SCAN_CHUNK, PAIR_ROWS
    t = np.arange(c)
    tri = np.stack([(t[None, :] <= t[:, None]), (t[None, :] >= t[:, None])]).astype(np.float32)
    i = np.arange(p2)
    same_head = (i[:, None] // c) == (i[None, :] // c)
    ti, si = i[:, None] % c, i[None, :] % c
    msk = np.stack([
        np.eye(p2, dtype=bool),
        same_head & (si < ti), same_head & (si <= ti),
        same_head & (si > ti), same_head & (si >= ti),
    ]).astype(np.float32)
    n_lev = int(np.log2(c))
    lvl = np.stack([
        ((i[:, None] >> (k + 1)) == (i[None, :] >> (k + 1))) & ((i[:, None] >> k) != (i[None, :] >> k))
        for k in range(n_lev)
    ]).astype(np.float32)
    return jnp.asarray(tri), jnp.asarray(msk), jnp.asarray(lvl)


def _wkv_scan(feats, dims):
    r, v, kk, lw0, lw1, kd0, kd1, b0, b1 = feats
    rows, bw = r.shape
    c = SCAN_CHUNK
    n, lc, nb = dims["n"], dims["l"], dims["b"]
    n_c, l_c = n // c, lc // c
    ctx_base = nb * n_c
    tri, msk, lvl = _scan_constants()

    def fwd(b, s):
        return (jnp.where(s < l_c, ctx_base + b * l_c + s, b * n_c + (s - l_c)), 0)

    def bwd(b, s):
        return (jnp.where(s < l_c, ctx_base + b * l_c + (l_c - 1 - s), b * n_c + (n_c - 1 - (s - l_c))), 0)

    const = lambda a: pl.BlockSpec(a.shape, lambda b, s: (0,) * a.ndim)
    out = jax.ShapeDtypeStruct((rows, bw), F32)
    return pl.pallas_call(
        _scan_kernel,
        grid=(nb, l_c + n_c),
        in_specs=[pl.BlockSpec((c, bw), fwd)] * 6 + [pl.BlockSpec((c, bw), bwd)] * 6
                 + [const(tri), const(msk), const(lvl)],
        out_specs=[pl.BlockSpec((c, bw), fwd), pl.BlockSpec((c, bw), bwd)],
        out_shape=[out, out],
        scratch_shapes=[pltpu.VMEM((bw // LANES, LANES, LANES), F32)] * 2,
        compiler_params=_cparams(("parallel", "arbitrary")),
        name="wkv_scan",
    )(r, v, kk, lw0, kd0, b0, r, v, kk, lw1, kd1, b1, tri, msk, lvl)


def _outproj_kernel(x_ref, mod_ref, wo_ref, oa_ref, *rest, even):
    o_ref = rest[-1]
    if even:
        yf, yb, r, k, v, g, rk, gnw, gnb, bd = rest[:-1]
        y = yf[...] + yb[...]
        mean = _headsum(y, bd[...]) * (1.0 / HEAD_DIM)
        yc = y - mean
        var = _headsum(yc * yc, bd[...]) * (1.0 / HEAD_DIM)
        yn = yc * lax.rsqrt(var + GN_EPS) * gnw[...] + gnb[...]
        bonus = _headsum(r[...] * k[...] * rk[...], bd[...]) * v[...]
        ob = ((yn + bonus) * g[...]).astype(BF16)
        half = oa_ref.shape[1]
        o = _dot(oa_ref[...], wo_ref[:half, :]) + _dot(ob, wo_ref[half:, :])
    else:
        o = _dot(oa_ref[...], wo_ref[...])
    o_ref[...] = x_ref[...] + mod_ref[5:6, :] * o


def _outproj(xs, mods, w_out, l, e, o_att, dims, readout=None):
    rows, d = xs.shape
    tm = dims["tm_proj"]
    tiles_per_batch = dims["n"] // tm
    nb = dims["b"]
    even = readout is not None
    in_specs = [
        pl.BlockSpec((tm, d), lambda i: (i, 0)),
        pl.BlockSpec((None, None, N_MOD, d), lambda i: (l, jnp.minimum(i // tiles_per_batch, nb), 0, 0)),
        pl.BlockSpec((None, d, d), lambda i: (e, 0, 0)),
        pl.BlockSpec((tm, o_att.shape[1]), lambda i: (i, 0)),
    ]
    args = [xs, mods, w_out, o_att]
    if even:
        yf, yb, r, k, v, g, rk, gnw, gnb, bd = readout
        bw = B_WIDTH
        in_specs += [pl.BlockSpec((tm, bw), lambda i: (i, 0))] * 6
        in_specs += [pl.BlockSpec((1, bw), lambda i: (0, 0))] * 3 + [pl.BlockSpec((bw, bw), lambda i: (0, 0))]
        args += [yf, yb, r, k, v, g, rk, gnw, gnb, bd]
    return pl.pallas_call(
        functools.partial(_outproj_kernel, even=even),
        grid=(rows // tm,),
        in_specs=in_specs,
        out_specs=pl.BlockSpec((tm, d), lambda i: (i, 0)),
        out_shape=jax.ShapeDtypeStruct((rows, d), F32),
        compiler_params=_cparams(("parallel",)),
        name="mixer_out",
    )(*args)


def _rope_tables(n, tm):
    rows = n // GRID_W
    row = jnp.repeat(jnp.arange(rows, dtype=F32), GRID_W)
    col = jnp.tile(jnp.arange(GRID_W, dtype=F32), rows)
    n_freq = HEAD_DIM // 4
    inv_freq = ROPE_THETA ** (-jnp.arange(n_freq, dtype=F32) / n_freq)
    ang = jnp.concatenate([row[:, None] * inv_freq, col[:, None] * inv_freq], axis=-1)
    cos = jnp.repeat(jnp.cos(ang), 2, axis=-1)
    sin = jnp.repeat(jnp.sin(ang), 2, axis=-1)
    even_lane = (jnp.arange(HEAD_DIM) % 2 == 0)[None, :]
    sin_a = jnp.where(even_lane, -sin, 0.0)
    sin_b = jnp.where(even_lane, 0.0, sin)
    pad = lambda tab, fill: jnp.concatenate(
        [jnp.tile(tab, (1, LANES // HEAD_DIM)), jnp.full((tm, LANES), fill, F32)], axis=0)
    return pad(cos, 1.0), pad(sin_a, 0.0), pad(sin_b, 0.0)


def _block_diag(width, value, dtype):
    h = np.arange(width) // HEAD_DIM
    return jnp.asarray((h[:, None] == h[None, :]) * value, dtype)


def _pad_lora(w):
    r = w.shape[1]
    z = jnp.zeros_like(w[0])
    return jnp.stack([jnp.concatenate([w[0], z], axis=0), jnp.concatenate([z, w[1]], axis=0)])


def _largest_tile(limit, *sizes):
    t = limit
    while any(s % t for s in sizes):
        t //= 2
    return t


def kernel(x, c, ctx, c_ctx, w_mod, b_mod, ffn_in, ffn_out, even_w_in, even_w_out, q_gain, k_gain, rwkv_mu, rwkv_w0,
           rwkv_w2, rwkv_a0, rwkv_a2, rwkv_g2, rwkv_k_k, rwkv_k_a, rwkv_r_k, rwkv_gn_w, rwkv_gn_b, odd_w_in, odd_w_out,
           sink, final_gain):
    nb, n, d = x.shape
    lc = ctx.shape[1]
    depth = w_mod.shape[0]
    dff = ffn_out.shape[2]
    assert n % TQ == 0 and lc % TQ == 0 and n >= TQ + 2 * WINDOW and n % GRID_W == 0
    assert n % SCAN_CHUNK == 0 and lc % SCAN_CHUNK == 0 and nb + 1 <= MOD_ROWS
    dims = {
        "b": nb, "n": n, "l": lc,
        "tm_ffn": _largest_tile(512, n, nb * lc),
        "tm_proj": _largest_tile(512, n, nb * lc),
        "tm_feat": _largest_tile(256, n, lc),
        "tf": 1408 if dff % 1408 == 0 else dff,
        "key_chunk": _largest_tile(1024, n),
    }

    cvec = jnp.zeros((MOD_ROWS, d), F32).at[:nb].set(c).at[nb].set(c_ctx)
    mods = _mod_table(cvec, w_mod, b_mod).reshape(depth, MOD_ROWS, N_MOD, d)

    ffn_in_h = ffn_in.astype(BF16)
    ffn_out_h = ffn_out.astype(BF16)
    even_in_h = even_w_in.astype(BF16)
    even_out_h = even_w_out.astype(BF16)
    odd_in_h = odd_w_in.astype(BF16)
    odd_out_h = odd_w_out.astype(BF16)

    rope_tabs = _rope_tables(n, dims["tm_proj"])
    a_kw = A_KV_HEADS * HEAD_DIM
    a_qw = A_HEADS * HEAD_DIM
    bd_q = _block_diag(a_qw, 1.0 / HEAD_DIM, BF16)
    bd_k = _block_diag(a_kw, 1.0 / HEAD_DIM, BF16)
    bd_ones = _block_diag(B_WIDTH, 1.0, BF16)
    fw = even_w_in.shape[2] - a_qw - 2 * a_kw

    xs = jnp.concatenate([x.reshape(nb * n, d), ctx.reshape(nb * lc, d)], axis=0)
    for l in range(depth):
        last = l == depth - 1
        xs = _ffn(xs, mods, ffn_in_h, ffn_out_h, l, 0, 0, dims)
        if l % 2 == 0:
            e = l // 2
            tile_h = lambda g, reps: jnp.tile(g, reps).reshape(1, -1)
            norm_args = (tile_h(q_gain[e], A_HEADS), tile_h(k_gain[e], A_KV_HEADS), bd_q, bd_k)
            q, k, v, f = _proj(xs, mods, even_in_h, l, e, rope_tabs, dims, a_qw, a_kw, fw, norm_args)
            oa = _attention(q, k, v, dims)
            feat_params = (rwkv_mu[e], rwkv_w0[e], _pad_lora(rwkv_w2[e]), rwkv_a0[e], _pad_lora(rwkv_a2[e]),
                           rwkv_g2[e], rwkv_k_k[e].reshape(1, -1), rwkv_k_a[e].reshape(1, -1), bd_ones)
            r, kr, vr, kk, g, lw0, lw1, kd0, kd1, b0, b1 = _rwkv_features(f, feat_params, dims)
            yf, yb = _wkv_scan((r, vr, kk, lw0, lw1, kd0, kd1, b0, b1), dims)
            readout = (yf, yb, r, kr, vr, g, rwkv_r_k[e].reshape(1, -1), rwkv_gn_w[e].reshape(1, -1),
                       rwkv_gn_b[e].reshape(1, -1), bd_ones)
            xs = _outproj(xs, mods, even_out_h, l, e, oa, dims, readout)
        else:
            o = l // 2
            c_kw = C_KV_HEADS * HEAD_DIM
            c_qw = C_HEADS * HEAD_DIM
            q, k, v = _proj(xs, mods, odd_in_h, l, o, rope_tabs, dims, c_qw, c_kw, 0)
            oc = _attention(q, k, v, dims, window=WINDOW, sink=sink[o])
            xs = _outproj(xs, mods, odd_out_h, l, o, oc, dims)
        xs = _ffn(xs, mods, ffn_in_h, ffn_out_h, l, 1, 6, dims, final_gain=final_gain if last else None)
    return xs.reshape(nb, n, d)
```

```python
import functools

import jax
import jax.numpy as jnp
import numpy as np
from jax import lax
from jax.experimental import pallas as pl
from jax.experimental.pallas import tpu as pltpu

F32 = jnp.float32
BF16 = jnp.bfloat16
HIGHEST = lax.Precision.HIGHEST

HEAD_DIM = 64
GRID_W = 64
A_HEADS, A_KV_HEADS = 8, 2
B_HEADS = 8
B_WIDTH = B_HEADS * HEAD_DIM
DECAY_LORA, ICLR_LORA, GATE_LORA = 64, 64, 128
C_HEADS, C_KV_HEADS = 16, 4
GQA_GROUP = 4
WINDOW = 128
N_MOD = 9
ROPE_THETA = 10000.0
EPS = 1e-6
GN_EPS = 64e-5
NEG_BIG = -1e30

LANES = 128
SUBLANES = 8
VMEM_LIMIT_BYTES = 56 * 1024 * 1024

SCAN_CHUNK = 64
PAIR_ROWS = 2 * SCAN_CHUNK
TQ = 128
MOD_ROWS = 16


def _cparams(sem):
    return pltpu.CompilerParams(dimension_semantics=sem, vmem_limit_bytes=VMEM_LIMIT_BYTES)


def _dot(a, b, precision=None):
    return jnp.dot(a, b, preferred_element_type=F32, precision=precision)


def _dot_nt(a, b, precision=None):
    return lax.dot_general(a, b, (((1,), (1,)), ((), ())), preferred_element_type=F32, precision=precision)


def _dot_tn(a, b, precision=None):
    return lax.dot_general(a, b, (((0,), (0,)), ((), ())), preferred_element_type=F32, precision=precision)


def _rms(x):
    return x * lax.rsqrt(jnp.mean(x * x, axis=-1, keepdims=True) + EPS)


def _sigmoid(x):
    return 1.0 / (1.0 + jnp.exp(-x))


def _headsum(x, bd):
    hi = x.astype(BF16)
    lo = (x - hi.astype(F32)).astype(BF16)
    return _dot(hi, bd) + _dot(lo, bd)


def _mod_kernel(c_ref, w_ref, b_ref, o_ref):
    c = c_ref[...]
    o_ref[...] = _dot(c * _sigmoid(c), w_ref[...], HIGHEST) + b_ref[...]


def _mod_table(cvec, w_mod, b_mod):
    depth, d, nd = w_mod.shape
    tn = 1536 if nd % 1536 == 0 else nd
    return pl.pallas_call(
        _mod_kernel,
        grid=(depth, nd // tn),
        in_specs=[
            pl.BlockSpec((MOD_ROWS, d), lambda l, n: (0, 0)),
            pl.BlockSpec((None, d, tn), lambda l, n: (l, 0, n)),
            pl.BlockSpec((None, 1, tn), lambda l, n: (l, 0, n)),
        ],
        out_specs=pl.BlockSpec((None, MOD_ROWS, tn), lambda l, n: (l, 0, n)),
        out_shape=jax.ShapeDtypeStruct((depth, MOD_ROWS, nd), F32),
        compiler_params=_cparams(("parallel", "parallel")),
        name="mod_table",
    )(cvec, w_mod, b_mod.reshape(depth, 1, nd))


def _ffn_kernel(x_ref, mod_ref, wg_ref, wu_ref, wo_ref, *rest, i_shift, final):
    if final:
        fg_ref, o_ref, hn_ref, acc_ref = rest
    else:
        o_ref, hn_ref, acc_ref = rest
    f = pl.program_id(1)

    @pl.when(f == 0)
    def _():
        hn = _rms(x_ref[...]) * (1.0 + mod_ref[i_shift + 1:i_shift + 2, :]) + mod_ref[i_shift:i_shift + 1, :]
        hn_ref[...] = hn.astype(BF16)
        acc_ref[...] = jnp.zeros_like(acc_ref)

    hn = hn_ref[...]
    g = _dot(hn, wg_ref[...])
    u = _dot(hn, wu_ref[...])
    a = (g * _sigmoid(g) * u).astype(BF16)
    acc_ref[...] += _dot(a, wo_ref[...])

    @pl.when(f == pl.num_programs(1) - 1)
    def _():
        y = x_ref[...] + (0.5 * mod_ref[i_shift + 2:i_shift + 3, :]) * acc_ref[...]
        if final:
            y = _rms(y) * fg_ref[...]
        o_ref[...] = y


def _ffn(xs, mods, w_in, w_out, l, j, i_shift, dims, final_gain=None):
    rows, d = xs.shape
    dff = w_out.shape[2]
    tm, tf = dims["tm_ffn"], dims["tf"]
    nf = dff // tf
    tiles_per_batch = dims["n"] // tm
    nb = dims["b"]
    final = final_gain is not None
    if final:
        rows = nb * dims["n"]
    in_specs = [
        pl.BlockSpec((tm, d), lambda i, f: (i, 0)),
        pl.BlockSpec((None, None, N_MOD, d), lambda i, f: (l, jnp.minimum(i // tiles_per_batch, nb), 0, 0)),
        pl.BlockSpec((None, None, d, tf), lambda i, f: (l, j, 0, f)),
        pl.BlockSpec((None, None, d, tf), lambda i, f: (l, j, 0, f + nf)),
        pl.BlockSpec((None, None, tf, d), lambda i, f: (l, j, f, 0)),
    ]
    args = [xs, mods, w_in, w_in, w_out]
    if final:
        in_specs.append(pl.BlockSpec((1, d), lambda i, f: (0, 0)))
        args.append(final_gain.reshape(1, d))
    return pl.pallas_call(
        functools.partial(_ffn_kernel, i_shift=i_shift, final=final),
        grid=(rows // tm, nf),
        in_specs=in_specs,
        out_specs=pl.BlockSpec((tm, d), lambda i, f: (i, 0)),
        out_shape=jax.ShapeDtypeStruct((rows, d), F32),
        scratch_shapes=[pltpu.VMEM((tm, d), BF16), pltpu.VMEM((tm, d), F32)],
        compiler_params=_cparams(("parallel", "arbitrary")),
        name="ffn",
    )(*args)


def _rope(x, cos, sin_a, sin_b):
    outs = []
    for g in range(x.shape[1] // LANES):
        xg = x[:, g * LANES:(g + 1) * LANES]
        nxt = pltpu.roll(xg, LANES - 1, 1)
        prv = pltpu.roll(xg, 1, 1)
        outs.append(xg * cos + nxt * sin_a + prv * sin_b)
    return outs[0] if len(outs) == 1 else jnp.concatenate(outs, axis=1)


def _proj_kernel(x_ref, mod_ref, w_ref, cos_ref, sa_ref, sb_ref, *rest, qw, kw, fw, qk_norm):
    if qk_norm:
        qg_ref, kg_ref, bdq_ref, bdk_ref = rest[:4]
        rest = rest[4:]
    q_ref, k_ref, v_ref = rest[:3]
    hn = (_rms(x_ref[...]) * (1.0 + mod_ref[4:5, :]) + mod_ref[3:4, :]).astype(BF16)
    p = _dot(hn, w_ref[...])
    q = p[:, :qw]
    k = p[:, qw:qw + kw]
    if qk_norm:
        q = q * lax.rsqrt(_headsum(q * q, bdq_ref[...]) + EPS) * qg_ref[...]
        k = k * lax.rsqrt(_headsum(k * k, bdk_ref[...]) + EPS) * kg_ref[...]
    cos, sa, sb = cos_ref[...], sa_ref[...], sb_ref[...]
    q_ref[...] = (_rope(q, cos, sa, sb) * (HEAD_DIM ** -0.5)).astype(BF16)
    k_ref[...] = _rope(k, cos, sa, sb).astype(BF16)
    v_ref[...] = p[:, qw + kw:qw + 2 * kw].astype(BF16)
    if fw:
        rest[3][...] = p[:, qw + 2 * kw:]


def _proj(xs, mods, w_in, l, e, rope_tabs, dims, qw, kw, fw, norm_args=None):
    rows, d = xs.shape
    tm = dims["tm_proj"]
    cols = qw + 2 * kw + fw
    tiles_per_batch = dims["n"] // tm
    n_lat_tiles = dims["b"] * tiles_per_batch
    nb = dims["b"]

    def tab_idx(i):
        return (jnp.where(i < n_lat_tiles, i % tiles_per_batch, tiles_per_batch), 0)

    in_specs = [
        pl.BlockSpec((tm, d), lambda i: (i, 0)),
        pl.BlockSpec((None, None, N_MOD, d), lambda i: (l, jnp.minimum(i // tiles_per_batch, nb), 0, 0)),
        pl.BlockSpec((None, d, cols), lambda i: (e, 0, 0)),
        pl.BlockSpec((tm, LANES), tab_idx),
        pl.BlockSpec((tm, LANES), tab_idx),
        pl.BlockSpec((tm, LANES), tab_idx),
    ]
    args = [xs, mods, w_in, *rope_tabs]
    qk_norm = norm_args is not None
    if qk_norm:
        qg, kg, bdq, bdk = norm_args
        in_specs += [
            pl.BlockSpec((1, qw), lambda i: (0, 0)),
            pl.BlockSpec((1, kw), lambda i: (0, 0)),
            pl.BlockSpec((qw, qw), lambda i: (0, 0)),
            pl.BlockSpec((kw, kw), lambda i: (0, 0)),
        ]
        args += [qg, kg, bdq, bdk]
    out_specs = [
        pl.BlockSpec((tm, qw), lambda i: (i, 0)),
        pl.BlockSpec((tm, kw), lambda i: (i, 0)),
        pl.BlockSpec((tm, kw), lambda i: (i, 0)),
    ]
    out_shape = [
        jax.ShapeDtypeStruct((rows, qw), BF16),
        jax.ShapeDtypeStruct((rows, kw), BF16),
        jax.ShapeDtypeStruct((rows, kw), BF16),
    ]
    if fw:
        out_specs.append(pl.BlockSpec((tm, fw), lambda i: (i, 0)))
        out_shape.append(jax.ShapeDtypeStruct((rows, fw), F32))
    return pl.pallas_call(
        functools.partial(_proj_kernel, qw=qw, kw=kw, fw=fw, qk_norm=qk_norm),
        grid=(rows // tm,),
        in_specs=in_specs,
        out_specs=out_specs,
        out_shape=out_shape,
        compiler_params=_cparams(("parallel",)),
        name="mixer_proj",
    )(*args)


def _attn_kernel(q_ref, kl_ref, vl_ref, kc_ref, vc_ref, *rest, n, n_q_lat, key_chunk, window, has_sink):
    if has_sink:
        sink_ref, o_ref = rest
    else:
        (o_ref,) = rest
    i = pl.program_id(2)
    tq = q_ref.shape[0]
    rows = GQA_GROUP * tq

    def attend(latent):
        outs = []
        for jj in range(2):
            hs = slice(jj * HEAD_DIM, (jj + 1) * HEAD_DIM)
            q4 = jnp.concatenate(
                [q_ref[:, (jj * GQA_GROUP + g) * HEAD_DIM:(jj * GQA_GROUP + g + 1) * HEAD_DIM]
                 for g in range(GQA_GROUP)], axis=0)
            segs = []
            extras = []
            if has_sink:
                extras.append((sink_ref[jj], None))
            if latent and window is None:
                for c in range(n // key_chunk):
                    ks = slice(c * key_chunk, (c + 1) * key_chunk)
                    segs.append((_dot_nt(q4, kl_ref[ks, hs]), vl_ref[ks, hs]))
            elif latent:
                span = tq + 2 * window
                start = pl.multiple_of(jnp.clip(i * tq - window, 0, n - span), LANES)
                qpos = i * tq + lax.broadcasted_iota(jnp.int32, (tq, span), 0)
                kpos = start + lax.broadcasted_iota(jnp.int32, (tq, span), 1)
                valid = (jnp.abs(kpos - qpos) <= window) & (kpos >= window)
                valid = jnp.concatenate([valid] * GQA_GROUP, axis=0)
                kwin = kl_ref[pl.ds(start, span), :]
                vwin = vl_ref[pl.ds(start, span), :]
                segs.append((jnp.where(valid, _dot_nt(q4, kwin[:, hs]), NEG_BIG), vwin[:, hs]))
                q_col = i * tq + lax.broadcasted_iota(jnp.int32, (tq, 1), 0)
                n_pad = jnp.maximum(q_col + window - n + 1, 0).astype(F32)
                n_pad = jnp.concatenate([n_pad] * GQA_GROUP, axis=0)
                extras.append((jnp.where(n_pad > 0.0, 0.0, NEG_BIG), n_pad))
            segs.append((_dot_nt(q4, kc_ref[:, hs]), vc_ref[:, hs]))

            m = functools.reduce(jnp.maximum, [jnp.max(s, axis=-1, keepdims=True) for s, _ in segs]
                                 + [logit for logit, _ in extras])
            l = jnp.zeros((rows, 1), F32)
            acc = jnp.zeros((rows, HEAD_DIM), F32)
            for s, v in segs:
                p = jnp.exp(s - m)
                l = l + jnp.sum(p, axis=-1, keepdims=True)
                acc = acc + _dot(p.astype(BF16), v)
            for logit, count in extras:
                e = jnp.exp(logit - m)
                l = l + (e if count is None else count * e)
            o = acc / l
            outs += [o[g * tq:(g + 1) * tq] for g in range(GQA_GROUP)]
        o_ref[...] = jnp.concatenate(outs, axis=1).astype(o_ref.dtype)

    @pl.when(i < n_q_lat)
    def _():
        attend(True)

    @pl.when(i >= n_q_lat)
    def _():
        attend(False)


def _attention(q, k, v, dims, window=None, sink=None, ctx_queries=True):
    rows, qw = q.shape
    kw = k.shape[1]
    n, lc, nb = dims["n"], dims["l"], dims["b"]
    n_pairs = kw // LANES
    n_q_lat = n // TQ
    n_q_ctx = lc // TQ if ctx_queries else 0
    lat_blocks = nb * n_q_lat
    qcols = 2 * GQA_GROUP * HEAD_DIM

    def q_idx(b, p, i):
        return (jnp.where(i < n_q_lat, b * n_q_lat + i, lat_blocks + b * (lc // TQ) + (i - n_q_lat)), p)

    in_specs = [
        pl.BlockSpec((TQ, qcols), q_idx),
        pl.BlockSpec((n, LANES), lambda b, p, i: (b, p)),
        pl.BlockSpec((n, LANES), lambda b, p, i: (b, p)),
        pl.BlockSpec((lc, LANES), lambda b, p, i: (nb * n // lc + b, p)),
        pl.BlockSpec((lc, LANES), lambda b, p, i: (nb * n // lc + b, p)),
    ]
    args = [q, k, v, k, v]
    has_sink = sink is not None
    if has_sink:
        sink_rows = jnp.repeat(sink.astype(F32).reshape(n_pairs, 2, GQA_GROUP), TQ, axis=2)[..., None]
        in_specs.append(pl.BlockSpec((None, 2, GQA_GROUP * TQ, 1), lambda b, p, i: (p, 0, 0, 0)))
        args.append(sink_rows)
    return pl.pallas_call(
        functools.partial(_attn_kernel, n=n, n_q_lat=n_q_lat, key_chunk=dims["key_chunk"], window=window,
                          has_sink=has_sink),
        grid=(nb, n_pairs, n_q_lat + n_q_ctx),
        in_specs=in_specs,
        out_specs=pl.BlockSpec((TQ, qcols), q_idx),
        out_shape=jax.ShapeDtypeStruct((rows, qw), BF16),
        compiler_params=_cparams(("parallel", "parallel", "arbitrary")),
        name="gqa_window" if window is not None else "gqa_global",
    )(*args)


def _feat_kernel(f_ref, fp_ref, fn_ref, mu_ref, w0_ref, w2_ref, a0_ref, a2_ref, g2_ref, kk_ref, ka_ref, bd_ref,
                 r_o, k_o, v_o, kk_o, g_o, lw0_o, lw1_o, kd0_o, kd1_o, b0_o, b1_o,
                 *, n_lat_tiles, tiles_lat, tiles_ctx):
    i = pl.program_id(0)
    t = f_ref.shape[0]
    bw = B_WIDTH
    is_lat = i < n_lat_tiles
    pos = jnp.where(is_lat, i % tiles_lat, (i - n_lat_tiles) % tiles_ctx)
    last = jnp.where(is_lat, tiles_lat, tiles_ctx) - 1
    f = f_ref[...]
    prow = jnp.where(pos == 0, 0.0, fp_ref[SUBLANES - 1:SUBLANES, :])
    nrow = jnp.where(pos == last, 0.0, fn_ref[0:1, :])
    rid = lax.broadcasted_iota(jnp.int32, (t, 1), 0)
    prev = jnp.where(rid == 0, prow, pltpu.roll(f, 1, 0))
    nxt = jnp.where(rid == t - 1, nrow, pltpu.roll(f, t - 1, 0))
    fs = f + mu_ref[0:1, :] * (prev - f) + mu_ref[1:2, :] * (nxt - f)

    r = fs[:, :bw]
    k = fs[:, bw:2 * bw]
    v = fs[:, 2 * bw:3 * bw]
    wl = jnp.tanh(fs[:, 3 * bw:3 * bw + 2 * DECAY_LORA])
    al = fs[:, 3 * bw + 2 * DECAY_LORA:3 * bw + 2 * DECAY_LORA + 2 * ICLR_LORA]
    gl = fs[:, 3 * bw + 2 * DECAY_LORA + 2 * ICLR_LORA:]

    kk = k * kk_ref[...]
    kk = kk * lax.rsqrt(_headsum(kk * kk, bd_ref[...]) + EPS)
    r_o[...] = r
    k_o[...] = k
    v_o[...] = v
    kk_o[...] = kk
    g_o[...] = _dot(_sigmoid(gl), g2_ref[...], HIGHEST)
    for d, (lw_o, kd_o, b_o) in enumerate(((lw0_o, kd0_o, b0_o), (lw1_o, kd1_o, b1_o))):
        w_log = w0_ref[d:d + 1, :] + _dot(wl, w2_ref[d], HIGHEST)
        z = -w_log
        softplus = jnp.maximum(z, 0.0) + jnp.log(1.0 + jnp.exp(-jnp.abs(z)))
        lw_o[...] = -jnp.exp(-softplus - 0.5)
        a = _sigmoid(a0_ref[d:d + 1, :] + _dot(al, a2_ref[d], HIGHEST))
        kd_o[...] = k * (1.0 + (a - 1.0) * ka_ref[...])
        b_o[...] = kk * a


def _rwkv_features(f, params, dims):
    rows, fw = f.shape
    t = dims["tm_feat"]
    n, lc, nb = dims["n"], dims["l"], dims["b"]
    tiles_lat, tiles_ctx = n // t, lc // t
    n_lat_tiles = nb * tiles_lat
    hb = t // SUBLANES
    n_halo = rows // SUBLANES
    mu, w0, w2p, a0, a2p, g2, k_k, k_a, bd = params
    bw = B_WIDTH
    const = lambda shape: pl.BlockSpec(shape, lambda i: (0,) * len(shape))
    in_specs = [
        pl.BlockSpec((t, fw), lambda i: (i, 0)),
        pl.BlockSpec((SUBLANES, fw), lambda i: (jnp.maximum(i * hb - 1, 0), 0)),
        pl.BlockSpec((SUBLANES, fw), lambda i: (jnp.minimum((i + 1) * hb, n_halo - 1), 0)),
        const((2, fw)), const((2, bw)), const((2, 2 * DECAY_LORA, bw)), const((2, bw)),
        const((2, 2 * ICLR_LORA, bw)), const((GATE_LORA, bw)), const((1, bw)), const((1, bw)), const((bw, bw)),
    ]
    out = jax.ShapeDtypeStruct((rows, bw), F32)
    return pl.pallas_call(
        functools.partial(_feat_kernel, n_lat_tiles=n_lat_tiles, tiles_lat=tiles_lat, tiles_ctx=tiles_ctx),
        grid=(rows // t,),
        in_specs=in_specs,
        out_specs=[pl.BlockSpec((t, bw), lambda i: (i, 0))] * 11,
        out_shape=[out] * 11,
        compiler_params=_cparams(("parallel",)),
        name="rwkv_features",
    )(f, f, f, mu, w0, w2p, a0, a2p, g2, k_k, k_a, bd)


def _stack_heads(x, head0):
    return jnp.concatenate([jnp.where(head0, x, 0.0), jnp.where(head0, 0.0, x)], axis=0).astype(BF16)


def _cumsum_rows(tri, x):
    hi = x.astype(BF16)
    r1 = x - hi.astype(F32)
    mid = r1.astype(BF16)
    lo = (r1 - mid.astype(F32)).astype(BF16)
    return _dot(tri, hi) + _dot(tri, mid) + _dot(tri, lo)


def _wkv_units(units, eye, lvl_ref):
    c = units[0]["v"].shape[0]
    p2 = 2 * c
    head0 = lax.broadcasted_iota(jnp.int32, (1, LANES), 1) < HEAD_DIM
    st = lambda x: _stack_heads(x, head0)
    fold = lambda x: x[:c] + x[c:]
    each = lambda fn: [fn(u) for u in units]

    for u in units:
        u["kq_s"], u["rq_s"], u["v_s"] = st(u["kq"]), st(u["rq"]), st(u["v"])
    a = each(lambda u: _dot_nt(jnp.concatenate([u["kq_s"], u["rq_s"]], axis=0),
                               jnp.concatenate([st(u["bi"]), st(u["ki"])], axis=0)))
    for u, au in zip(units, a):
        u["nkb"] = jnp.where(u["m_strict"], au[:p2, :p2], 0.0)
        u["akk"] = jnp.where(u["m_strict"], au[:p2, p2:], 0.0).astype(BF16)
        u["arb"] = jnp.where(u["m_incl"], au[p2:, :p2], 0.0).astype(BF16)
        u["ark"] = jnp.where(u["m_incl"], au[p2:, p2:], 0.0).astype(BF16)

    nkb_h = each(lambda u: u["nkb"].astype(BF16))
    tinv = each(lambda u: eye - u["nkb"] * lvl_ref[0].astype(F32))
    for lev in range(1, lvl_ref.shape[0]):
        t_h = [t.astype(BF16) for t in tinv]
        lt = [_dot(nh * lvl_ref[lev], th).astype(BF16) for nh, th in zip(nkb_h, t_h)]
        tinv = [t - _dot(th, x) for t, th, x in zip(tinv, t_h, lt)]
    t_h = [t.astype(BF16) for t in tinv]

    akkv = each(lambda u: _dot(u["akk"], u["v_s"]).astype(BF16))
    gp_h = [_dot(th, jnp.concatenate([u["kq_s"], x], axis=1)).astype(BF16)
            for u, th, x in zip(units, t_h, akkv)]
    corr = [_dot(u["arb"], g) for u, g in zip(units, gp_h)]
    arkv = each(lambda u: _dot(u["ark"], u["v_s"]))
    btg = [_dot_tn(st(u["bt"]), g) for u, g in zip(units, gp_h)]
    ktv = each(lambda u: _dot_tn(st(u["kt"]), u["v_s"]))

    outs = []
    for u, cr, av, bg, kv in zip(units, corr, arkv, btg, ktv):
        rq = u["rq"]
        rq2 = fold(jnp.concatenate([jnp.where(head0, rq, 0.0), jnp.where(head0, 0.0, rq)], axis=0) - cr[:, :LANES])
        yl = fold(av - cr[:, LANES:])
        m_state = jnp.where(eye > 0.0, u["decay"], 0.0) - bg[:, :LANES]
        n_state = kv - bg[:, LANES:]
        h_h = u["h"].astype(BF16)
        y = _dot(rq2.astype(BF16), h_h) + yl
        h_new = _dot(m_state.astype(BF16), h_h) + n_state
        outs.append((y, h_new))
    return outs


def _scan_kernel(rf, vf, kkf, lwf, kdf, bf, rb, vb, kkb, lwb, kdb, bb, tri_ref, msk_ref, lvl_ref,
                 yf_o, yb_o, hf_s, hb_s):
    @pl.when(pl.program_id(1) == 0)
    def _():
        hf_s[...] = jnp.zeros_like(hf_s)
        hb_s[...] = jnp.zeros_like(hb_s)

    eye = msk_ref[0]
    units, sinks = [], []
    for d, (refs, y_o, h_s) in enumerate((((rf, vf, kkf, lwf, kdf, bf), yf_o, hf_s),
                                          ((rb, vb, kkb, lwb, kdb, bb), yb_o, hb_s))):
        m_strict = msk_ref[1 + 2 * d] > 0.0
        m_incl = msk_ref[2 + 2 * d] > 0.0
        r, v, kk, lw, kd, b = (ref[...] for ref in refs)
        cw = _cumsum_rows(tri_ref[d], lw)
        tot = jnp.sum(lw, axis=0, keepdims=True)
        w_inv = jnp.exp(-cw)
        w_rest = jnp.exp(tot - cw)
        facs = {"kq": kk * jnp.exp(cw - lw), "rq": r * jnp.exp(cw), "bi": b * w_inv, "ki": kd * w_inv,
                "bt": b * w_rest, "kt": kd * w_rest, "v": v, "decay": jnp.exp(tot)}
        for p in range(B_WIDTH // LANES):
            ls = slice(p * LANES, (p + 1) * LANES)
            unit = {name: x[:, ls] for name, x in facs.items()}
            unit.update(h=h_s[p], m_strict=m_strict, m_incl=m_incl)
            units.append(unit)
            sinks.append((y_o, h_s, p, ls))
    for (y, h_new), (y_o, h_s, p, ls) in zip(_wkv_units(units, eye, lvl_ref), sinks):
        y_o[:, ls] = y
        h_s[p] = h_new


def _scan_constants():
    c, p2 = SCAN_CHUNK, PAIR_ROWS
    t = np.arange(c)
    tri = np.stack([(t[None, :] <= t[:, None]), (t[None, :] >= t[:, None])]).astype(np.float32)
    i = np.arange(p2)
    same_head = (i[:, None] // c) == (i[None, :] // c)
    ti, si = i[:, None] % c, i[None, :] % c
    msk = np.stack([
        np.eye(p2, dtype=bool),
        same_head & (si < ti), same_head & (si <= ti),
        same_head & (si > ti), same_head & (si >= ti),
    ]).astype(np.float32)
    n_lev = int(np.log2(c))
    lvl = np.stack([
        ((i[:, None] >> (k + 1)) == (i[None, :] >> (k + 1))) & ((i[:, None] >> k) != (i[None, :] >> k))
        for k in range(n_lev)
    ]).astype(np.float32)
    return jnp.asarray(tri, BF16), jnp.asarray(msk), jnp.asarray(lvl, BF16)


def _wkv_scan(feats, dims):
    r, v, kk, lw0, lw1, kd0, kd1, b0, b1 = feats
    rows, bw = r.shape
    c = SCAN_CHUNK
    n, lc, nb = dims["n"], dims["l"], dims["b"]
    n_c, l_c = n // c, lc // c
    ctx_base = nb * n_c
    tri, msk, lvl = _scan_constants()

    def fwd(b, s):
        return (jnp.where(s < l_c, ctx_base + b * l_c + s, b * n_c + (s - l_c)), 0)

    def bwd(b, s):
        return (jnp.where(s < l_c, ctx_base + b * l_c + (l_c - 1 - s), b * n_c + (n_c - 1 - (s - l_c))), 0)

    const = lambda a: pl.BlockSpec(a.shape, lambda b, s: (0,) * a.ndim)
    out = jax.ShapeDtypeStruct((rows, bw), F32)
    return pl.pallas_call(
        _scan_kernel,
        grid=(nb, l_c + n_c),
        in_specs=[pl.BlockSpec((c, bw), fwd)] * 6 + [pl.BlockSpec((c, bw), bwd)] * 6
                 + [const(tri), const(msk), const(lvl)],
        out_specs=[pl.BlockSpec((c, bw), fwd), pl.BlockSpec((c, bw), bwd)],
        out_shape=[out, out],
        scratch_shapes=[pltpu.VMEM((bw // LANES, LANES, LANES), F32)] * 2,
        compiler_params=_cparams(("parallel", "arbitrary")),
        name="wkv_scan",
    )(r, v, kk, lw0, kd0, b0, r, v, kk, lw1, kd1, b1, tri, msk, lvl)


def _outproj_kernel(x_ref, mod_ref, wo_ref, oa_ref, *rest, even):
    o_ref = rest[-1]
    if even:
        yf, yb, r, k, v, g, rk, gnw, gnb, bd = rest[:-1]
        y = yf[...] + yb[...]
        mean = _headsum(y, bd[...]) * (1.0 / HEAD_DIM)
        yc = y - mean
        var = _headsum(yc * yc, bd[...]) * (1.0 / HEAD_DIM)
        yn = yc * lax.rsqrt(var + GN_EPS) * gnw[...] + gnb[...]
        bonus = _headsum(r[...] * k[...] * rk[...], bd[...]) * v[...]
        ob = ((yn + bonus) * g[...]).astype(BF16)
        half = oa_ref.shape[1]
        o = _dot(oa_ref[...], wo_ref[:half, :]) + _dot(ob, wo_ref[half:, :])
    else:
        o = _dot(oa_ref[...], wo_ref[...])
    o_ref[...] = x_ref[...] + mod_ref[5:6, :] * o


def _outproj(xs, mods, w_out, l, e, o_att, dims, readout=None):
    rows, d = xs.shape
    tm = dims["tm_proj"]
    tiles_per_batch = dims["n"] // tm
    nb = dims["b"]
    even = readout is not None
    in_specs = [
        pl.BlockSpec((tm, d), lambda i: (i, 0)),
        pl.BlockSpec((None, None, N_MOD, d), lambda i: (l, jnp.minimum(i // tiles_per_batch, nb), 0, 0)),
        pl.BlockSpec((None, d, d), lambda i: (e, 0, 0)),
        pl.BlockSpec((tm, o_att.shape[1]), lambda i: (i, 0)),
    ]
    args = [xs, mods, w_out, o_att]
    if even:
        yf, yb, r, k, v, g, rk, gnw, gnb, bd = readout
        bw = B_WIDTH
        in_specs += [pl.BlockSpec((tm, bw), lambda i: (i, 0))] * 6
        in_specs += [pl.BlockSpec((1, bw), lambda i: (0, 0))] * 3 + [pl.BlockSpec((bw, bw), lambda i: (0, 0))]
        args += [yf, yb, r, k, v, g, rk, gnw, gnb, bd]
    return pl.pallas_call(
        functools.partial(_outproj_kernel, even=even),
        grid=(rows // tm,),
        in_specs=in_specs,
        out_specs=pl.BlockSpec((tm, d), lambda i: (i, 0)),
        out_shape=jax.ShapeDtypeStruct((rows, d), F32),
        compiler_params=_cparams(("parallel",)),
        name="mixer_out",
    )(*args)


def _rope_tables(n, tm):
    rows = n // GRID_W
    row = jnp.repeat(jnp.arange(rows, dtype=F32), GRID_W)
    col = jnp.tile(jnp.arange(GRID_W, dtype=F32), rows)
    n_freq = HEAD_DIM // 4
    inv_freq = ROPE_THETA ** (-jnp.arange(n_freq, dtype=F32) / n_freq)
    ang = jnp.concatenate([row[:, None] * inv_freq, col[:, None] * inv_freq], axis=-1)
    cos = jnp.repeat(jnp.cos(ang), 2, axis=-1)
    sin = jnp.repeat(jnp.sin(ang), 2, axis=-1)
    even_lane = (jnp.arange(HEAD_DIM) % 2 == 0)[None, :]
    sin_a = jnp.where(even_lane, -sin, 0.0)
    sin_b = jnp.where(even_lane, 0.0, sin)
    pad = lambda tab, fill: jnp.concatenate(
        [jnp.tile(tab, (1, LANES // HEAD_DIM)), jnp.full((tm, LANES), fill, F32)], axis=0)
    return pad(cos, 1.0), pad(sin_a, 0.0), pad(sin_b, 0.0)


def _block_diag(width, value, dtype):
    h = np.arange(width) // HEAD_DIM
    return jnp.asarray((h[:, None] == h[None, :]) * value, dtype)


def _pad_lora(w):
    r = w.shape[1]
    z = jnp.zeros_like(w[0])
    return jnp.stack([jnp.concatenate([w[0], z], axis=0), jnp.concatenate([z, w[1]], axis=0)])


def _largest_tile(limit, *sizes):
    t = limit
    while any(s % t for s in sizes):
        t //= 2
    return t


def kernel(x, c, ctx, c_ctx, w_mod, b_mod, ffn_in, ffn_out, even_w_in, even_w_out, q_gain, k_gain, rwkv_mu, rwkv_w0,
           rwkv_w2, rwkv_a0, rwkv_a2, rwkv_g2, rwkv_k_k, rwkv_k_a, rwkv_r_k, rwkv_gn_w, rwkv_gn_b, odd_w_in, odd_w_out,
           sink, final_gain):
    nb, n, d = x.shape
    lc = ctx.shape[1]
    depth = w_mod.shape[0]
    dff = ffn_out.shape[2]
    assert n % TQ == 0 and lc % TQ == 0 and n >= TQ + 2 * WINDOW and n % GRID_W == 0
    assert n % SCAN_CHUNK == 0 and lc % SCAN_CHUNK == 0 and nb + 1 <= MOD_ROWS
    dims = {
        "b": nb, "n": n, "l": lc,
        "tm_ffn": _largest_tile(512, n, nb * lc),
        "tm_proj": _largest_tile(512, n, nb * lc),
        "tm_feat": _largest_tile(256, n, lc),
        "tf": 1408 if dff % 1408 == 0 else dff,
        "key_chunk": _largest_tile(1024, n),
    }

    cvec = jnp.zeros((MOD_ROWS, d), F32).at[:nb].set(c).at[nb].set(c_ctx)
    mods = _mod_table(cvec, w_mod, b_mod).reshape(depth, MOD_ROWS, N_MOD, d)

    ffn_in_h = ffn_in.astype(BF16)
    ffn_out_h = ffn_out.astype(BF16)
    even_in_h = even_w_in.astype(BF16)
    even_out_h = even_w_out.astype(BF16)
    odd_in_h = odd_w_in.astype(BF16)
    odd_out_h = odd_w_out.astype(BF16)

    rope_tabs = _rope_tables(n, dims["tm_proj"])
    a_kw = A_KV_HEADS * HEAD_DIM
    a_qw = A_HEADS * HEAD_DIM
    bd_q = _block_diag(a_qw, 1.0 / HEAD_DIM, BF16)
    bd_k = _block_diag(a_kw, 1.0 / HEAD_DIM, BF16)
    bd_ones = _block_diag(B_WIDTH, 1.0, BF16)
    fw = even_w_in.shape[2] - a_qw - 2 * a_kw

    xs = jnp.concatenate([x.reshape(nb * n, d), ctx.reshape(nb * lc, d)], axis=0)
    for l in range(depth):
        last = l == depth - 1
        xs = _ffn(xs, mods, ffn_in_h, ffn_out_h, l, 0, 0, dims)
        if l % 2 == 0:
            e = l // 2
            tile_h = lambda g, reps: jnp.tile(g, reps).reshape(1, -1)
            norm_args = (tile_h(q_gain[e], A_HEADS), tile_h(k_gain[e], A_KV_HEADS), bd_q, bd_k)
            q, k, v, f = _proj(xs, mods, even_in_h, l, e, rope_tabs, dims, a_qw, a_kw, fw, norm_args)
            oa = _attention(q, k, v, dims)
            feat_params = (rwkv_mu[e], rwkv_w0[e], _pad_lora(rwkv_w2[e]), rwkv_a0[e], _pad_lora(rwkv_a2[e]),
                           rwkv_g2[e], rwkv_k_k[e].reshape(1, -1), rwkv_k_a[e].reshape(1, -1), bd_ones)
            r, kr, vr, kk, g, lw0, lw1, kd0, kd1, b0, b1 = _rwkv_features(f, feat_params, dims)
            yf, yb = _wkv_scan((r, vr, kk, lw0, lw1, kd0, kd1, b0, b1), dims)
            readout = (yf, yb, r, kr, vr, g, rwkv_r_k[e].reshape(1, -1), rwkv_gn_w[e].reshape(1, -1),
                       rwkv_gn_b[e].reshape(1, -1), bd_ones)
            xs = _outproj(xs, mods, even_out_h, l, e, oa, dims, readout)
        else:
            o = l // 2
            c_kw = C_KV_HEADS * HEAD_DIM
            c_qw = C_HEADS * HEAD_DIM
            q, k, v = _proj(xs, mods, odd_in_h, l, o, rope_tabs, dims, c_qw, c_kw, 0)
            oc = _attention(q, k, v, dims, window=WINDOW, sink=sink[o])
            xs = _outproj(xs, mods, odd_out_h, l, o, oc, dims)
        xs = _ffn(xs, mods, ffn_in_h, ffn_out_h, l, 1, 6, dims, final_gain=final_gain if last else None)
    return xs.reshape(nb, n, d)
```

```python
import functools

import jax
import jax.numpy as jnp
import numpy as np
from jax import lax
from jax.experimental import pallas as pl
from jax.experimental.pallas import tpu as pltpu

F32 = jnp.float32
BF16 = jnp.bfloat16
HIGHEST = lax.Precision.HIGHEST

HEAD_DIM = 64
GRID_W = 64
A_HEADS, A_KV_HEADS = 8, 2
B_HEADS = 8
B_WIDTH = B_HEADS * HEAD_DIM
DECAY_LORA, ICLR_LORA, GATE_LORA = 64, 64, 128
C_HEADS, C_KV_HEADS = 16, 4
GQA_GROUP = 4
WINDOW = 128
N_MOD = 9
ROPE_THETA = 10000.0
EPS = 1e-6
GN_EPS = 64e-5
NEG_BIG = -1e30
LOG2_E = 1.4426950408889634

LANES = 128
SUBLANES = 8
VMEM_LIMIT_BYTES = 56 * 1024 * 1024

SCAN_CHUNK = 64
PAIR_ROWS = 2 * SCAN_CHUNK
TQ = 128
MOD_ROWS = 16


def _cparams(sem):
    return pltpu.CompilerParams(dimension_semantics=sem, vmem_limit_bytes=VMEM_LIMIT_BYTES)


def _dot(a, b, precision=None):
    return jnp.dot(a, b, preferred_element_type=F32, precision=precision)


def _dot_nt(a, b, precision=None):
    return lax.dot_general(a, b, (((1,), (1,)), ((), ())), preferred_element_type=F32, precision=precision)


def _dot_tn(a, b, precision=None):
    return lax.dot_general(a, b, (((0,), (0,)), ((), ())), preferred_element_type=F32, precision=precision)


def _rms(x):
    return x * lax.rsqrt(jnp.mean(x * x, axis=-1, keepdims=True) + EPS)


def _sigmoid(x):
    return 1.0 / (1.0 + jnp.exp(-x))


def _headsum(x, bd):
    hi = x.astype(BF16)
    lo = (x - hi.astype(F32)).astype(BF16)
    return _dot(hi, bd) + _dot(lo, bd)


def _mod_kernel(c_ref, w_ref, b_ref, o_ref):
    c = c_ref[...]
    o_ref[...] = _dot(c * _sigmoid(c), w_ref[...], HIGHEST) + b_ref[...]


def _mod_table(cvec, w_mod, b_mod):
    depth, d, nd = w_mod.shape
    tn = 1536 if nd % 1536 == 0 else nd
    return pl.pallas_call(
        _mod_kernel,
        grid=(depth, nd // tn),
        in_specs=[
            pl.BlockSpec((MOD_ROWS, d), lambda l, n: (0, 0)),
            pl.BlockSpec((None, d, tn), lambda l, n: (l, 0, n)),
            pl.BlockSpec((None, 1, tn), lambda l, n: (l, 0, n)),
        ],
        out_specs=pl.BlockSpec((None, MOD_ROWS, tn), lambda l, n: (l, 0, n)),
        out_shape=jax.ShapeDtypeStruct((depth, MOD_ROWS, nd), F32),
        compiler_params=_cparams(("parallel", "parallel")),
        name="mod_table",
    )(cvec, w_mod, b_mod.reshape(depth, 1, nd))


def _ffn_kernel(x_ref, mod_ref, wg_ref, wu_ref, wo_ref, *rest, i_shift, final):
    if final:
        fg_ref, o_ref, hn_ref, acc_ref = rest
    else:
        o_ref, hn_ref, acc_ref = rest
    f = pl.program_id(1)

    @pl.when(f == 0)
    def _():
        hn = _rms(x_ref[...]) * (1.0 + mod_ref[i_shift + 1:i_shift + 2, :]) + mod_ref[i_shift:i_shift + 1, :]
        hn_ref[...] = hn.astype(BF16)
        acc_ref[...] = jnp.zeros_like(acc_ref)

    hn = hn_ref[...]
    g = _dot(hn, wg_ref[...])
    u = _dot(hn, wu_ref[...])
    a = (g * _sigmoid(g) * u).astype(BF16)
    acc_ref[...] += _dot(a, wo_ref[...])

    @pl.when(f == pl.num_programs(1) - 1)
    def _():
        y = x_ref[...] + (0.5 * mod_ref[i_shift + 2:i_shift + 3, :]) * acc_ref[...]
        if final:
            y = _rms(y) * fg_ref[...]
        o_ref[...] = y


def _ffn(xs, mods, w_in, w_out, l, j, i_shift, dims, final_gain=None):
    rows, d = xs.shape
    dff = w_out.shape[2]
    tm, tf = dims["tm_ffn"], dims["tf"]
    nf = dff // tf
    tiles_per_batch = dims["n"] // tm
    nb = dims["b"]
    final = final_gain is not None
    if final:
        rows = nb * dims["n"]
    in_specs = [
        pl.BlockSpec((tm, d), lambda i, f: (i, 0)),
        pl.BlockSpec((None, None, N_MOD, d), lambda i, f: (l, jnp.minimum(i // tiles_per_batch, nb), 0, 0)),
        pl.BlockSpec((None, None, d, tf), lambda i, f: (l, j, 0, f)),
        pl.BlockSpec((None, None, d, tf), lambda i, f: (l, j, 0, f + nf)),
        pl.BlockSpec((None, None, tf, d), lambda i, f: (l, j, f, 0)),
    ]
    args = [xs, mods, w_in, w_in, w_out]
    if final:
        in_specs.append(pl.BlockSpec((1, d), lambda i, f: (0, 0)))
        args.append(final_gain.reshape(1, d))
    return pl.pallas_call(
        functools.partial(_ffn_kernel, i_shift=i_shift, final=final),
        grid=(rows // tm, nf),
        in_specs=in_specs,
        out_specs=pl.BlockSpec((tm, d), lambda i, f: (i, 0)),
        out_shape=jax.ShapeDtypeStruct((rows, d), F32),
        scratch_shapes=[pltpu.VMEM((tm, d), BF16), pltpu.VMEM((tm, d), F32)],
        compiler_params=_cparams(("parallel", "arbitrary")),
        name="ffn",
    )(*args)


def _rope(x, cos, sin_a, sin_b):
    outs = []
    for g in range(x.shape[1] // LANES):
        xg = x[:, g * LANES:(g + 1) * LANES]
        nxt = pltpu.roll(xg, LANES - 1, 1)
        prv = pltpu.roll(xg, 1, 1)
        outs.append(xg * cos + nxt * sin_a + prv * sin_b)
    return outs[0] if len(outs) == 1 else jnp.concatenate(outs, axis=1)


def _proj_kernel(x_ref, mod_ref, w_ref, cos_ref, sa_ref, sb_ref, *rest, qw, kw, fw, qk_norm):
    if qk_norm:
        qg_ref, kg_ref, bdq_ref, bdk_ref = rest[:4]
        rest = rest[4:]
    q_ref, k_ref, v_ref = rest[:3]
    hn = (_rms(x_ref[...]) * (1.0 + mod_ref[4:5, :]) + mod_ref[3:4, :]).astype(BF16)
    p = _dot(hn, w_ref[...])
    q = p[:, :qw]
    k = p[:, qw:qw + kw]
    if qk_norm:
        q = q * lax.rsqrt(_headsum(q * q, bdq_ref[...]) + EPS) * qg_ref[...]
        k = k * lax.rsqrt(_headsum(k * k, bdk_ref[...]) + EPS) * kg_ref[...]
    cos, sa, sb = cos_ref[...], sa_ref[...], sb_ref[...]
    q_ref[...] = (_rope(q, cos, sa, sb) * (HEAD_DIM ** -0.5 * LOG2_E)).T.astype(BF16)
    k_ref[...] = _rope(k, cos, sa, sb).astype(BF16)
    v_ref[...] = p[:, qw + kw:qw + 2 * kw].T.astype(BF16)
    if fw:
        rest[3][...] = p[:, qw + 2 * kw:]


def _proj(xs, mods, w_in, l, e, rope_tabs, dims, qw, kw, fw, norm_args=None):
    rows, d = xs.shape
    tm = dims["tm_proj"]
    cols = qw + 2 * kw + fw
    tiles_per_batch = dims["n"] // tm
    n_lat_tiles = dims["b"] * tiles_per_batch
    nb = dims["b"]

    def tab_idx(i):
        return (jnp.where(i < n_lat_tiles, i % tiles_per_batch, tiles_per_batch), 0)

    in_specs = [
        pl.BlockSpec((tm, d), lambda i: (i, 0)),
        pl.BlockSpec((None, None, N_MOD, d), lambda i: (l, jnp.minimum(i // tiles_per_batch, nb), 0, 0)),
        pl.BlockSpec((None, d, cols), lambda i: (e, 0, 0)),
        pl.BlockSpec((tm, LANES), tab_idx),
        pl.BlockSpec((tm, LANES), tab_idx),
        pl.BlockSpec((tm, LANES), tab_idx),
    ]
    args = [xs, mods, w_in, *rope_tabs]
    qk_norm = norm_args is not None
    if qk_norm:
        qg, kg, bdq, bdk = norm_args
        in_specs += [
            pl.BlockSpec((1, qw), lambda i: (0, 0)),
            pl.BlockSpec((1, kw), lambda i: (0, 0)),
            pl.BlockSpec((qw, qw), lambda i: (0, 0)),
            pl.BlockSpec((kw, kw), lambda i: (0, 0)),
        ]
        args += [qg, kg, bdq, bdk]
    out_specs = [
        pl.BlockSpec((qw, tm), lambda i: (0, i)),
        pl.BlockSpec((tm, kw), lambda i: (i, 0)),
        pl.BlockSpec((kw, tm), lambda i: (0, i)),
    ]
    out_shape = [
        jax.ShapeDtypeStruct((qw, rows), BF16),
        jax.ShapeDtypeStruct((rows, kw), BF16),
        jax.ShapeDtypeStruct((kw, rows), BF16),
    ]
    if fw:
        out_specs.append(pl.BlockSpec((tm, fw), lambda i: (i, 0)))
        out_shape.append(jax.ShapeDtypeStruct((rows, fw), F32))
    return pl.pallas_call(
        functools.partial(_proj_kernel, qw=qw, kw=kw, fw=fw, qk_norm=qk_norm),
        grid=(rows // tm,),
        in_specs=in_specs,
        out_specs=out_specs,
        out_shape=out_shape,
        compiler_params=_cparams(("parallel",)),
        name="mixer_proj",
    )(*args)


def _attend(qt_ref, segs, sink_ref, n_pad, o_ref):
    tq = qt_ref.shape[1]
    cols = GQA_GROUP * tq
    heads = (0, 1)
    qpad, m, l, acc = [], [], [], []
    for jj in heads:
        q4 = jnp.concatenate([qt_ref[(jj * GQA_GROUP + g) * HEAD_DIM:(jj * GQA_GROUP + g + 1) * HEAD_DIM, :]
                              for g in range(GQA_GROUP)], axis=1)
        z = jnp.zeros_like(q4)
        qpad.append(jnp.concatenate([q4, z] if jj == 0 else [z, q4], axis=0))
        if sink_ref is not None:
            mj, lj = sink_ref[jj], jnp.ones((1, cols), F32)
        else:
            mj, lj = jnp.full((1, cols), NEG_BIG, F32), jnp.zeros((1, cols), F32)
        if n_pad is not None:
            m_new = jnp.maximum(mj, jnp.where(n_pad > 0.0, 0.0, NEG_BIG))
            lj = lj * jnp.exp2(mj - m_new) + jnp.where(n_pad > 0.0, n_pad * jnp.exp2(-m_new), 0.0)
            mj = m_new
        m.append(mj)
        l.append(lj)
        acc.append(jnp.zeros((LANES, cols), F32))
    for k, vt, valid in segs:
        if valid is not None:
            valid = jnp.concatenate([valid] * GQA_GROUP, axis=1)
        for jj in heads:
            s = _dot(k, qpad[jj])
            if valid is not None:
                s = jnp.where(valid, s, NEG_BIG)
            m_new = jnp.maximum(m[jj], jnp.max(s, axis=0, keepdims=True))
            alpha = jnp.exp2(m[jj] - m_new)
            p = jnp.exp2(s - m_new)
            l[jj] = alpha * l[jj] + jnp.sum(p, axis=0, keepdims=True)
            acc[jj] = alpha * acc[jj] + _dot(vt, p.astype(BF16))
            m[jj] = m_new
    blocks = []
    for jj in heads:
        ot = acc[jj][jj * HEAD_DIM:(jj + 1) * HEAD_DIM, :] * (1.0 / l[jj])
        for g in range(0, GQA_GROUP, 2):
            blocks.append(jnp.concatenate([ot[:, g * tq:(g + 1) * tq], ot[:, (g + 1) * tq:(g + 2) * tq]], axis=0).T)
    o_ref[...] = jnp.concatenate(blocks, axis=1).astype(o_ref.dtype)


def _attn_global_kernel(qt_ref, kl_ref, vl_ref, kc_ref, vc_ref, o_ref, *, n_q_lat, key_chunk):
    i = pl.program_id(2)
    ctx_seg = (kc_ref[...], vc_ref[...], None)

    @pl.when(i < n_q_lat)
    def _():
        segs = [(kl_ref[c * key_chunk:(c + 1) * key_chunk, :], vl_ref[:, c * key_chunk:(c + 1) * key_chunk], None)
                for c in range(kl_ref.shape[0] // key_chunk)]
        _attend(qt_ref, segs + [ctx_seg], None, None, o_ref)

    @pl.when(i >= n_q_lat)
    def _():
        _attend(qt_ref, [ctx_seg], None, None, o_ref)


def _attn_window_kernel(qt_ref, k0_ref, k1_ref, k2_ref, v0_ref, v1_ref, v2_ref, kc_ref, vc_ref, sink_ref, o_ref,
                        *, n, n_q_lat):
    i = pl.program_id(2)
    tq = qt_ref.shape[1]
    ctx_seg = (kc_ref[...], vc_ref[...], None)

    @pl.when(i < n_q_lat)
    def _():
        k = jnp.concatenate([k0_ref[...], k1_ref[...], k2_ref[...]], axis=0)
        vt = jnp.concatenate([v0_ref[...], v1_ref[...], v2_ref[...]], axis=1)
        kpos = (i - 1) * tq + lax.broadcasted_iota(jnp.int32, (3 * tq, tq), 0)
        qpos = i * tq + lax.broadcasted_iota(jnp.int32, (3 * tq, tq), 1)
        valid = (jnp.abs(kpos - qpos) <= WINDOW) & (kpos >= WINDOW) & (kpos < n)
        q_row = i * tq + lax.broadcasted_iota(jnp.int32, (1, tq), 1)
        n_pad = jnp.maximum(q_row + WINDOW - n + 1, 0).astype(F32)
        n_pad = jnp.concatenate([n_pad] * GQA_GROUP, axis=1)
        _attend(qt_ref, [(k, vt, valid), ctx_seg], sink_ref, n_pad, o_ref)

    @pl.when(i >= n_q_lat)
    def _():
        _attend(qt_ref, [ctx_seg], sink_ref, None, o_ref)


def _attention(qt, k, vt, dims, sink=None):
    qw, rows = qt.shape
    kw = k.shape[1]
    n, lc, nb = dims["n"], dims["l"], dims["b"]
    n_pairs = kw // LANES
    n_q_lat = n // TQ
    n_q_ctx = lc // TQ
    lat_blocks = nb * n_q_lat
    ctx_block = nb * n // lc
    qrows = 2 * GQA_GROUP * HEAD_DIM

    def q_blk(b, i):
        return jnp.where(i < n_q_lat, b * n_q_lat + i, lat_blocks + b * n_q_ctx + (i - n_q_lat))

    ctx_specs = [pl.BlockSpec((lc, LANES), lambda b, p, i: (ctx_block + b, p)),
                 pl.BlockSpec((LANES, lc), lambda b, p, i: (p, ctx_block + b))]
    qt_spec = pl.BlockSpec((qrows, TQ), lambda b, p, i: (p, q_blk(b, i)))
    if sink is None:
        body = functools.partial(_attn_global_kernel, n_q_lat=n_q_lat, key_chunk=dims["key_chunk"])
        in_specs = [qt_spec,
                    pl.BlockSpec((n, LANES), lambda b, p, i: (b, p)),
                    pl.BlockSpec((LANES, n), lambda b, p, i: (p, b))] + ctx_specs
        args = [qt, k, vt, k, vt]
    else:
        assert WINDOW == TQ
        body = functools.partial(_attn_window_kernel, n=n, n_q_lat=n_q_lat)
        nbr = lambda b, i, off: b * n_q_lat + jnp.clip(i + off, 0, n_q_lat - 1)
        in_specs = ([qt_spec]
                    + [pl.BlockSpec((TQ, LANES), functools.partial(lambda b, p, i, off: (nbr(b, i, off), p), off=off))
                       for off in (-1, 0, 1)]
                    + [pl.BlockSpec((LANES, TQ), functools.partial(lambda b, p, i, off: (p, nbr(b, i, off)), off=off))
                       for off in (-1, 0, 1)]
                    + ctx_specs
                    + [pl.BlockSpec((None, 2, 1, GQA_GROUP * TQ), lambda b, p, i: (p, 0, 0, 0))])
        sink_rows = jnp.repeat(sink.astype(F32).reshape(n_pairs, 2, 1, GQA_GROUP) * LOG2_E, TQ, axis=3)
        args = [qt, k, k, k, vt, vt, vt, k, vt, sink_rows]
    return pl.pallas_call(
        body,
        grid=(nb, n_pairs, n_q_lat + n_q_ctx),
        in_specs=in_specs,
        out_specs=pl.BlockSpec((TQ, qrows), lambda b, p, i: (q_blk(b, i), p)),
        out_shape=jax.ShapeDtypeStruct((rows, qw), BF16),
        compiler_params=_cparams(("parallel", "parallel", "arbitrary")),
        name="gqa_global" if sink is None else "gqa_window",
    )(*args)


def _feat_kernel(f_ref, fp_ref, fn_ref, mu_ref, w0_ref, w2_ref, a0_ref, a2_ref, g2_ref, kk_ref, ka_ref, bd_ref,
                 r_o, k_o, v_o, kk_o, g_o, lw0_o, lw1_o, kd0_o, kd1_o, b0_o, b1_o,
                 *, n_lat_tiles, tiles_lat, tiles_ctx):
    i = pl.program_id(0)
    t = f_ref.shape[0]
    bw = B_WIDTH
    is_lat = i < n_lat_tiles
    pos = jnp.where(is_lat, i % tiles_lat, (i - n_lat_tiles) % tiles_ctx)
    last = jnp.where(is_lat, tiles_lat, tiles_ctx) - 1
    f = f_ref[...]
    prow = jnp.where(pos == 0, 0.0, fp_ref[SUBLANES - 1:SUBLANES, :])
    nrow = jnp.where(pos == last, 0.0, fn_ref[0:1, :])
    rid = lax.broadcasted_iota(jnp.int32, (t, 1), 0)
    prev = jnp.where(rid == 0, prow, pltpu.roll(f, 1, 0))
    nxt = jnp.where(rid == t - 1, nrow, pltpu.roll(f, t - 1, 0))
    fs = f + mu_ref[0:1, :] * (prev - f) + mu_ref[1:2, :] * (nxt - f)

    r = fs[:, :bw]
    k = fs[:, bw:2 * bw]
    v = fs[:, 2 * bw:3 * bw]
    wl = jnp.tanh(fs[:, 3 * bw:3 * bw + 2 * DECAY_LORA])
    al = fs[:, 3 * bw + 2 * DECAY_LORA:3 * bw + 2 * DECAY_LORA + 2 * ICLR_LORA]
    gl = fs[:, 3 * bw + 2 * DECAY_LORA + 2 * ICLR_LORA:]

    kk = k * kk_ref[...]
    kk = kk * lax.rsqrt(_headsum(kk * kk, bd_ref[...]) + EPS)
    r_o[...] = r
    k_o[...] = k
    v_o[...] = v
    kk_o[...] = kk
    g_o[...] = _dot(_sigmoid(gl).astype(BF16), g2_ref[...])
    wl = wl.astype(BF16)
    al = al.astype(BF16)
    for d, (lw_o, kd_o, b_o) in enumerate(((lw0_o, kd0_o, b0_o), (lw1_o, kd1_o, b1_o))):
        w_log = w0_ref[d:d + 1, :] + _dot(wl, w2_ref[d])
        z = -w_log
        softplus = jnp.maximum(z, 0.0) + jnp.log(1.0 + jnp.exp(-jnp.abs(z)))
        lw_o[...] = -jnp.exp(-softplus - 0.5)
        a = _sigmoid(a0_ref[d:d + 1, :] + _dot(al, a2_ref[d]))
        kd_o[...] = k * (1.0 + (a - 1.0) * ka_ref[...])
        b_o[...] = kk * a


def _rwkv_features(f, params, dims):
    rows, fw = f.shape
    t = dims["tm_feat"]
    n, lc, nb = dims["n"], dims["l"], dims["b"]
    tiles_lat, tiles_ctx = n // t, lc // t
    n_lat_tiles = nb * tiles_lat
    hb = t // SUBLANES
    n_halo = rows // SUBLANES
    mu, w0, w2p, a0, a2p, g2, k_k, k_a, bd = params
    bw = B_WIDTH
    const = lambda shape: pl.BlockSpec(shape, lambda i: (0,) * len(shape))
    in_specs = [
        pl.BlockSpec((t, fw), lambda i: (i, 0)),
        pl.BlockSpec((SUBLANES, fw), lambda i: (jnp.maximum(i * hb - 1, 0), 0)),
        pl.BlockSpec((SUBLANES, fw), lambda i: (jnp.minimum((i + 1) * hb, n_halo - 1), 0)),
        const((2, fw)), const((2, bw)), const((2, 2 * DECAY_LORA, bw)), const((2, bw)),
        const((2, 2 * ICLR_LORA, bw)), const((GATE_LORA, bw)), const((1, bw)), const((1, bw)), const((bw, bw)),
    ]
    out = jax.ShapeDtypeStruct((rows, bw), F32)
    return pl.pallas_call(
        functools.partial(_feat_kernel, n_lat_tiles=n_lat_tiles, tiles_lat=tiles_lat, tiles_ctx=tiles_ctx),
        grid=(rows // t,),
        in_specs=in_specs,
        out_specs=[pl.BlockSpec((t, bw), lambda i: (i, 0))] * 11,
        out_shape=[out] * 11,
        compiler_params=_cparams(("parallel",)),
        name="rwkv_features",
    )(f, f, f, mu, w0, w2p, a0, a2p, g2, k_k, k_a, bd)


def _stack_heads(x, head0):
    return jnp.concatenate([jnp.where(head0, x, 0.0), jnp.where(head0, 0.0, x)], axis=0).astype(BF16)


def _cumsum_rows(tri, x):
    hi = x.astype(BF16)
    r1 = x - hi.astype(F32)
    mid = r1.astype(BF16)
    lo = (r1 - mid.astype(F32)).astype(BF16)
    return _dot(tri, hi) + _dot(tri, mid) + _dot(tri, lo)


def _wkv_units(units, eye, lvl_ref):
    c = units[0]["v"].shape[0]
    p2 = 2 * c
    head0 = lax.broadcasted_iota(jnp.int32, (1, LANES), 1) < HEAD_DIM
    st = lambda x: _stack_heads(x, head0)
    fold = lambda x: x[:c] + x[c:]
    each = lambda fn: [fn(u) for u in units]

    for u in units:
        u["kq_s"], u["rq_s"], u["v_s"] = st(u["kq"]), st(u["rq"]), st(u["v"])
    a = each(lambda u: _dot_nt(jnp.concatenate([u["kq_s"], u["rq_s"]], axis=0),
                               jnp.concatenate([st(u["bi"]), st(u["ki"])], axis=0)))
    for u, au in zip(units, a):
        u["nkb"] = jnp.where(u["m_strict"], au[:p2, :p2], 0.0)
        u["akk"] = jnp.where(u["m_strict"], au[:p2, p2:], 0.0).astype(BF16)
        u["arb"] = jnp.where(u["m_incl"], au[p2:, :p2], 0.0).astype(BF16)
        u["ark"] = jnp.where(u["m_incl"], au[p2:, p2:], 0.0).astype(BF16)

    nkb_h = each(lambda u: u["nkb"].astype(BF16))
    tinv = each(lambda u: eye - u["nkb"] * lvl_ref[0].astype(F32))
    for lev in range(1, lvl_ref.shape[0]):
        t_h = [t.astype(BF16) for t in tinv]
        lt = [_dot(nh * lvl_ref[lev], th).astype(BF16) for nh, th in zip(nkb_h, t_h)]
        tinv = [t - _dot(th, x) for t, th, x in zip(tinv, t_h, lt)]
    t_h = [t.astype(BF16) for t in tinv]

    akkv = each(lambda u: _dot(u["akk"], u["v_s"]).astype(BF16))
    gp_h = [_dot(th, jnp.concatenate([u["kq_s"], x], axis=1)).astype(BF16)
            for u, th, x in zip(units, t_h, akkv)]
    corr = [_dot(u["arb"], g) for u, g in zip(units, gp_h)]
    arkv = each(lambda u: _dot(u["ark"], u["v_s"]))
    btg = [_dot_tn(st(u["bt"]), g) for u, g in zip(units, gp_h)]
    ktv = each(lambda u: _dot_tn(st(u["kt"]), u["v_s"]))

    outs = []
    for u, cr, av, bg, kv in zip(units, corr, arkv, btg, ktv):
        rq = u["rq"]
        rq2 = fold(jnp.concatenate([jnp.where(head0, rq, 0.0), jnp.where(head0, 0.0, rq)], axis=0) - cr[:, :LANES])
        yl = fold(av - cr[:, LANES:])
        m_state = jnp.where(eye > 0.0, u["decay"], 0.0) - bg[:, :LANES]
        n_state = kv - bg[:, LANES:]
        h_h = u["h"].astype(BF16)
        y = _dot(rq2.astype(BF16), h_h) + yl
        h_new = _dot(m_state.astype(BF16), h_h) + n_state
        outs.append((y, h_new))
    return outs


def _scan_kernel(rf, vf, kkf, lwf, kdf, bf, rb, vb, kkb, lwb, kdb, bb, tri_ref, msk_ref, lvl_ref,
                 yf_o, yb_o, hf_s, hb_s):
    @pl.when(pl.program_id(1) == 0)
    def _():
        hf_s[...] = jnp.zeros_like(hf_s)
        hb_s[...] = jnp.zeros_like(hb_s)

    eye = msk_ref[0]
    units, sinks = [], []
    for d, (refs, y_o, h_s) in enumerate((((rf, vf, kkf, lwf, kdf, bf), yf_o, hf_s),
                                          ((rb, vb, kkb, lwb, kdb, bb), yb_o, hb_s))):
        m_strict = msk_ref[1 + 2 * d] > 0.0
        m_incl = msk_ref[2 + 2 * d] > 0.0
        r, v, kk, lw, kd, b = (ref[...] for ref in refs)
        cw = _cumsum_rows(tri_ref[d], lw)
        tot = jnp.sum(lw, axis=0, keepdims=True)
        w_inv = jnp.exp(-cw)
        w_rest = jnp.exp(tot - cw)
        facs = {"kq": kk * jnp.exp(cw - lw), "rq": r * jnp.exp(cw), "bi": b * w_inv, "ki": kd * w_inv,
                "bt": b * w_rest, "kt": kd * w_rest, "v": v, "decay": jnp.exp(tot)}
        for p in range(B_WIDTH // LANES):
            ls = slice(p * LANES, (p + 1) * LANES)
            unit = {name: x[:, ls] for name, x in facs.items()}
            unit.update(h=h_s[p], m_strict=m_strict, m_incl=m_incl)
            units.append(unit)
            sinks.append((y_o, h_s, p, ls))
    for (y, h_new), (y_o, h_s, p, ls) in zip(_wkv_units(units, eye, lvl_ref), sinks):
        y_o[:, ls] = y
        h_s[p] = h_new


def _scan_constants():
    c, p2 = SCAN_CHUNK, PAIR_ROWS
    t = np.arange(c)
    tri = np.stack([(t[None, :] <= t[:, None]), (t[None, :] >= t[:, None])]).astype(np.float32)
    i = np.arange(p2)
    same_head = (i[:, None] // c) == (i[None, :] // c)
    ti, si = i[:, None] % c, i[None, :] % c
    msk = np.stack([
        np.eye(p2, dtype=bool),
        same_head & (si < ti), same_head & (si <= ti),
        same_head & (si > ti), same_head & (si >= ti),
    ]).astype(np.float32)
    n_lev = int(np.log2(c))
    lvl = np.stack([
        ((i[:, None] >> (k + 1)) == (i[None, :] >> (k + 1))) & ((i[:, None] >> k) != (i[None, :] >> k))
        for k in range(n_lev)
    ]).astype(np.float32)
    return jnp.asarray(tri, BF16), jnp.asarray(msk), jnp.asarray(lvl, BF16)


def _wkv_scan(feats, dims):
    r, v, kk, lw0, lw1, kd0, kd1, b0, b1 = feats
    rows, bw = r.shape
    c = SCAN_CHUNK
    n, lc, nb = dims["n"], dims["l"], dims["b"]
    n_c, l_c = n // c, lc // c
    ctx_base = nb * n_c
    tri, msk, lvl = _scan_constants()

    def fwd(b, s):
        return (jnp.where(s < l_c, ctx_base + b * l_c + s, b * n_c + (s - l_c)), 0)

    def bwd(b, s):
        return (jnp.where(s < l_c, ctx_base + b * l_c + (l_c - 1 - s), b * n_c + (n_c - 1 - (s - l_c))), 0)

    const = lambda a: pl.BlockSpec(a.shape, lambda b, s: (0,) * a.ndim)
    out = jax.ShapeDtypeStruct((rows, bw), F32)
    return pl.pallas_call(
        _scan_kernel,
        grid=(nb, l_c + n_c),
        in_specs=[pl.BlockSpec((c, bw), fwd)] * 6 + [pl.BlockSpec((c, bw), bwd)] * 6
                 + [const(tri), const(msk), const(lvl)],
        out_specs=[pl.BlockSpec((c, bw), fwd), pl.BlockSpec((c, bw), bwd)],
        out_shape=[out, out],
        scratch_shapes=[pltpu.VMEM((bw // LANES, LANES, LANES), F32)] * 2,
        compiler_params=_cparams(("parallel", "arbitrary")),
        name="wkv_scan",
    )(r, v, kk, lw0, kd0, b0, r, v, kk, lw1, kd1, b1, tri, msk, lvl)


def _outproj_kernel(x_ref, mod_ref, wo_ref, oa_ref, *rest, even):
    o_ref = rest[-1]
    if even:
        yf, yb, r, k, v, g, rk, gnw, gnb, bd = rest[:-1]
        y = yf[...] + yb[...]
        mean = _headsum(y, bd[...]) * (1.0 / HEAD_DIM)
        yc = y - mean
        var = _headsum(yc * yc, bd[...]) * (1.0 / HEAD_DIM)
        yn = yc * lax.rsqrt(var + GN_EPS) * gnw[...] + gnb[...]
        bonus = _headsum(r[...] * k[...] * rk[...], bd[...]) * v[...]
        ob = ((yn + bonus) * g[...]).astype(BF16)
        half = oa_ref.shape[1]
        o = _dot(oa_ref[...], wo_ref[:half, :]) + _dot(ob, wo_ref[half:, :])
    else:
        o = _dot(oa_ref[...], wo_ref[...])
    o_ref[...] = x_ref[...] + mod_ref[5:6, :] * o


def _outproj(xs, mods, w_out, l, e, o_att, dims, readout=None):
    rows, d = xs.shape
    tm = dims["tm_proj"]
    tiles_per_batch = dims["n"] // tm
    nb = dims["b"]
    even = readout is not None
    in_specs = [
        pl.BlockSpec((tm, d), lambda i: (i, 0)),
        pl.BlockSpec((None, None, N_MOD, d), lambda i: (l, jnp.minimum(i // tiles_per_batch, nb), 0, 0)),
        pl.BlockSpec((None, d, d), lambda i: (e, 0, 0)),
        pl.BlockSpec((tm, o_att.shape[1]), lambda i: (i, 0)),
    ]
    args = [xs, mods, w_out, o_att]
    if even:
        yf, yb, r, k, v, g, rk, gnw, gnb, bd = readout
        bw = B_WIDTH
        in_specs += [pl.BlockSpec((tm, bw), lambda i: (i, 0))] * 6
        in_specs += [pl.BlockSpec((1, bw), lambda i: (0, 0))] * 3 + [pl.BlockSpec((bw, bw), lambda i: (0, 0))]
        args += [yf, yb, r, k, v, g, rk, gnw, gnb, bd]
    return pl.pallas_call(
        functools.partial(_outproj_kernel, even=even),
        grid=(rows // tm,),
        in_specs=in_specs,
        out_specs=pl.BlockSpec((tm, d), lambda i: (i, 0)),
        out_shape=jax.ShapeDtypeStruct((rows, d), F32),
        compiler_params=_cparams(("parallel",)),
        name="mixer_out",
    )(*args)


def _rope_tables(n, tm):
    rows = n // GRID_W
    row = jnp.repeat(jnp.arange(rows, dtype=F32), GRID_W)
    col = jnp.tile(jnp.arange(GRID_W, dtype=F32), rows)
    n_freq = HEAD_DIM // 4
    inv_freq = ROPE_THETA ** (-jnp.arange(n_freq, dtype=F32) / n_freq)
    ang = jnp.concatenate([row[:, None] * inv_freq, col[:, None] * inv_freq], axis=-1)
    cos = jnp.repeat(jnp.cos(ang), 2, axis=-1)
    sin = jnp.repeat(jnp.sin(ang), 2, axis=-1)
    even_lane = (jnp.arange(HEAD_DIM) % 2 == 0)[None, :]
    sin_a = jnp.where(even_lane, -sin, 0.0)
    sin_b = jnp.where(even_lane, 0.0, sin)
    pad = lambda tab, fill: jnp.concatenate(
        [jnp.tile(tab, (1, LANES // HEAD_DIM)), jnp.full((tm, LANES), fill, F32)], axis=0)
    return pad(cos, 1.0), pad(sin_a, 0.0), pad(sin_b, 0.0)


def _block_diag(width, value, dtype):
    h = np.arange(width) // HEAD_DIM
    return jnp.asarray((h[:, None] == h[None, :]) * value, dtype)


def _pad_lora(w):
    r = w.shape[1]
    z = jnp.zeros_like(w[0])
    return jnp.stack([jnp.concatenate([w[0], z], axis=0), jnp.concatenate([z, w[1]], axis=0)])


def _largest_tile(limit, *sizes):
    t = limit
    while any(s % t for s in sizes):
        t //= 2
    return t


def kernel(x, c, ctx, c_ctx, w_mod, b_mod, ffn_in, ffn_out, even_w_in, even_w_out, q_gain, k_gain, rwkv_mu, rwkv_w0,
           rwkv_w2, rwkv_a0, rwkv_a2, rwkv_g2, rwkv_k_k, rwkv_k_a, rwkv_r_k, rwkv_gn_w, rwkv_gn_b, odd_w_in, odd_w_out,
           sink, final_gain):
    nb, n, d = x.shape
    lc = ctx.shape[1]
    depth = w_mod.shape[0]
    dff = ffn_out.shape[2]
    assert n % TQ == 0 and lc % TQ == 0 and n >= TQ + 2 * WINDOW and n % GRID_W == 0
    assert n % SCAN_CHUNK == 0 and lc % SCAN_CHUNK == 0 and nb + 1 <= MOD_ROWS
    dims = {
        "b": nb, "n": n, "l": lc,
        "tm_ffn": _largest_tile(512, n, nb * lc),
        "tm_proj": _largest_tile(512, n, nb * lc),
        "tm_feat": _largest_tile(256, n, lc),
        "tf": 1408 if dff % 1408 == 0 else dff,
        "key_chunk": _largest_tile(2048, n),
    }

    cvec = jnp.zeros((MOD_ROWS, d), F32).at[:nb].set(c).at[nb].set(c_ctx)
    mods = _mod_table(cvec, w_mod, b_mod).reshape(depth, MOD_ROWS, N_MOD, d)

    ffn_in_h = ffn_in.astype(BF16)
    ffn_out_h = ffn_out.astype(BF16)
    even_in_h = even_w_in.astype(BF16)
    even_out_h = even_w_out.astype(BF16)
    odd_in_h = odd_w_in.astype(BF16)
    odd_out_h = odd_w_out.astype(BF16)

    rope_tabs = _rope_tables(n, dims["tm_proj"])
    a_kw = A_KV_HEADS * HEAD_DIM
    a_qw = A_HEADS * HEAD_DIM
    bd_q = _block_diag(a_qw, 1.0 / HEAD_DIM, BF16)
    bd_k = _block_diag(a_kw, 1.0 / HEAD_DIM, BF16)
    bd_ones = _block_diag(B_WIDTH, 1.0, BF16)
    fw = even_w_in.shape[2] - a_qw - 2 * a_kw

    xs = jnp.concatenate([x.reshape(nb * n, d), ctx.reshape(nb * lc, d)], axis=0)
    for l in range(depth):
        last = l == depth - 1
        xs = _ffn(xs, mods, ffn_in_h, ffn_out_h, l, 0, 0, dims)
        if l % 2 == 0:
            e = l // 2
            tile_h = lambda g, reps: jnp.tile(g, reps).reshape(1, -1)
            norm_args = (tile_h(q_gain[e], A_HEADS), tile_h(k_gain[e], A_KV_HEADS), bd_q, bd_k)
            q, k, v, f = _proj(xs, mods, even_in_h, l, e, rope_tabs, dims, a_qw, a_kw, fw, norm_args)
            oa = _attention(q, k, v, dims)
            feat_params = (rwkv_mu[e], rwkv_w0[e], _pad_lora(rwkv_w2[e]).astype(BF16), rwkv_a0[e],
                           _pad_lora(rwkv_a2[e]).astype(BF16), rwkv_g2[e].astype(BF16),
                           rwkv_k_k[e].reshape(1, -1), rwkv_k_a[e].reshape(1, -1), bd_ones)
            r, kr, vr, kk, g, lw0, lw1, kd0, kd1, b0, b1 = _rwkv_features(f, feat_params, dims)
            yf, yb = _wkv_scan((r, vr, kk, lw0, lw1, kd0, kd1, b0, b1), dims)
            readout = (yf, yb, r, kr, vr, g, rwkv_r_k[e].reshape(1, -1), rwkv_gn_w[e].reshape(1, -1),
                       rwkv_gn_b[e].reshape(1, -1), bd_ones)
            xs = _outproj(xs, mods, even_out_h, l, e, oa, dims, readout)
        else:
            o = l // 2
            c_kw = C_KV_HEADS * HEAD_DIM
            c_qw = C_HEADS * HEAD_DIM
            q, k, v = _proj(xs, mods, odd_in_h, l, o, rope_tabs, dims, c_qw, c_kw, 0)
            oc = _attention(q, k, v, dims, sink=sink[o])
            xs = _outproj(xs, mods, odd_out_h, l, o, oc, dims)
        xs = _ffn(xs, mods, ffn_in_h, ffn_out_h, l, 1, 6, dims, final_gain=final_gain if last else None)
    return xs.reshape(nb, n, d)
```

```python
import functools

import jax
import jax.numpy as jnp
import numpy as np
from jax import lax
from jax.experimental import pallas as pl
from jax.experimental.pallas import tpu as pltpu

F32 = jnp.float32
BF16 = jnp.bfloat16
HIGHEST = lax.Precision.HIGHEST

HEAD_DIM = 64
GRID_W = 64
A_HEADS, A_KV_HEADS = 8, 2
B_HEADS = 8
B_WIDTH = B_HEADS * HEAD_DIM
DECAY_LORA, ICLR_LORA, GATE_LORA = 64, 64, 128
C_HEADS, C_KV_HEADS = 16, 4
GQA_GROUP = 4
WINDOW = 128
N_MOD = 9
ROPE_THETA = 10000.0
EPS = 1e-6
GN_EPS = 64e-5
NEG_BIG = -1e30
LOG2_E = 1.4426950408889634

LANES = 128
SUBLANES = 8
VMEM_LIMIT_BYTES = 56 * 1024 * 1024

SCAN_CHUNK = 64
SCAN_STEP_CHUNKS = 2
PAIR_ROWS = 2 * SCAN_CHUNK
TQ = 128
MOD_ROWS = 16


def _cparams(sem):
    return pltpu.CompilerParams(dimension_semantics=sem, vmem_limit_bytes=VMEM_LIMIT_BYTES)


def _dot(a, b, precision=None):
    return jnp.dot(a, b, preferred_element_type=F32, precision=precision)


def _dot_nt(a, b, precision=None):
    return lax.dot_general(a, b, (((1,), (1,)), ((), ())), preferred_element_type=F32, precision=precision)


def _dot_tn(a, b, precision=None):
    return lax.dot_general(a, b, (((0,), (0,)), ((), ())), preferred_element_type=F32, precision=precision)


def _rms(x):
    return x * lax.rsqrt(jnp.mean(x * x, axis=-1, keepdims=True) + EPS)


def _sigmoid(x):
    return 1.0 / (1.0 + jnp.exp(-x))


def _headsum(x, bd):
    hi = x.astype(BF16)
    lo = (x - hi.astype(F32)).astype(BF16)
    return _dot(hi, bd) + _dot(lo, bd)


def _mod_kernel(c_ref, w_ref, b_ref, o_ref):
    c = c_ref[...]
    o_ref[...] = _dot(c * _sigmoid(c), w_ref[...], HIGHEST) + b_ref[...]


def _mod_table(cvec, w_mod, b_mod):
    depth, d, nd = w_mod.shape
    tn = 1536 if nd % 1536 == 0 else nd
    return pl.pallas_call(
        _mod_kernel,
        grid=(depth, nd // tn),
        in_specs=[
            pl.BlockSpec((MOD_ROWS, d), lambda l, n: (0, 0)),
            pl.BlockSpec((None, d, tn), lambda l, n: (l, 0, n)),
            pl.BlockSpec((None, 1, tn), lambda l, n: (l, 0, n)),
        ],
        out_specs=pl.BlockSpec((None, MOD_ROWS, tn), lambda l, n: (l, 0, n)),
        out_shape=jax.ShapeDtypeStruct((depth, MOD_ROWS, nd), F32),
        compiler_params=_cparams(("parallel", "parallel")),
        name="mod_table",
    )(cvec, w_mod, b_mod.reshape(depth, 1, nd))


def _ffn_kernel(x_ref, mod_ref, wg_ref, wu_ref, wo_ref, *rest, i_shift, final):
    o_ref = rest[-1]
    x = x_ref[...]
    hn = (_rms(x) * (1.0 + mod_ref[i_shift + 1:i_shift + 2, :]) + mod_ref[i_shift:i_shift + 1, :]).astype(BF16)
    g = _dot(hn, wg_ref[...])
    u = _dot(hn, wu_ref[...])
    a = (g * _sigmoid(g) * u).astype(BF16)
    y = x + (0.5 * mod_ref[i_shift + 2:i_shift + 3, :]) * _dot(a, wo_ref[...])
    if final:
        y = _rms(y) * rest[0][...]
    o_ref[...] = y


def _ffn(xs, mods, w_in, w_out, l, j, i_shift, dims, final_gain=None):
    rows, d = xs.shape
    dff = w_out.shape[2]
    tm = dims["tm_ffn"]
    tiles_per_batch = dims["n"] // tm
    nb = dims["b"]
    final = final_gain is not None
    if final:
        rows = nb * dims["n"]
    in_specs = [
        pl.BlockSpec((tm, d), lambda i: (i, 0)),
        pl.BlockSpec((None, None, N_MOD, d), lambda i: (l, jnp.minimum(i // tiles_per_batch, nb), 0, 0)),
        pl.BlockSpec((None, None, d, dff), lambda i: (l, j, 0, 0)),
        pl.BlockSpec((None, None, d, dff), lambda i: (l, j, 0, 1)),
        pl.BlockSpec((None, None, dff, d), lambda i: (l, j, 0, 0)),
    ]
    args = [xs, mods, w_in, w_in, w_out]
    if final:
        in_specs.append(pl.BlockSpec((1, d), lambda i: (0, 0)))
        args.append(final_gain.reshape(1, d))
    return pl.pallas_call(
        functools.partial(_ffn_kernel, i_shift=i_shift, final=final),
        grid=(rows // tm,),
        in_specs=in_specs,
        out_specs=pl.BlockSpec((tm, d), lambda i: (i, 0)),
        out_shape=jax.ShapeDtypeStruct((rows, d), F32),
        compiler_params=_cparams(("parallel",)),
        name="ffn",
    )(*args)


def _rope(x, cos, sin_a, sin_b):
    outs = []
    for g in range(x.shape[1] // LANES):
        xg = x[:, g * LANES:(g + 1) * LANES]
        nxt = pltpu.roll(xg, LANES - 1, 1)
        prv = pltpu.roll(xg, 1, 1)
        outs.append(xg * cos + nxt * sin_a + prv * sin_b)
    return outs[0] if len(outs) == 1 else jnp.concatenate(outs, axis=1)


def _proj_kernel(x_ref, mod_ref, w_ref, cos_ref, sa_ref, sb_ref, *rest, qw, kw, fw, qk_norm):
    if qk_norm:
        qg_ref, kg_ref, bdq_ref, bdk_ref = rest[:4]
        rest = rest[4:]
    q_ref, k_ref, v_ref = rest[:3]
    hn = (_rms(x_ref[...]) * (1.0 + mod_ref[4:5, :]) + mod_ref[3:4, :]).astype(BF16)
    p = _dot(hn, w_ref[...])
    q = p[:, :qw]
    k = p[:, qw:qw + kw]
    if qk_norm:
        q = q * lax.rsqrt(_headsum(q * q, bdq_ref[...]) + EPS) * qg_ref[...]
        k = k * lax.rsqrt(_headsum(k * k, bdk_ref[...]) + EPS) * kg_ref[...]
    cos, sa, sb = cos_ref[...], sa_ref[...], sb_ref[...]
    q_ref[...] = (_rope(q, cos, sa, sb) * (HEAD_DIM ** -0.5 * LOG2_E)).T.astype(BF16)
    k_ref[...] = _rope(k, cos, sa, sb).astype(BF16)
    v_ref[...] = p[:, qw + kw:qw + 2 * kw].T.astype(BF16)
    if fw:
        rest[3][...] = p[:, qw + 2 * kw:]


def _proj(xs, mods, w_in, l, e, rope_tabs, dims, qw, kw, fw, norm_args=None):
    rows, d = xs.shape
    tm = dims["tm_proj"]
    cols = qw + 2 * kw + fw
    tiles_per_batch = dims["n"] // tm
    n_lat_tiles = dims["b"] * tiles_per_batch
    nb = dims["b"]

    def tab_idx(i):
        return (jnp.where(i < n_lat_tiles, i % tiles_per_batch, tiles_per_batch), 0)

    in_specs = [
        pl.BlockSpec((tm, d), lambda i: (i, 0)),
        pl.BlockSpec((None, None, N_MOD, d), lambda i: (l, jnp.minimum(i // tiles_per_batch, nb), 0, 0)),
        pl.BlockSpec((None, d, cols), lambda i: (e, 0, 0)),
        pl.BlockSpec((tm, LANES), tab_idx),
        pl.BlockSpec((tm, LANES), tab_idx),
        pl.BlockSpec((tm, LANES), tab_idx),
    ]
    args = [xs, mods, w_in, *rope_tabs]
    qk_norm = norm_args is not None
    if qk_norm:
        qg, kg, bdq, bdk = norm_args
        in_specs += [
            pl.BlockSpec((1, qw), lambda i: (0, 0)),
            pl.BlockSpec((1, kw), lambda i: (0, 0)),
            pl.BlockSpec((qw, qw), lambda i: (0, 0)),
            pl.BlockSpec((kw, kw), lambda i: (0, 0)),
        ]
        args += [qg, kg, bdq, bdk]
    out_specs = [
        pl.BlockSpec((qw, tm), lambda i: (0, i)),
        pl.BlockSpec((tm, kw), lambda i: (i, 0)),
        pl.BlockSpec((kw, tm), lambda i: (0, i)),
    ]
    out_shape = [
        jax.ShapeDtypeStruct((qw, rows), BF16),
        jax.ShapeDtypeStruct((rows, kw), BF16),
        jax.ShapeDtypeStruct((kw, rows), BF16),
    ]
    if fw:
        out_specs.append(pl.BlockSpec((tm, fw), lambda i: (i, 0)))
        out_shape.append(jax.ShapeDtypeStruct((rows, fw), F32))
    return pl.pallas_call(
        functools.partial(_proj_kernel, qw=qw, kw=kw, fw=fw, qk_norm=qk_norm),
        grid=(rows // tm,),
        in_specs=in_specs,
        out_specs=out_specs,
        out_shape=out_shape,
        compiler_params=_cparams(("parallel",)),
        name="mixer_proj",
    )(*args)


def _attend(qt_ref, segs, sink_ref, n_pad, o_ref):
    tq = qt_ref.shape[1]
    cols = GQA_GROUP * tq
    heads = range(qt_ref.shape[0] // (GQA_GROUP * HEAD_DIM))
    pair = lambda h: slice((h // 2) * LANES, (h // 2 + 1) * LANES)
    qpad, m, l, acc = [], [], [], []
    for h in heads:
        q4 = jnp.concatenate([qt_ref[(h * GQA_GROUP + g) * HEAD_DIM:(h * GQA_GROUP + g + 1) * HEAD_DIM, :]
                              for g in range(GQA_GROUP)], axis=1)
        z = jnp.zeros_like(q4)
        qpad.append(jnp.concatenate([q4, z] if h % 2 == 0 else [z, q4], axis=0))
        if sink_ref is not None:
            mh, lh = sink_ref[h], jnp.ones((1, cols), F32)
        else:
            mh, lh = jnp.full((1, cols), NEG_BIG, F32), jnp.zeros((1, cols), F32)
        if n_pad is not None:
            m_new = jnp.maximum(mh, jnp.where(n_pad > 0.0, 0.0, NEG_BIG))
            lh = lh * jnp.exp2(mh - m_new) + jnp.where(n_pad > 0.0, n_pad * jnp.exp2(-m_new), 0.0)
            mh = m_new
        m.append(mh)
        l.append(lh)
        acc.append(jnp.zeros((LANES, cols), F32))
    for k, vt, valid in segs:
        if valid is not None:
            valid = jnp.concatenate([valid] * GQA_GROUP, axis=1)
        for h in heads:
            s = _dot(k[:, pair(h)], qpad[h])
            if valid is not None:
                s = jnp.where(valid, s, NEG_BIG)
            m_new = jnp.maximum(m[h], jnp.max(s, axis=0, keepdims=True))
            alpha = jnp.exp2(m[h] - m_new)
            p = jnp.exp2(s - m_new)
            l[h] = alpha * l[h] + jnp.sum(p, axis=0, keepdims=True)
            acc[h] = alpha * acc[h] + _dot(vt[pair(h), :], p.astype(BF16))
            m[h] = m_new
    blocks = []
    for h in heads:
        ot = acc[h][(h % 2) * HEAD_DIM:(h % 2 + 1) * HEAD_DIM, :] * (1.0 / l[h])
        for g in range(0, GQA_GROUP, 2):
            blocks.append(jnp.concatenate([ot[:, g * tq:(g + 1) * tq], ot[:, (g + 1) * tq:(g + 2) * tq]], axis=0).T)
    o_ref[...] = jnp.concatenate(blocks, axis=1).astype(o_ref.dtype)


def _attn_global_kernel(qt_ref, kl_ref, vl_ref, kc_ref, vc_ref, o_ref, *, n_q_lat, key_chunk):
    i = pl.program_id(1)
    ctx_seg = (kc_ref[...], vc_ref[...], None)

    @pl.when(i < n_q_lat)
    def _():
        segs = [(kl_ref[c * key_chunk:(c + 1) * key_chunk, :], vl_ref[:, c * key_chunk:(c + 1) * key_chunk], None)
                for c in range(kl_ref.shape[0] // key_chunk)]
        _attend(qt_ref, segs + [ctx_seg], None, None, o_ref)

    @pl.when(i >= n_q_lat)
    def _():
        _attend(qt_ref, [ctx_seg], None, None, o_ref)


def _attn_window_kernel(qt_ref, k0_ref, k1_ref, k2_ref, v0_ref, v1_ref, v2_ref, kc_ref, vc_ref, sink_ref, o_ref,
                        *, n, n_q_lat):
    i = pl.program_id(1)
    tq = qt_ref.shape[1]
    ctx_seg = (kc_ref[...], vc_ref[...], None)

    @pl.when(i < n_q_lat)
    def _():
        k = jnp.concatenate([k0_ref[...], k1_ref[...], k2_ref[...]], axis=0)
        vt = jnp.concatenate([v0_ref[...], v1_ref[...], v2_ref[...]], axis=1)
        kpos = (i - 1) * tq + lax.broadcasted_iota(jnp.int32, (3 * tq, tq), 0)
        qpos = i * tq + lax.broadcasted_iota(jnp.int32, (3 * tq, tq), 1)
        valid = (jnp.abs(kpos - qpos) <= WINDOW) & (kpos >= WINDOW) & (kpos < n)
        q_row = i * tq + lax.broadcasted_iota(jnp.int32, (1, tq), 1)
        n_pad = jnp.maximum(q_row + WINDOW - n + 1, 0).astype(F32)
        n_pad = jnp.concatenate([n_pad] * GQA_GROUP, axis=1)
        _attend(qt_ref, [(k, vt, valid), ctx_seg], sink_ref, n_pad, o_ref)

    @pl.when(i >= n_q_lat)
    def _():
        _attend(qt_ref, [ctx_seg], sink_ref, None, o_ref)


def _attention(qt, k, vt, dims, sink=None):
    qw, rows = qt.shape
    kw = k.shape[1]
    n, lc, nb = dims["n"], dims["l"], dims["b"]
    n_q_lat = n // TQ
    n_q_ctx = lc // TQ
    lat_blocks = nb * n_q_lat
    ctx_block = nb * n // lc

    def q_blk(b, i):
        return jnp.where(i < n_q_lat, b * n_q_lat + i, lat_blocks + b * n_q_ctx + (i - n_q_lat))

    ctx_specs = [pl.BlockSpec((lc, kw), lambda b, i: (ctx_block + b, 0)),
                 pl.BlockSpec((kw, lc), lambda b, i: (0, ctx_block + b))]
    qt_spec = pl.BlockSpec((qw, TQ), lambda b, i: (0, q_blk(b, i)))
    if sink is None:
        body = functools.partial(_attn_global_kernel, n_q_lat=n_q_lat, key_chunk=dims["key_chunk"])
        in_specs = [qt_spec,
                    pl.BlockSpec((n, kw), lambda b, i: (b, 0)),
                    pl.BlockSpec((kw, n), lambda b, i: (0, b))] + ctx_specs
        args = [qt, k, vt, k, vt]
    else:
        assert WINDOW == TQ
        body = functools.partial(_attn_window_kernel, n=n, n_q_lat=n_q_lat)
        nbr = lambda b, i, off: b * n_q_lat + jnp.clip(i + off, 0, n_q_lat - 1)
        n_kv = kw // HEAD_DIM
        in_specs = ([qt_spec]
                    + [pl.BlockSpec((TQ, kw), functools.partial(lambda b, i, off: (nbr(b, i, off), 0), off=off))
                       for off in (-1, 0, 1)]
                    + [pl.BlockSpec((kw, TQ), functools.partial(lambda b, i, off: (0, nbr(b, i, off)), off=off))
                       for off in (-1, 0, 1)]
                    + ctx_specs
                    + [pl.BlockSpec((n_kv, 1, GQA_GROUP * TQ), lambda b, i: (0, 0, 0))])
        sink_rows = jnp.repeat(sink.astype(F32).reshape(n_kv, 1, GQA_GROUP) * LOG2_E, TQ, axis=2)
        args = [qt, k, k, k, vt, vt, vt, k, vt, sink_rows]
    return pl.pallas_call(
        body,
        grid=(nb, n_q_lat + n_q_ctx),
        in_specs=in_specs,
        out_specs=pl.BlockSpec((TQ, qw), lambda b, i: (q_blk(b, i), 0)),
        out_shape=jax.ShapeDtypeStruct((rows, qw), BF16),
        compiler_params=_cparams(("parallel", "arbitrary")),
        name="gqa_global" if sink is None else "gqa_window",
    )(*args)


def _feat_kernel(f_ref, fp_ref, fn_ref, mu_ref, w0_ref, w2_ref, a0_ref, a2_ref, g2_ref, kk_ref, ka_ref, bd_ref,
                 r_o, k_o, v_o, kk_o, g_o, lw0_o, lw1_o, kd0_o, kd1_o, b0_o, b1_o,
                 *, n_lat_tiles, tiles_lat, tiles_ctx):
    i = pl.program_id(0)
    t = f_ref.shape[0]
    bw = B_WIDTH
    is_lat = i < n_lat_tiles
    pos = jnp.where(is_lat, i % tiles_lat, (i - n_lat_tiles) % tiles_ctx)
    last = jnp.where(is_lat, tiles_lat, tiles_ctx) - 1
    f = f_ref[...]
    prow = jnp.where(pos == 0, 0.0, fp_ref[SUBLANES - 1:SUBLANES, :])
    nrow = jnp.where(pos == last, 0.0, fn_ref[0:1, :])
    rid = lax.broadcasted_iota(jnp.int32, (t, 1), 0)
    prev = jnp.where(rid == 0, prow, pltpu.roll(f, 1, 0))
    nxt = jnp.where(rid == t - 1, nrow, pltpu.roll(f, t - 1, 0))
    fs = f + mu_ref[0:1, :] * (prev - f) + mu_ref[1:2, :] * (nxt - f)

    r = fs[:, :bw]
    k = fs[:, bw:2 * bw]
    v = fs[:, 2 * bw:3 * bw]
    wl = jnp.tanh(fs[:, 3 * bw:3 * bw + 2 * DECAY_LORA])
    al = fs[:, 3 * bw + 2 * DECAY_LORA:3 * bw + 2 * DECAY_LORA + 2 * ICLR_LORA]
    gl = fs[:, 3 * bw + 2 * DECAY_LORA + 2 * ICLR_LORA:]

    kk = k * kk_ref[...]
    kk = kk * lax.rsqrt(_headsum(kk * kk, bd_ref[...]) + EPS)
    r_o[...] = r
    k_o[...] = k
    v_o[...] = v
    kk_o[...] = kk
    g_o[...] = _dot(_sigmoid(gl).astype(BF16), g2_ref[...])
    wl = wl.astype(BF16)
    al = al.astype(BF16)
    for d, (lw_o, kd_o, b_o) in enumerate(((lw0_o, kd0_o, b0_o), (lw1_o, kd1_o, b1_o))):
        w_log = w0_ref[d:d + 1, :] + _dot(wl, w2_ref[d])
        z = -w_log
        softplus = jnp.maximum(z, 0.0) + jnp.log(1.0 + jnp.exp(-jnp.abs(z)))
        lw_o[...] = -jnp.exp(-softplus - 0.5)
        a = _sigmoid(a0_ref[d:d + 1, :] + _dot(al, a2_ref[d]))
        kd_o[...] = k * (1.0 + (a - 1.0) * ka_ref[...])
        b_o[...] = kk * a


def _rwkv_features(f, params, dims):
    rows, fw = f.shape
    t = dims["tm_feat"]
    n, lc, nb = dims["n"], dims["l"], dims["b"]
    tiles_lat, tiles_ctx = n // t, lc // t
    n_lat_tiles = nb * tiles_lat
    hb = t // SUBLANES
    n_halo = rows // SUBLANES
    mu, w0, w2p, a0, a2p, g2, k_k, k_a, bd = params
    bw = B_WIDTH
    const = lambda shape: pl.BlockSpec(shape, lambda i: (0,) * len(shape))
    in_specs = [
        pl.BlockSpec((t, fw), lambda i: (i, 0)),
        pl.BlockSpec((SUBLANES, fw), lambda i: (jnp.maximum(i * hb - 1, 0), 0)),
        pl.BlockSpec((SUBLANES, fw), lambda i: (jnp.minimum((i + 1) * hb, n_halo - 1), 0)),
        const((2, fw)), const((2, bw)), const((2, 2 * DECAY_LORA, bw)), const((2, bw)),
        const((2, 2 * ICLR_LORA, bw)), const((GATE_LORA, bw)), const((1, bw)), const((1, bw)), const((bw, bw)),
    ]
    out = jax.ShapeDtypeStruct((rows, bw), F32)
    return pl.pallas_call(
        functools.partial(_feat_kernel, n_lat_tiles=n_lat_tiles, tiles_lat=tiles_lat, tiles_ctx=tiles_ctx),
        grid=(rows // t,),
        in_specs=in_specs,
        out_specs=[pl.BlockSpec((t, bw), lambda i: (i, 0))] * 11,
        out_shape=[out] * 11,
        compiler_params=_cparams(("parallel",)),
        name="rwkv_features",
    )(f, f, f, mu, w0, w2p, a0, a2p, g2, k_k, k_a, bd)


def _stack_heads(x, head0):
    return jnp.concatenate([jnp.where(head0, x, 0.0), jnp.where(head0, 0.0, x)], axis=0).astype(BF16)


def _cumsum_rows(tri, x):
    hi = x.astype(BF16)
    r1 = x - hi.astype(F32)
    mid = r1.astype(BF16)
    lo = (r1 - mid.astype(F32)).astype(BF16)
    return _dot(tri, hi) + _dot(tri, mid) + _dot(tri, lo)


def _wkv_units(units, eye, lvl_ref):
    c = units[0]["v"].shape[0]
    p2 = 2 * c
    head0 = lax.broadcasted_iota(jnp.int32, (1, LANES), 1) < HEAD_DIM
    st = lambda x: _stack_heads(x, head0)
    fold = lambda x: x[:c] + x[c:]
    each = lambda fn: [fn(u) for u in units]

    for u in units:
        u["kq_s"], u["rq_s"], u["v_s"] = st(u["kq"]), st(u["rq"]), st(u["v"])
    a = each(lambda u: _dot_nt(jnp.concatenate([u["kq_s"], u["rq_s"]], axis=0),
                               jnp.concatenate([st(u["bi"]), st(u["ki"])], axis=0)))
    for u, au in zip(units, a):
        u["nkb"] = jnp.where(u["m_strict"], au[:p2, :p2], 0.0)
        u["akk"] = jnp.where(u["m_strict"], au[:p2, p2:], 0.0).astype(BF16)
        u["arb"] = jnp.where(u["m_incl"], au[p2:, :p2], 0.0).astype(BF16)
        u["ark"] = jnp.where(u["m_incl"], au[p2:, p2:], 0.0).astype(BF16)

    nkb_h = each(lambda u: u["nkb"].astype(BF16))
    tinv = each(lambda u: eye - u["nkb"] * lvl_ref[0].astype(F32))
    for lev in range(1, lvl_ref.shape[0]):
        t_h = [t.astype(BF16) for t in tinv]
        lt = [_dot(nh * lvl_ref[lev], th).astype(BF16) for nh, th in zip(nkb_h, t_h)]
        tinv = [t - _dot(th, x) for t, th, x in zip(tinv, t_h, lt)]
    t_h = [t.astype(BF16) for t in tinv]

    akkv = each(lambda u: _dot(u["akk"], u["v_s"]).astype(BF16))
    gp_h = [_dot(th, jnp.concatenate([u["kq_s"], x], axis=1)).astype(BF16)
            for u, th, x in zip(units, t_h, akkv)]
    corr = [_dot(u["arb"], g) for u, g in zip(units, gp_h)]
    arkv = each(lambda u: _dot(u["ark"], u["v_s"]))
    btg = [_dot_tn(st(u["bt"]), g) for u, g in zip(units, gp_h)]
    ktv = each(lambda u: _dot_tn(st(u["kt"]), u["v_s"]))

    for u, cr, av, bg, kv in zip(units, corr, arkv, btg, ktv):
        rq = u["rq"]
        rq2 = fold(jnp.concatenate([jnp.where(head0, rq, 0.0), jnp.where(head0, 0.0, rq)], axis=0) - cr[:, :LANES])
        u["rq2"] = rq2.astype(BF16)
        u["yl"] = fold(av - cr[:, LANES:])
        u["m_state"] = (jnp.where(eye > 0.0, u["decay"], 0.0) - bg[:, :LANES]).astype(BF16)
        u["n_state"] = kv - bg[:, LANES:]


def _scan_kernel(rf, vf, kkf, lwf, kdf, bf, rb, vb, kkb, lwb, kdb, bb, tri_ref, msk_ref, lvl_ref,
                 yf_o, yb_o, hf_s, hb_s):
    @pl.when(pl.program_id(1) == 0)
    def _():
        hf_s[...] = jnp.zeros_like(hf_s)
        hb_s[...] = jnp.zeros_like(hb_s)

    c = SCAN_CHUNK
    eye = msk_ref[0]
    units, chains = [], []
    for d, (refs, y_o, h_s) in enumerate((((rf, vf, kkf, lwf, kdf, bf), yf_o, hf_s),
                                          ((rb, vb, kkb, lwb, kdb, bb), yb_o, hb_s))):
        m_strict = msk_ref[1 + 2 * d] > 0.0
        m_incl = msk_ref[2 + 2 * d] > 0.0
        r, v, kk, lw, kd, b = (ref[...] for ref in refs)
        cw = _cumsum_rows(tri_ref[d], lw)
        tot = [jnp.sum(lw[j * c:(j + 1) * c], axis=0, keepdims=True) for j in range(SCAN_STEP_CHUNKS)]
        tot_rows = jnp.concatenate([jnp.broadcast_to(t, (c, t.shape[1])) for t in tot], axis=0)
        w_inv = jnp.exp(-cw)
        w_rest = jnp.exp(tot_rows - cw)
        facs = {"kq": kk * jnp.exp(cw - lw), "rq": r * jnp.exp(cw), "bi": b * w_inv, "ki": kd * w_inv,
                "bt": b * w_rest, "kt": kd * w_rest, "v": v}
        order = range(SCAN_STEP_CHUNKS) if d == 0 else range(SCAN_STEP_CHUNKS - 1, -1, -1)
        for p in range(B_WIDTH // LANES):
            ls = slice(p * LANES, (p + 1) * LANES)
            chain = []
            for j in order:
                unit = {name: x[j * c:(j + 1) * c, ls] for name, x in facs.items()}
                unit.update(decay=jnp.exp(tot[j][:, ls]), m_strict=m_strict, m_incl=m_incl, rows=slice(j * c, (j + 1) * c))
                units.append(unit)
                chain.append(unit)
            chains.append((chain, y_o, h_s, p, ls))
    _wkv_units(units, eye, lvl_ref)
    states = [h_s[p] for _, _, h_s, p, _ in chains]
    for pos in range(SCAN_STEP_CHUNKS):
        for ci, (chain, y_o, _, _, ls) in enumerate(chains):
            u = chain[pos]
            h_h = states[ci].astype(BF16)
            y_o[u["rows"], ls] = _dot(u["rq2"], h_h) + u["yl"]
            states[ci] = _dot(u["m_state"], h_h) + u["n_state"]
    for h, (_, _, h_s, p, _) in zip(states, chains):
        h_s[p] = h


def _scan_constants():
    c, p2 = SCAN_CHUNK, PAIR_ROWS
    t = np.arange(SCAN_STEP_CHUNKS * c)
    same_chunk = (t[None, :] // c) == (t[:, None] // c)
    tri = np.stack([same_chunk & (t[None, :] <= t[:, None]), same_chunk & (t[None, :] >= t[:, None])]).astype(np.float32)
    i = np.arange(p2)
    same_head = (i[:, None] // c) == (i[None, :] // c)
    ti, si = i[:, None] % c, i[None, :] % c
    msk = np.stack([
        np.eye(p2, dtype=bool),
        same_head & (si < ti), same_head & (si <= ti),
        same_head & (si > ti), same_head & (si >= ti),
    ]).astype(np.float32)
    n_lev = int(np.log2(c))
    lvl = np.stack([
        ((i[:, None] >> (k + 1)) == (i[None, :] >> (k + 1))) & ((i[:, None] >> k) != (i[None, :] >> k))
        for k in range(n_lev)
    ]).astype(np.float32)
    return jnp.asarray(tri, BF16), jnp.asarray(msk), jnp.asarray(lvl, BF16)


def _wkv_scan(feats, dims):
    r, v, kk, lw0, lw1, kd0, kd1, b0, b1 = feats
    rows, bw = r.shape
    c = SCAN_STEP_CHUNKS * SCAN_CHUNK
    n, lc, nb = dims["n"], dims["l"], dims["b"]
    n_c, l_c = n // c, lc // c
    ctx_base = nb * n_c
    tri, msk, lvl = _scan_constants()

    def fwd(b, s):
        return (jnp.where(s < l_c, ctx_base + b * l_c + s, b * n_c + (s - l_c)), 0)

    def bwd(b, s):
        return (jnp.where(s < l_c, ctx_base + b * l_c + (l_c - 1 - s), b * n_c + (n_c - 1 - (s - l_c))), 0)

    const = lambda a: pl.BlockSpec(a.shape, lambda b, s: (0,) * a.ndim)
    out = jax.ShapeDtypeStruct((rows, bw), F32)
    return pl.pallas_call(
        _scan_kernel,
        grid=(nb, l_c + n_c),
        in_specs=[pl.BlockSpec((c, bw), fwd)] * 6 + [pl.BlockSpec((c, bw), bwd)] * 6
                 + [const(tri), const(msk), const(lvl)],
        out_specs=[pl.BlockSpec((c, bw), fwd), pl.BlockSpec((c, bw), bwd)],
        out_shape=[out, out],
        scratch_shapes=[pltpu.VMEM((bw // LANES, LANES, LANES), F32)] * 2,
        compiler_params=_cparams(("parallel", "arbitrary")),
        name="wkv_scan",
    )(r, v, kk, lw0, kd0, b0, r, v, kk, lw1, kd1, b1, tri, msk, lvl)


def _outproj_kernel(x_ref, mod_ref, wo_ref, oa_ref, *rest, even):
    o_ref = rest[-1]
    if even:
        yf, yb, r, k, v, g, rk, gnw, gnb, bd = rest[:-1]
        y = yf[...] + yb[...]
        mean = _headsum(y, bd[...]) * (1.0 / HEAD_DIM)
        yc = y - mean
        var = _headsum(yc * yc, bd[...]) * (1.0 / HEAD_DIM)
        yn = yc * lax.rsqrt(var + GN_EPS) * gnw[...] + gnb[...]
        bonus = _headsum(r[...] * k[...] * rk[...], bd[...]) * v[...]
        ob = ((yn + bonus) * g[...]).astype(BF16)
        half = oa_ref.shape[1]
        o = _dot(oa_ref[...], wo_ref[:half, :]) + _dot(ob, wo_ref[half:, :])
    else:
        o = _dot(oa_ref[...], wo_ref[...])
    o_ref[...] = x_ref[...] + mod_ref[5:6, :] * o


def _outproj(xs, mods, w_out, l, e, o_att, dims, readout=None):
    rows, d = xs.shape
    tm = dims["tm_proj"]
    tiles_per_batch = dims["n"] // tm
    nb = dims["b"]
    even = readout is not None
    in_specs = [
        pl.BlockSpec((tm, d), lambda i: (i, 0)),
        pl.BlockSpec((None, None, N_MOD, d), lambda i: (l, jnp.minimum(i // tiles_per_batch, nb), 0, 0)),
        pl.BlockSpec((None, d, d), lambda i: (e, 0, 0)),
        pl.BlockSpec((tm, o_att.shape[1]), lambda i: (i, 0)),
    ]
    args = [xs, mods, w_out, o_att]
    if even:
        yf, yb, r, k, v, g, rk, gnw, gnb, bd = readout
        bw = B_WIDTH
        in_specs += [pl.BlockSpec((tm, bw), lambda i: (i, 0))] * 6
        in_specs += [pl.BlockSpec((1, bw), lambda i: (0, 0))] * 3 + [pl.BlockSpec((bw, bw), lambda i: (0, 0))]
        args += [yf, yb, r, k, v, g, rk, gnw, gnb, bd]
    return pl.pallas_call(
        functools.partial(_outproj_kernel, even=even),
        grid=(rows // tm,),
        in_specs=in_specs,
        out_specs=pl.BlockSpec((tm, d), lambda i: (i, 0)),
        out_shape=jax.ShapeDtypeStruct((rows, d), F32),
        compiler_params=_cparams(("parallel",)),
        name="mixer_out",
    )(*args)


def _rope_tables(n, tm):
    rows = n // GRID_W
    row = jnp.repeat(jnp.arange(rows, dtype=F32), GRID_W)
    col = jnp.tile(jnp.arange(GRID_W, dtype=F32), rows)
    n_freq = HEAD_DIM // 4
    inv_freq = ROPE_THETA ** (-jnp.arange(n_freq, dtype=F32) / n_freq)
    ang = jnp.concatenate([row[:, None] * inv_freq, col[:, None] * inv_freq], axis=-1)
    cos = jnp.repeat(jnp.cos(ang), 2, axis=-1)
    sin = jnp.repeat(jnp.sin(ang), 2, axis=-1)
    even_lane = (jnp.arange(HEAD_DIM) % 2 == 0)[None, :]
    sin_a = jnp.where(even_lane, -sin, 0.0)
    sin_b = jnp.where(even_lane, 0.0, sin)
    pad = lambda tab, fill: jnp.concatenate(
        [jnp.tile(tab, (1, LANES // HEAD_DIM)), jnp.full((tm, LANES), fill, F32)], axis=0)
    return pad(cos, 1.0), pad(sin_a, 0.0), pad(sin_b, 0.0)


def _block_diag(width, value, dtype):
    h = np.arange(width) // HEAD_DIM
    return jnp.asarray((h[:, None] == h[None, :]) * value, dtype)


def _pad_lora(w):
    r = w.shape[1]
    z = jnp.zeros_like(w[0])
    return jnp.stack([jnp.concatenate([w[0], z], axis=0), jnp.concatenate([z, w[1]], axis=0)])


def _largest_tile(limit, *sizes):
    t = limit
    while any(s % t for s in sizes):
        t //= 2
    return t


def kernel(x, c, ctx, c_ctx, w_mod, b_mod, ffn_in, ffn_out, even_w_in, even_w_out, q_gain, k_gain, rwkv_mu, rwkv_w0,
           rwkv_w2, rwkv_a0, rwkv_a2, rwkv_g2, rwkv_k_k, rwkv_k_a, rwkv_r_k, rwkv_gn_w, rwkv_gn_b, odd_w_in, odd_w_out,
           sink, final_gain):
    nb, n, d = x.shape
    lc = ctx.shape[1]
    depth = w_mod.shape[0]
    dff = ffn_out.shape[2]
    assert n % TQ == 0 and lc % TQ == 0 and n >= TQ + 2 * WINDOW and n % GRID_W == 0
    scan_rows = SCAN_STEP_CHUNKS * SCAN_CHUNK
    assert n % scan_rows == 0 and lc % scan_rows == 0 and nb + 1 <= MOD_ROWS
    dims = {
        "b": nb, "n": n, "l": lc,
        "tm_ffn": _largest_tile(512, n, nb * lc),
        "tm_proj": _largest_tile(512, n, nb * lc),
        "tm_feat": _largest_tile(256, n, lc),
        "key_chunk": _largest_tile(2048, n),
    }

    cvec = jnp.zeros((MOD_ROWS, d), F32).at[:nb].set(c).at[nb].set(c_ctx)
    mods = _mod_table(cvec, w_mod, b_mod).reshape(depth, MOD_ROWS, N_MOD, d)

    ffn_in_h = ffn_in.astype(BF16)
    ffn_out_h = ffn_out.astype(BF16)
    even_in_h = even_w_in.astype(BF16)
    even_out_h = even_w_out.astype(BF16)
    odd_in_h = odd_w_in.astype(BF16)
    odd_out_h = odd_w_out.astype(BF16)

    rope_tabs = _rope_tables(n, dims["tm_proj"])
    a_kw = A_KV_HEADS * HEAD_DIM
    a_qw = A_HEADS * HEAD_DIM
    bd_q = _block_diag(a_qw, 1.0 / HEAD_DIM, BF16)
    bd_k = _block_diag(a_kw, 1.0 / HEAD_DIM, BF16)
    bd_ones = _block_diag(B_WIDTH, 1.0, BF16)
    fw = even_w_in.shape[2] - a_qw - 2 * a_kw

    xs = jnp.concatenate([x.reshape(nb * n, d), ctx.reshape(nb * lc, d)], axis=0)
    for l in range(depth):
        last = l == depth - 1
        xs = _ffn(xs, mods, ffn_in_h, ffn_out_h, l, 0, 0, dims)
        if l % 2 == 0:
            e = l // 2
            tile_h = lambda g, reps: jnp.tile(g, reps).reshape(1, -1)
            norm_args = (tile_h(q_gain[e], A_HEADS), tile_h(k_gain[e], A_KV_HEADS), bd_q, bd_k)
            q, k, v, f = _proj(xs, mods, even_in_h, l, e, rope_tabs, dims, a_qw, a_kw, fw, norm_args)
            oa = _attention(q, k, v, dims)
            feat_params = (rwkv_mu[e], rwkv_w0[e], _pad_lora(rwkv_w2[e]).astype(BF16), rwkv_a0[e],
                           _pad_lora(rwkv_a2[e]).astype(BF16), rwkv_g2[e].astype(BF16),
                           rwkv_k_k[e].reshape(1, -1), rwkv_k_a[e].reshape(1, -1), bd_ones)
            r, kr, vr, kk, g, lw0, lw1, kd0, kd1, b0, b1 = _rwkv_features(f, feat_params, dims)
            yf, yb = _wkv_scan((r, vr, kk, lw0, lw1, kd0, kd1, b0, b1), dims)
            readout = (yf, yb, r, kr, vr, g, rwkv_r_k[e].reshape(1, -1), rwkv_gn_w[e].reshape(1, -1),
                       rwkv_gn_b[e].reshape(1, -1), bd_ones)
            xs = _outproj(xs, mods, even_out_h, l, e, oa, dims, readout)
        else:
            o = l // 2
            c_kw = C_KV_HEADS * HEAD_DIM
            c_qw = C_HEADS * HEAD_DIM
            q, k, v = _proj(xs, mods, odd_in_h, l, o, rope_tabs, dims, c_qw, c_kw, 0)
            oc = _attention(q, k, v, dims, sink=sink[o])
            xs = _outproj(xs, mods, odd_out_h, l, o, oc, dims)
        xs = _ffn(xs, mods, ffn_in_h, ffn_out_h, l, 1, 6, dims, final_gain=final_gain if last else None)
    return xs.reshape(nb, n, d)
```

```python
import functools

import jax
import jax.numpy as jnp
import numpy as np
from jax import lax
from jax.experimental import pallas as pl
from jax.experimental.pallas import tpu as pltpu

F32 = jnp.float32
BF16 = jnp.bfloat16
HIGHEST = lax.Precision.HIGHEST

HEAD_DIM = 64
GRID_W = 64
A_HEADS, A_KV_HEADS = 8, 2
B_HEADS = 8
B_WIDTH = B_HEADS * HEAD_DIM
DECAY_LORA, ICLR_LORA, GATE_LORA = 64, 64, 128
C_HEADS, C_KV_HEADS = 16, 4
GQA_GROUP = 4
WINDOW = 128
N_MOD = 9
ROPE_THETA = 10000.0
EPS = 1e-6
GN_EPS = 64e-5
NEG_BIG = -1e30
LOG2_E = 1.4426950408889634

LANES = 128
SUBLANES = 8
VMEM_LIMIT_BYTES = 56 * 1024 * 1024

SCAN_CHUNK = 64
SCAN_STEP_CHUNKS = 2
PAIR_ROWS = 2 * SCAN_CHUNK
TQ = 128
AHEAD = 2
ONES_ROWS = 16
MOD_ROWS = 16


def _cparams(sem):
    return pltpu.CompilerParams(dimension_semantics=sem, vmem_limit_bytes=VMEM_LIMIT_BYTES)


def _dot(a, b, precision=None):
    return jnp.dot(a, b, preferred_element_type=F32, precision=precision)


def _dot_nt(a, b, precision=None):
    return lax.dot_general(a, b, (((1,), (1,)), ((), ())), preferred_element_type=F32, precision=precision)


def _dot_tn(a, b, precision=None):
    return lax.dot_general(a, b, (((0,), (0,)), ((), ())), preferred_element_type=F32, precision=precision)


def _rms(x):
    return x * lax.rsqrt(jnp.mean(x * x, axis=-1, keepdims=True) + EPS)


def _sigmoid(x):
    return 1.0 / (1.0 + jnp.exp(-x))


def _headsum(x, bd):
    hi = x.astype(BF16)
    lo = (x - hi.astype(F32)).astype(BF16)
    return _dot(hi, bd) + _dot(lo, bd)


def _mod_kernel(c_ref, w_ref, b_ref, o_ref):
    c = c_ref[...]
    o_ref[...] = _dot(c * _sigmoid(c), w_ref[...], HIGHEST) + b_ref[...]


def _mod_table(cvec, w_mod, b_mod):
    depth, d, nd = w_mod.shape
    tn = 1536 if nd % 1536 == 0 else nd
    return pl.pallas_call(
        _mod_kernel,
        grid=(depth, nd // tn),
        in_specs=[
            pl.BlockSpec((MOD_ROWS, d), lambda l, n: (0, 0)),
            pl.BlockSpec((None, d, tn), lambda l, n: (l, 0, n)),
            pl.BlockSpec((None, 1, tn), lambda l, n: (l, 0, n)),
        ],
        out_specs=pl.BlockSpec((None, MOD_ROWS, tn), lambda l, n: (l, 0, n)),
        out_shape=jax.ShapeDtypeStruct((depth, MOD_ROWS, nd), F32),
        compiler_params=_cparams(("parallel", "parallel")),
        name="mod_table",
    )(cvec, w_mod, b_mod.reshape(depth, 1, nd))


def _ffn_kernel(x_ref, mod_ref, wg_ref, wu_ref, wo_ref, *rest, i_shift, final):
    o_ref = rest[-1]
    x = x_ref[...]
    hn = (_rms(x) * (1.0 + mod_ref[i_shift + 1:i_shift + 2, :]) + mod_ref[i_shift:i_shift + 1, :]).astype(BF16)
    g = _dot(hn, wg_ref[...])
    u = _dot(hn, wu_ref[...])
    a = (g * _sigmoid(g) * u).astype(BF16)
    y = x + (0.5 * mod_ref[i_shift + 2:i_shift + 3, :]) * _dot(a, wo_ref[...])
    if final:
        y = _rms(y) * rest[0][...]
    o_ref[...] = y


def _ffn(xs, mods, w_in, w_out, l, j, i_shift, dims, final_gain=None):
    rows, d = xs.shape
    dff = w_out.shape[2]
    tm = dims["tm_ffn"]
    tiles_per_batch = dims["n"] // tm
    nb = dims["b"]
    final = final_gain is not None
    if final:
        rows = nb * dims["n"]
    in_specs = [
        pl.BlockSpec((tm, d), lambda i: (i, 0)),
        pl.BlockSpec((None, None, N_MOD, d), lambda i: (l, jnp.minimum(i // tiles_per_batch, nb), 0, 0)),
        pl.BlockSpec((None, None, d, dff), lambda i: (l, j, 0, 0)),
        pl.BlockSpec((None, None, d, dff), lambda i: (l, j, 0, 1)),
        pl.BlockSpec((None, None, dff, d), lambda i: (l, j, 0, 0)),
    ]
    args = [xs, mods, w_in, w_in, w_out]
    if final:
        in_specs.append(pl.BlockSpec((1, d), lambda i: (0, 0)))
        args.append(final_gain.reshape(1, d))
    return pl.pallas_call(
        functools.partial(_ffn_kernel, i_shift=i_shift, final=final),
        grid=(rows // tm,),
        in_specs=in_specs,
        out_specs=pl.BlockSpec((tm, d), lambda i: (i, 0)),
        out_shape=jax.ShapeDtypeStruct((rows, d), F32),
        compiler_params=_cparams(("parallel",)),
        name="ffn",
    )(*args)


def _rope(x, cos, sin_a, sin_b):
    outs = []
    for g in range(x.shape[1] // LANES):
        xg = x[:, g * LANES:(g + 1) * LANES]
        nxt = pltpu.roll(xg, LANES - 1, 1)
        prv = pltpu.roll(xg, 1, 1)
        outs.append(xg * cos + nxt * sin_a + prv * sin_b)
    return outs[0] if len(outs) == 1 else jnp.concatenate(outs, axis=1)


def _proj_kernel(x_ref, mod_ref, w_ref, cos_ref, sa_ref, sb_ref, *rest, qw, kw, fw, qk_norm):
    if qk_norm:
        qg_ref, kg_ref, bdq_ref, bdk_ref = rest[:4]
        rest = rest[4:]
    q_ref, k_ref, v_ref = rest[:3]
    hn = (_rms(x_ref[...]) * (1.0 + mod_ref[4:5, :]) + mod_ref[3:4, :]).astype(BF16)
    p = _dot(hn, w_ref[...])
    q = p[:, :qw]
    k = p[:, qw:qw + kw]
    if qk_norm:
        q = q * lax.rsqrt(_headsum(q * q, bdq_ref[...]) + EPS) * qg_ref[...]
        k = k * lax.rsqrt(_headsum(k * k, bdk_ref[...]) + EPS) * kg_ref[...]
    cos, sa, sb = cos_ref[...], sa_ref[...], sb_ref[...]
    q_ref[...] = (_rope(q, cos, sa, sb) * (HEAD_DIM ** -0.5 * LOG2_E)).T.astype(BF16)
    k_ref[...] = _rope(k, cos, sa, sb).astype(BF16)
    v_ref[...] = p[:, qw + kw:qw + 2 * kw].T.astype(BF16)
    if fw:
        rest[3][...] = p[:, qw + 2 * kw:]


def _proj(xs, mods, w_in, l, e, rope_tabs, dims, qw, kw, fw, norm_args=None):
    rows, d = xs.shape
    tm = dims["tm_proj"]
    cols = qw + 2 * kw + fw
    tiles_per_batch = dims["n"] // tm
    n_lat_tiles = dims["b"] * tiles_per_batch
    nb = dims["b"]

    def tab_idx(i):
        return (jnp.where(i < n_lat_tiles, i % tiles_per_batch, tiles_per_batch), 0)

    in_specs = [
        pl.BlockSpec((tm, d), lambda i: (i, 0)),
        pl.BlockSpec((None, None, N_MOD, d), lambda i: (l, jnp.minimum(i // tiles_per_batch, nb), 0, 0)),
        pl.BlockSpec((None, d, cols), lambda i: (e, 0, 0)),
        pl.BlockSpec((tm, LANES), tab_idx),
        pl.BlockSpec((tm, LANES), tab_idx),
        pl.BlockSpec((tm, LANES), tab_idx),
    ]
    args = [xs, mods, w_in, *rope_tabs]
    qk_norm = norm_args is not None
    if qk_norm:
        qg, kg, bdq, bdk = norm_args
        in_specs += [
            pl.BlockSpec((1, qw), lambda i: (0, 0)),
            pl.BlockSpec((1, kw), lambda i: (0, 0)),
            pl.BlockSpec((qw, qw), lambda i: (0, 0)),
            pl.BlockSpec((kw, kw), lambda i: (0, 0)),
        ]
        args += [qg, kg, bdq, bdk]
    out_specs = [
        pl.BlockSpec((qw, tm), lambda i: (0, i)),
        pl.BlockSpec((tm, kw), lambda i: (i, 0)),
        pl.BlockSpec((kw, tm), lambda i: (0, i)),
    ]
    out_shape = [
        jax.ShapeDtypeStruct((qw, rows), BF16),
        jax.ShapeDtypeStruct((rows, kw), BF16),
        jax.ShapeDtypeStruct((kw, rows), BF16),
    ]
    if fw:
        out_specs.append(pl.BlockSpec((tm, fw), lambda i: (i, 0)))
        out_shape.append(jax.ShapeDtypeStruct((rows, fw), F32))
    return pl.pallas_call(
        functools.partial(_proj_kernel, qw=qw, kw=kw, fw=fw, qk_norm=qk_norm),
        grid=(rows // tm,),
        in_specs=in_specs,
        out_specs=out_specs,
        out_shape=out_shape,
        compiler_params=_cparams(("parallel",)),
        name="mixer_proj",
    )(*args)


def _attend(qt_ref, segs, sink_ref, n_pad, o_ref):
    tq = qt_ref.shape[1]
    cols = GQA_GROUP * tq
    heads = range(qt_ref.shape[0] // (GQA_GROUP * HEAD_DIM))
    pair = lambda h: slice((h // 2) * LANES, (h // 2 + 1) * LANES)
    qpad, m, acc = [], [], []
    for h in heads:
        q4 = jnp.concatenate([qt_ref[(h * GQA_GROUP + g) * HEAD_DIM:(h * GQA_GROUP + g + 1) * HEAD_DIM, :]
                              for g in range(GQA_GROUP)], axis=1)
        z = jnp.zeros_like(q4)
        qpad.append(jnp.concatenate([q4, z] if h % 2 == 0 else [z, q4], axis=0))
        if sink_ref is not None:
            mh, lh = sink_ref[h], jnp.ones((1, cols), F32)
        else:
            mh, lh = jnp.full((1, cols), NEG_BIG, F32), jnp.zeros((1, cols), F32)
        if n_pad is not None:
            m_new = jnp.maximum(mh, jnp.where(n_pad > 0.0, 0.0, NEG_BIG))
            lh = lh * jnp.exp2(mh - m_new) + jnp.where(n_pad > 0.0, n_pad * jnp.exp2(-m_new), 0.0)
            mh = m_new
        m.append(mh)
        acc.append(jnp.concatenate([jnp.zeros((HEAD_DIM, cols), F32), jnp.broadcast_to(lh, (ONES_ROWS, cols))], axis=0))
    def scores(item):
        (k, _, valid), h = item
        s = _dot(k[:, pair(h)], qpad[h])
        if valid is not None:
            s = jnp.where(jnp.concatenate([valid] * GQA_GROUP, axis=1), s, NEG_BIG)
        return s

    items = [(seg, h) for seg in segs for h in heads]
    pending = [scores(it) for it in items[:AHEAD]]
    for idx, ((_, vt, _), h) in enumerate(items):
        s = pending.pop(0)
        if idx + AHEAD < len(items):
            pending.append(scores(items[idx + AHEAD]))
        m_new = jnp.maximum(m[h], jnp.max(s, axis=0, keepdims=True))
        alpha = jnp.exp2(m[h] - m_new)
        p = jnp.exp2(s - m_new).astype(BF16)
        v1 =jnp.concatenate([vt[h * HEAD_DIM:(h + 1) * HEAD_DIM, :], jnp.ones((ONES_ROWS, vt.shape[1]), BF16)], axis=0)
        acc[h] = alpha * acc[h] + _dot(v1, p)
        m[h] = m_new
    blocks = []
    for h in heads:
        ot = acc[h][:HEAD_DIM, :] * (1.0 / acc[h][HEAD_DIM:HEAD_DIM + 1, :])
        for g in range(0, GQA_GROUP, 2):
            blocks.append(jnp.concatenate([ot[:, g * tq:(g + 1) * tq], ot[:, (g + 1) * tq:(g + 2) * tq]], axis=0).T)
    o_ref[...] = jnp.concatenate(blocks, axis=1).astype(o_ref.dtype)


def _attn_global_kernel(qt_ref, kl_ref, vl_ref, kc_ref, vc_ref, o_ref, *, n_q_lat, key_chunk):
    i = pl.program_id(1)
    ctx_seg = (kc_ref[...], vc_ref[...], None)

    @pl.when(i < n_q_lat)
    def _():
        segs = [(kl_ref[c * key_chunk:(c + 1) * key_chunk, :], vl_ref[:, c * key_chunk:(c + 1) * key_chunk], None)
                for c in range(kl_ref.shape[0] // key_chunk)]
        _attend(qt_ref, segs + [ctx_seg], None, None, o_ref)

    @pl.when(i >= n_q_lat)
    def _():
        _attend(qt_ref, [ctx_seg], None, None, o_ref)


def _attn_window_kernel(qt_ref, k0_ref, k1_ref, k2_ref, v0_ref, v1_ref, v2_ref, kc_ref, vc_ref, sink_ref, o_ref,
                        *, n, n_q_lat):
    i = pl.program_id(1)
    tq = qt_ref.shape[1]
    ctx_seg = (kc_ref[...], vc_ref[...], None)

    @pl.when(i < n_q_lat)
    def _():
        k = jnp.concatenate([k0_ref[...], k1_ref[...], k2_ref[...]], axis=0)
        vt = jnp.concatenate([v0_ref[...], v1_ref[...], v2_ref[...]], axis=1)
        kpos = (i - 1) * tq + lax.broadcasted_iota(jnp.int32, (3 * tq, tq), 0)
        qpos = i * tq + lax.broadcasted_iota(jnp.int32, (3 * tq, tq), 1)
        valid = (jnp.abs(kpos - qpos) <= WINDOW) & (kpos >= WINDOW) & (kpos < n)
        q_row = i * tq + lax.broadcasted_iota(jnp.int32, (1, tq), 1)
        n_pad = jnp.maximum(q_row + WINDOW - n + 1, 0).astype(F32)
        n_pad = jnp.concatenate([n_pad] * GQA_GROUP, axis=1)
        _attend(qt_ref, [(k, vt, valid), ctx_seg], sink_ref, n_pad, o_ref)

    @pl.when(i >= n_q_lat)
    def _():
        _attend(qt_ref, [ctx_seg], sink_ref, None, o_ref)


def _attention(qt, k, vt, dims, sink=None):
    qw, rows = qt.shape
    kw = k.shape[1]
    n, lc, nb = dims["n"], dims["l"], dims["b"]
    n_q_lat = n // TQ
    n_q_ctx = lc // TQ
    lat_blocks = nb * n_q_lat
    ctx_block = nb * n // lc

    def q_blk(b, i):
        return jnp.where(i < n_q_lat, b * n_q_lat + i, lat_blocks + b * n_q_ctx + (i - n_q_lat))

    ctx_specs = [pl.BlockSpec((lc, kw), lambda b, i: (ctx_block + b, 0)),
                 pl.BlockSpec((kw, lc), lambda b, i: (0, ctx_block + b))]
    qt_spec = pl.BlockSpec((qw, TQ), lambda b, i: (0, q_blk(b, i)))
    if sink is None:
        body = functools.partial(_attn_global_kernel, n_q_lat=n_q_lat, key_chunk=dims["key_chunk"])
        in_specs = [qt_spec,
                    pl.BlockSpec((n, kw), lambda b, i: (b, 0)),
                    pl.BlockSpec((kw, n), lambda b, i: (0, b))] + ctx_specs
        args = [qt, k, vt, k, vt]
    else:
        assert WINDOW == TQ
        body = functools.partial(_attn_window_kernel, n=n, n_q_lat=n_q_lat)
        nbr = lambda b, i, off: b * n_q_lat + jnp.clip(i + off, 0, n_q_lat - 1)
        n_kv = kw // HEAD_DIM
        in_specs = ([qt_spec]
                    + [pl.BlockSpec((TQ, kw), functools.partial(lambda b, i, off: (nbr(b, i, off), 0), off=off))
                       for off in (-1, 0, 1)]
                    + [pl.BlockSpec((kw, TQ), functools.partial(lambda b, i, off: (0, nbr(b, i, off)), off=off))
                       for off in (-1, 0, 1)]
                    + ctx_specs
                    + [pl.BlockSpec((n_kv, 1, GQA_GROUP * TQ), lambda b, i: (0, 0, 0))])
        sink_rows = jnp.repeat(sink.astype(F32).reshape(n_kv, 1, GQA_GROUP) * LOG2_E, TQ, axis=2)
        args = [qt, k, k, k, vt, vt, vt, k, vt, sink_rows]
    return pl.pallas_call(
        body,
        grid=(nb, n_q_lat + n_q_ctx),
        in_specs=in_specs,
        out_specs=pl.BlockSpec((TQ, qw), lambda b, i: (q_blk(b, i), 0)),
        out_shape=jax.ShapeDtypeStruct((rows, qw), BF16),
        compiler_params=_cparams(("parallel", "arbitrary")),
        name="gqa_global" if sink is None else "gqa_window",
    )(*args)


def _feat_kernel(f_ref, fp_ref, fn_ref, mu_ref, w0_ref, w2_ref, a0_ref, a2_ref, g2_ref, kk_ref, ka_ref, bd_ref,
                 r_o, k_o, v_o, kk_o, g_o, lw0_o, lw1_o, kd0_o, kd1_o, b0_o, b1_o,
                 *, n_lat_tiles, tiles_lat, tiles_ctx):
    i = pl.program_id(0)
    t = f_ref.shape[0]
    bw = B_WIDTH
    is_lat = i < n_lat_tiles
    pos = jnp.where(is_lat, i % tiles_lat, (i - n_lat_tiles) % tiles_ctx)
    last = jnp.where(is_lat, tiles_lat, tiles_ctx) - 1
    f = f_ref[...]
    prow = jnp.where(pos == 0, 0.0, fp_ref[SUBLANES - 1:SUBLANES, :])
    nrow = jnp.where(pos == last, 0.0, fn_ref[0:1, :])
    rid = lax.broadcasted_iota(jnp.int32, (t, 1), 0)
    prev = jnp.where(rid == 0, prow, pltpu.roll(f, 1, 0))
    nxt = jnp.where(rid == t - 1, nrow, pltpu.roll(f, t - 1, 0))
    fs = f + mu_ref[0:1, :] * (prev - f) + mu_ref[1:2, :] * (nxt - f)

    r = fs[:, :bw]
    k = fs[:, bw:2 * bw]
    v = fs[:, 2 * bw:3 * bw]
    wl = jnp.tanh(fs[:, 3 * bw:3 * bw + 2 * DECAY_LORA])
    al = fs[:, 3 * bw + 2 * DECAY_LORA:3 * bw + 2 * DECAY_LORA + 2 * ICLR_LORA]
    gl = fs[:, 3 * bw + 2 * DECAY_LORA + 2 * ICLR_LORA:]

    kk = k * kk_ref[...]
    kk = kk * lax.rsqrt(_headsum(kk * kk, bd_ref[...]) + EPS)
    r_o[...] = r.astype(BF16)
    k_o[...] = k.astype(BF16)
    v_o[...] = v.astype(BF16)
    kk_o[...] = kk.astype(BF16)
    g_o[...] = _dot(_sigmoid(gl).astype(BF16), g2_ref[...]).astype(BF16)
    wl = wl.astype(BF16)
    al = al.astype(BF16)
    for d, (lw_o, kd_o, b_o) in enumerate(((lw0_o, kd0_o, b0_o), (lw1_o, kd1_o, b1_o))):
        w_log = w0_ref[d:d + 1, :] + _dot(wl, w2_ref[d])
        z = -w_log
        softplus = jnp.maximum(z, 0.0) + jnp.log(1.0 + jnp.exp(-jnp.abs(z)))
        lw_o[...] = -jnp.exp(-softplus - 0.5)
        a = _sigmoid(a0_ref[d:d + 1, :] + _dot(al, a2_ref[d]))
        kd_o[...] = (k * (1.0 + (a - 1.0) * ka_ref[...])).astype(BF16)
        b_o[...] = (kk * a).astype(BF16)


def _rwkv_features(f, params, dims):
    rows, fw = f.shape
    t = dims["tm_feat"]
    n, lc, nb = dims["n"], dims["l"], dims["b"]
    tiles_lat, tiles_ctx = n // t, lc // t
    n_lat_tiles = nb * tiles_lat
    hb = t // SUBLANES
    n_halo = rows // SUBLANES
    mu, w0, w2p, a0, a2p, g2, k_k, k_a, bd = params
    bw = B_WIDTH
    const = lambda shape: pl.BlockSpec(shape, lambda i: (0,) * len(shape))
    in_specs = [
        pl.BlockSpec((t, fw), lambda i: (i, 0)),
        pl.BlockSpec((SUBLANES, fw), lambda i: (jnp.maximum(i * hb - 1, 0), 0)),
        pl.BlockSpec((SUBLANES, fw), lambda i: (jnp.minimum((i + 1) * hb, n_halo - 1), 0)),
        const((2, fw)), const((2, bw)), const((2, 2 * DECAY_LORA, bw)), const((2, bw)),
        const((2, 2 * ICLR_LORA, bw)), const((GATE_LORA, bw)), const((1, bw)), const((1, bw)), const((bw, bw)),
    ]
    out = lambda dtype: jax.ShapeDtypeStruct((rows, bw), dtype)
    return pl.pallas_call(
        functools.partial(_feat_kernel, n_lat_tiles=n_lat_tiles, tiles_lat=tiles_lat, tiles_ctx=tiles_ctx),
        grid=(rows // t,),
        in_specs=in_specs,
        out_specs=[pl.BlockSpec((t, bw), lambda i: (i, 0))] * 11,
        out_shape=[out(BF16)] * 5 + [out(F32)] * 2 + [out(BF16)] * 4,
        compiler_params=_cparams(("parallel",)),
        name="rwkv_features",
    )(f, f, f, mu, w0, w2p, a0, a2p, g2, k_k, k_a, bd)


def _stack_heads(x, head0):
    return jnp.concatenate([jnp.where(head0, x, 0.0), jnp.where(head0, 0.0, x)], axis=0).astype(BF16)


def _cumsum_rows(tri, x):
    hi = x.astype(BF16)
    r1 = x - hi.astype(F32)
    mid = r1.astype(BF16)
    lo = (r1 - mid.astype(F32)).astype(BF16)
    return _dot(tri, hi) + _dot(tri, mid) + _dot(tri, lo)


def _wkv_units(units, eye, lvl_ref):
    c = units[0]["v"].shape[0]
    p2 = 2 * c
    head0 = lax.broadcasted_iota(jnp.int32, (1, LANES), 1) < HEAD_DIM
    st = lambda x: _stack_heads(x, head0)
    fold = lambda x: x[:c] + x[c:]
    each = lambda fn: [fn(u) for u in units]

    for u in units:
        u["kq_s"], u["rq_s"], u["v_s"] = st(u["kq"]), st(u["rq"]), st(u["v"])
    a = each(lambda u: _dot_nt(jnp.concatenate([u["kq_s"], u["rq_s"]], axis=0),
                               jnp.concatenate([st(u["bi"]), st(u["ki"])], axis=0)))
    for u, au in zip(units, a):
        u["nkb"] = jnp.where(u["m_strict"], au[:p2, :p2], 0.0)
        u["akk"] = jnp.where(u["m_strict"], au[:p2, p2:], 0.0).astype(BF16)
        u["arb"] = jnp.where(u["m_incl"], au[p2:, :p2], 0.0).astype(BF16)
        u["ark"] = jnp.where(u["m_incl"], au[p2:, p2:], 0.0).astype(BF16)

    nkb_h = each(lambda u: u["nkb"].astype(BF16))
    tinv = each(lambda u: eye - u["nkb"] * lvl_ref[0].astype(F32))
    for lev in range(1, lvl_ref.shape[0]):
        t_h = [t.astype(BF16) for t in tinv]
        lt = [_dot(nh * lvl_ref[lev], th).astype(BF16) for nh, th in zip(nkb_h, t_h)]
        tinv = [t - _dot(th, x) for t, th, x in zip(tinv, t_h, lt)]
    t_h = [t.astype(BF16) for t in tinv]

    akkv = each(lambda u: _dot(u["akk"], u["v_s"]).astype(BF16))
    gp_h = [_dot(th, jnp.concatenate([u["kq_s"], x], axis=1)).astype(BF16)
            for u, th, x in zip(units, t_h, akkv)]
    corr = [_dot(u["arb"], g) for u, g in zip(units, gp_h)]
    arkv = each(lambda u: _dot(u["ark"], u["v_s"]))
    btg = [_dot_tn(st(u["bt"]), g) for u, g in zip(units, gp_h)]
    ktv = each(lambda u: _dot_tn(st(u["kt"]), u["v_s"]))

    for u, cr, av, bg, kv in zip(units, corr, arkv, btg, ktv):
        rq = u["rq"]
        rq2 = fold(jnp.concatenate([jnp.where(head0, rq, 0.0), jnp.where(head0, 0.0, rq)], axis=0) - cr[:, :LANES])
        u["rq2"] = rq2.astype(BF16)
        u["yl"] = fold(av - cr[:, LANES:])
        u["m_state"] = (jnp.where(eye > 0.0, u["decay"], 0.0) - bg[:, :LANES]).astype(BF16)
        u["n_state"] = kv - bg[:, LANES:]


def _scan_kernel(rf, vf, kkf, lwf, kdf, bf, rb, vb, kkb, lwb, kdb, bb, tri_ref, msk_ref, lvl_ref,
                 yf_o, yb_o, hf_s, hb_s):
    @pl.when(pl.program_id(1) == 0)
    def _():
        hf_s[...] = jnp.zeros_like(hf_s)
        hb_s[...] = jnp.zeros_like(hb_s)

    c = SCAN_CHUNK
    eye = msk_ref[0]
    units, chains = [], []
    for d, (refs, y_o, h_s) in enumerate((((rf, vf, kkf, lwf, kdf, bf), yf_o, hf_s),
                                          ((rb, vb, kkb, lwb, kdb, bb), yb_o, hb_s))):
        m_strict = msk_ref[1 + 2 * d] > 0.0
        m_incl = msk_ref[2 + 2 * d] > 0.0
        r, v, kk, lw, kd, b = (ref[...] for ref in refs)
        cw = _cumsum_rows(tri_ref[d], lw)
        tot = [jnp.sum(lw[j * c:(j + 1) * c], axis=0, keepdims=True) for j in range(SCAN_STEP_CHUNKS)]
        tot_rows = jnp.concatenate([jnp.broadcast_to(t, (c, t.shape[1])) for t in tot], axis=0)
        w_inv = jnp.exp(-cw)
        w_rest = jnp.exp(tot_rows - cw)
        facs = {"kq": kk * jnp.exp(cw - lw), "rq": r * jnp.exp(cw), "bi": b * w_inv, "ki": kd * w_inv,
                "bt": b * w_rest, "kt": kd * w_rest, "v": v}
        order = range(SCAN_STEP_CHUNKS) if d == 0 else range(SCAN_STEP_CHUNKS - 1, -1, -1)
        for p in range(B_WIDTH // LANES):
            ls = slice(p * LANES, (p + 1) * LANES)
            chain = []
            for j in order:
                unit = {name: x[j * c:(j + 1) * c, ls] for name, x in facs.items()}
                unit.update(decay=jnp.exp(tot[j][:, ls]), m_strict=m_strict, m_incl=m_incl, rows=slice(j * c, (j + 1) * c))
                units.append(unit)
                chain.append(unit)
            chains.append((chain, y_o, h_s, p, ls))
    _wkv_units(units, eye, lvl_ref)
    states = [h_s[p] for _, _, h_s, p, _ in chains]
    for pos in range(SCAN_STEP_CHUNKS):
        for ci, (chain, y_o, _, _, ls) in enumerate(chains):
            u = chain[pos]
            h_h = states[ci].astype(BF16)
            y_o[u["rows"], ls] = _dot(u["rq2"], h_h) + u["yl"]
            states[ci] = _dot(u["m_state"], h_h) + u["n_state"]
    for h, (_, _, h_s, p, _) in zip(states, chains):
        h_s[p] = h


def _scan_constants():
    c, p2 = SCAN_CHUNK, PAIR_ROWS
    t = np.arange(SCAN_STEP_CHUNKS * c)
    same_chunk = (t[None, :] // c) == (t[:, None] // c)
    tri = np.stack([same_chunk & (t[None, :] <= t[:, None]), same_chunk & (t[None, :] >= t[:, None])]).astype(np.float32)
    i = np.arange(p2)
    same_head = (i[:, None] // c) == (i[None, :] // c)
    ti, si = i[:, None] % c, i[None, :] % c
    msk = np.stack([
        np.eye(p2, dtype=bool),
        same_head & (si < ti), same_head & (si <= ti),
        same_head & (si > ti), same_head & (si >= ti),
    ]).astype(np.float32)
    n_lev = int(np.log2(c))
    lvl = np.stack([
        ((i[:, None] >> (k + 1)) == (i[None, :] >> (k + 1))) & ((i[:, None] >> k) != (i[None, :] >> k))
        for k in range(n_lev)
    ]).astype(np.float32)
    return jnp.asarray(tri, BF16), jnp.asarray(msk), jnp.asarray(lvl, BF16)


def _wkv_scan(feats, dims):
    r, v, kk, lw0, lw1, kd0, kd1, b0, b1 = feats
    rows, bw = r.shape
    c = SCAN_STEP_CHUNKS * SCAN_CHUNK
    n, lc, nb = dims["n"], dims["l"], dims["b"]
    n_c, l_c = n // c, lc // c
    ctx_base = nb * n_c
    tri, msk, lvl = _scan_constants()

    def fwd(b, s):
        return (jnp.where(s < l_c, ctx_base + b * l_c + s, b * n_c + (s - l_c)), 0)

    def bwd(b, s):
        return (jnp.where(s < l_c, ctx_base + b * l_c + (l_c - 1 - s), b * n_c + (n_c - 1 - (s - l_c))), 0)

    const = lambda a: pl.BlockSpec(a.shape, lambda b, s: (0,) * a.ndim)
    out = jax.ShapeDtypeStruct((rows, bw), F32)
    return pl.pallas_call(
        _scan_kernel,
        grid=(nb, l_c + n_c),
        in_specs=[pl.BlockSpec((c, bw), fwd)] * 6 + [pl.BlockSpec((c, bw), bwd)] * 6
                 + [const(tri), const(msk), const(lvl)],
        out_specs=[pl.BlockSpec((c, bw), fwd), pl.BlockSpec((c, bw), bwd)],
        out_shape=[out, out],
        scratch_shapes=[pltpu.VMEM((bw // LANES, LANES, LANES), F32)] * 2,
        compiler_params=_cparams(("parallel", "arbitrary")),
        name="wkv_scan",
    )(r, v, kk, lw0, kd0, b0, r, v, kk, lw1, kd1, b1, tri, msk, lvl)


def _outproj_kernel(x_ref, mod_ref, wo_ref, oa_ref, *rest, even):
    o_ref = rest[-1]
    if even:
        yf, yb, r, k, v, g, rk, gnw, gnb, bd = rest[:-1]
        y = yf[...] + yb[...]
        mean = _headsum(y, bd[...]) * (1.0 / HEAD_DIM)
        yc = y - mean
        var = _headsum(yc * yc, bd[...]) * (1.0 / HEAD_DIM)
        yn = yc * lax.rsqrt(var + GN_EPS) * gnw[...] + gnb[...]
        up = lambda ref: ref[...].astype(F32)
        bonus = _headsum(up(r) * up(k) * rk[...], bd[...]) * up(v)
        ob = ((yn + bonus) * up(g)).astype(BF16)
        half = oa_ref.shape[1]
        o = _dot(oa_ref[...], wo_ref[:half, :]) + _dot(ob, wo_ref[half:, :])
    else:
        o = _dot(oa_ref[...], wo_ref[...])
    o_ref[...] = x_ref[...] + mod_ref[5:6, :] * o


def _outproj(xs, mods, w_out, l, e, o_att, dims, readout=None):
    rows, d = xs.shape
    tm = dims["tm_proj"]
    tiles_per_batch = dims["n"] // tm
    nb = dims["b"]
    even = readout is not None
    in_specs = [
        pl.BlockSpec((tm, d), lambda i: (i, 0)),
        pl.BlockSpec((None, None, N_MOD, d), lambda i: (l, jnp.minimum(i // tiles_per_batch, nb), 0, 0)),
        pl.BlockSpec((None, d, d), lambda i: (e, 0, 0)),
        pl.BlockSpec((tm, o_att.shape[1]), lambda i: (i, 0)),
    ]
    args = [xs, mods, w_out, o_att]
    if even:
        yf, yb, r, k, v, g, rk, gnw, gnb, bd = readout
        bw = B_WIDTH
        in_specs += [pl.BlockSpec((tm, bw), lambda i: (i, 0))] * 6
        in_specs += [pl.BlockSpec((1, bw), lambda i: (0, 0))] * 3 + [pl.BlockSpec((bw, bw), lambda i: (0, 0))]
        args += [yf, yb, r, k, v, g, rk, gnw, gnb, bd]
    return pl.pallas_call(
        functools.partial(_outproj_kernel, even=even),
        grid=(rows // tm,),
        in_specs=in_specs,
        out_specs=pl.BlockSpec((tm, d), lambda i: (i, 0)),
        out_shape=jax.ShapeDtypeStruct((rows, d), F32),
        compiler_params=_cparams(("parallel",)),
        name="mixer_out",
    )(*args)


def _rope_tables(n, tm):
    rows = n // GRID_W
    row = jnp.repeat(jnp.arange(rows, dtype=F32), GRID_W)
    col = jnp.tile(jnp.arange(GRID_W, dtype=F32), rows)
    n_freq = HEAD_DIM // 4
    inv_freq = ROPE_THETA ** (-jnp.arange(n_freq, dtype=F32) / n_freq)
    ang = jnp.concatenate([row[:, None] * inv_freq, col[:, None] * inv_freq], axis=-1)
    cos = jnp.repeat(jnp.cos(ang), 2, axis=-1)
    sin = jnp.repeat(jnp.sin(ang), 2, axis=-1)
    even_lane = (jnp.arange(HEAD_DIM) % 2 == 0)[None, :]
    sin_a = jnp.where(even_lane, -sin, 0.0)
    sin_b = jnp.where(even_lane, 0.0, sin)
    pad = lambda tab, fill: jnp.concatenate(
        [jnp.tile(tab, (1, LANES // HEAD_DIM)), jnp.full((tm, LANES), fill, F32)], axis=0)
    return pad(cos, 1.0), pad(sin_a, 0.0), pad(sin_b, 0.0)


def _block_diag(width, value, dtype):
    h = np.arange(width) // HEAD_DIM
    return jnp.asarray((h[:, None] == h[None, :]) * value, dtype)


def _pad_lora(w):
    r = w.shape[1]
    z = jnp.zeros_like(w[0])
    return jnp.stack([jnp.concatenate([w[0], z], axis=0), jnp.concatenate([z, w[1]], axis=0)])


def _largest_tile(limit, *sizes):
    t = limit
    while any(s % t for s in sizes):
        t //= 2
    return t


def kernel(x, c, ctx, c_ctx, w_mod, b_mod, ffn_in, ffn_out, even_w_in, even_w_out, q_gain, k_gain, rwkv_mu, rwkv_w0,
           rwkv_w2, rwkv_a0, rwkv_a2, rwkv_g2, rwkv_k_k, rwkv_k_a, rwkv_r_k, rwkv_gn_w, rwkv_gn_b, odd_w_in, odd_w_out,
           sink, final_gain):
    nb, n, d = x.shape
    lc = ctx.shape[1]
    depth = w_mod.shape[0]
    dff = ffn_out.shape[2]
    assert n % TQ == 0 and lc % TQ == 0 and n >= TQ + 2 * WINDOW and n % GRID_W == 0
    scan_rows = SCAN_STEP_CHUNKS * SCAN_CHUNK
    assert n % scan_rows == 0 and lc % scan_rows == 0 and nb + 1 <= MOD_ROWS
    dims = {
        "b": nb, "n": n, "l": lc,
        "tm_ffn": _largest_tile(512, n, nb * lc),
        "tm_proj": _largest_tile(512, n, nb * lc),
        "tm_feat": _largest_tile(256, n, lc),
        "key_chunk": _largest_tile(1024, n),
    }

    cvec = jnp.zeros((MOD_ROWS, d), F32).at[:nb].set(c).at[nb].set(c_ctx)
    mods = _mod_table(cvec, w_mod, b_mod).reshape(depth, MOD_ROWS, N_MOD, d)

    ffn_in_h = ffn_in.astype(BF16)
    ffn_out_h = ffn_out.astype(BF16)
    even_in_h = even_w_in.astype(BF16)
    even_out_h = even_w_out.astype(BF16)
    odd_in_h = odd_w_in.astype(BF16)
    odd_out_h = odd_w_out.astype(BF16)

    rope_tabs = _rope_tables(n, dims["tm_proj"])
    a_kw = A_KV_HEADS * HEAD_DIM
    a_qw = A_HEADS * HEAD_DIM
    bd_q = _block_diag(a_qw, 1.0 / HEAD_DIM, BF16)
    bd_k = _block_diag(a_kw, 1.0 / HEAD_DIM, BF16)
    bd_ones = _block_diag(B_WIDTH, 1.0, BF16)
    fw = even_w_in.shape[2] - a_qw - 2 * a_kw

    xs = jnp.concatenate([x.reshape(nb * n, d), ctx.reshape(nb * lc, d)], axis=0)
    for l in range(depth):
        last = l == depth - 1
        xs = _ffn(xs, mods, ffn_in_h, ffn_out_h, l, 0, 0, dims)
        if l % 2 == 0:
            e = l // 2
            tile_h = lambda g, reps: jnp.tile(g, reps).reshape(1, -1)
            norm_args = (tile_h(q_gain[e], A_HEADS), tile_h(k_gain[e], A_KV_HEADS), bd_q, bd_k)
            q, k, v, f = _proj(xs, mods, even_in_h, l, e, rope_tabs, dims, a_qw, a_kw, fw, norm_args)
            oa = _attention(q, k, v, dims)
            feat_params = (rwkv_mu[e], rwkv_w0[e], _pad_lora(rwkv_w2[e]).astype(BF16), rwkv_a0[e],
                           _pad_lora(rwkv_a2[e]).astype(BF16), rwkv_g2[e].astype(BF16),
                           rwkv_k_k[e].reshape(1, -1), rwkv_k_a[e].reshape(1, -1), bd_ones)
            r, kr, vr, kk, g, lw0, lw1, kd0, kd1, b0, b1 = _rwkv_features(f, feat_params, dims)
            yf, yb = _wkv_scan((r, vr, kk, lw0, lw1, kd0, kd1, b0, b1), dims)
            readout = (yf, yb, r, kr, vr, g, rwkv_r_k[e].reshape(1, -1), rwkv_gn_w[e].reshape(1, -1),
                       rwkv_gn_b[e].reshape(1, -1), bd_ones)
            xs = _outproj(xs, mods, even_out_h, l, e, oa, dims, readout)
        else:
            o = l // 2
            c_kw = C_KV_HEADS * HEAD_DIM
            c_qw = C_HEADS * HEAD_DIM
            q, k, v = _proj(xs, mods, odd_in_h, l, o, rope_tabs, dims, c_qw, c_kw, 0)
            oc = _attention(q, k, v, dims, sink=sink[o])
            xs = _outproj(xs, mods, odd_out_h, l, o, oc, dims)
        xs = _ffn(xs, mods, ffn_in_h, ffn_out_h, l, 1, 6, dims, final_gain=final_gain if last else None)
    return xs.reshape(nb, n, d)
```

```python
import functools

import jax
import jax.numpy as jnp
import numpy as np
from jax import lax
from jax.experimental import pallas as pl
from jax.experimental.pallas import tpu as pltpu

F32 = jnp.float32
BF16 = jnp.bfloat16

HEAD_DIM = 64
GRID_W = 64
A_HEADS, A_KV_HEADS = 8, 2
B_HEADS = 8
B_WIDTH = B_HEADS * HEAD_DIM
DECAY_LORA, ICLR_LORA, GATE_LORA = 64, 64, 128
C_HEADS, C_KV_HEADS = 16, 4
GQA_GROUP = 4
WINDOW = 128
N_MOD = 9
ROPE_THETA = 10000.0
EPS = 1e-6
GN_EPS = 64e-5
NEG_BIG = -1e30
LOG2_E = 1.4426950408889634

LANES = 128
SUBLANES = 8
VMEM_LIMIT_BYTES = 56 * 1024 * 1024

SCAN_CHUNK = 64
SCAN_STEP_CHUNKS = 2
PAIR_ROWS = 2 * SCAN_CHUNK
TQ = 128
AHEAD = 4
ONES_ROWS = 16
MOD_ROWS = 16


def _cparams(sem):
    return pltpu.CompilerParams(dimension_semantics=sem, vmem_limit_bytes=VMEM_LIMIT_BYTES)


def _dot(a, b, precision=None):
    return jnp.dot(a, b, preferred_element_type=F32, precision=precision)


def _dot_nt(a, b, precision=None):
    return lax.dot_general(a, b, (((1,), (1,)), ((), ())), preferred_element_type=F32, precision=precision)


def _dot_tn(a, b, precision=None):
    return lax.dot_general(a, b, (((0,), (0,)), ((), ())), preferred_element_type=F32, precision=precision)


def _rms(x):
    return x * lax.rsqrt(jnp.mean(x * x, axis=-1, keepdims=True) + EPS)


def _sigmoid(x):
    return 1.0 / (1.0 + jnp.exp(-x))


def _headsum(x, bd):
    hi = x.astype(BF16)
    lo = (x - hi.astype(F32)).astype(BF16)
    return _dot(hi, bd) + _dot(lo, bd)


def _mod_kernel(c_ref, w_ref, b_ref, o_ref):
    c = c_ref[...]
    s = c * _sigmoid(c)
    hi = s.astype(BF16)
    lo = (s - hi.astype(F32)).astype(BF16)
    w = w_ref[...].astype(BF16)
    o_ref[...] = _dot(hi, w) + _dot(lo, w) + b_ref[...]


def _mod_table(cvec, w_mod, b_mod):
    depth, d, nd = w_mod.shape
    tn = 1536 if nd % 1536 == 0 else nd
    return pl.pallas_call(
        _mod_kernel,
        grid=(depth, nd // tn),
        in_specs=[
            pl.BlockSpec((MOD_ROWS, d), lambda l, n: (0, 0)),
            pl.BlockSpec((None, d, tn), lambda l, n: (l, 0, n)),
            pl.BlockSpec((None, 1, tn), lambda l, n: (l, 0, n)),
        ],
        out_specs=pl.BlockSpec((None, MOD_ROWS, tn), lambda l, n: (l, 0, n)),
        out_shape=jax.ShapeDtypeStruct((depth, MOD_ROWS, nd), F32),
        compiler_params=_cparams(("parallel", "parallel")),
        name="mod_table",
    )(cvec, w_mod, b_mod.reshape(depth, 1, nd))


def _ffn_kernel(x_ref, mod_ref, wg_ref, wu_ref, wo_ref, *rest, i_shift, final, n_lat_tiles):
    o_ref = rest[-1]
    x = x_ref[...]
    if n_lat_tiles is not None:
        x = jnp.where(pl.program_id(0) < n_lat_tiles, x, rest[0][...])
    hn = (_rms(x) * (1.0 + mod_ref[i_shift + 1:i_shift + 2, :]) + mod_ref[i_shift:i_shift + 1, :]).astype(BF16)
    g = _dot(hn, wg_ref[...])
    u = _dot(hn, wu_ref[...])
    a = (g * _sigmoid(g) * u).astype(BF16)
    y = x + (0.5 * mod_ref[i_shift + 2:i_shift + 3, :]) * _dot(a, wo_ref[...])
    if final:
        y = _rms(y) * rest[-2][...]
    o_ref[...] = y


def _ffn(xs, mods, w_in, w_out, l, j, i_shift, dims, final_gain=None, ctx_rows=None):
    rows, d = xs.shape
    dff = w_out.shape[2]
    tm = dims["tm_ffn"]
    tiles_per_batch = dims["n"] // tm
    nb = dims["b"]
    n_lat_tiles = nb * tiles_per_batch
    final = final_gain is not None
    if final:
        rows = nb * dims["n"]
    split = ctx_rows is not None
    if split:
        rows = rows + ctx_rows.shape[0]
    in_specs = [
        pl.BlockSpec((tm, d), (lambda i: (jnp.minimum(i, n_lat_tiles - 1), 0)) if split else (lambda i: (i, 0))),
        pl.BlockSpec((None, None, N_MOD, d), lambda i: (l, jnp.minimum(i // tiles_per_batch, nb), 0, 0)),
        pl.BlockSpec((None, None, d, dff), lambda i: (l, j, 0, 0)),
        pl.BlockSpec((None, None, d, dff), lambda i: (l, j, 0, 1)),
        pl.BlockSpec((None, None, dff, d), lambda i: (l, j, 0, 0)),
    ]
    args = [xs, mods, w_in, w_in, w_out]
    if split:
        in_specs.append(pl.BlockSpec((tm, d), lambda i: (jnp.maximum(i - n_lat_tiles, 0), 0)))
        args.append(ctx_rows)
    if final:
        in_specs.append(pl.BlockSpec((1, d), lambda i: (0, 0)))
        args.append(final_gain.reshape(1, d))
    return pl.pallas_call(
        functools.partial(_ffn_kernel, i_shift=i_shift, final=final, n_lat_tiles=n_lat_tiles if split else None),
        grid=(rows // tm,),
        in_specs=in_specs,
        out_specs=pl.BlockSpec((tm, d), lambda i: (i, 0)),
        out_shape=jax.ShapeDtypeStruct((rows, d), F32),
        compiler_params=_cparams(("parallel",)),
        name="ffn",
    )(*args)


def _rope(x, cos, sin_a, sin_b):
    outs = []
    for g in range(x.shape[1] // LANES):
        xg = x[:, g * LANES:(g + 1) * LANES]
        nxt = pltpu.roll(xg, LANES - 1, 1)
        prv = pltpu.roll(xg, 1, 1)
        outs.append(xg * cos + nxt * sin_a + prv * sin_b)
    return outs[0] if len(outs) == 1 else jnp.concatenate(outs, axis=1)


def _proj_kernel(x_ref, mod_ref, w_ref, cos_ref, sa_ref, sb_ref, *rest, qw, kw, fw, qk_norm):
    if qk_norm:
        qg_ref, kg_ref, bdq_ref, bdk_ref = rest[:4]
        rest = rest[4:]
    q_ref, k_ref, v_ref = rest[:3]
    hn = (_rms(x_ref[...]) * (1.0 + mod_ref[4:5, :]) + mod_ref[3:4, :]).astype(BF16)
    q = _dot(hn, w_ref[:, :qw])
    k = _dot(hn, w_ref[:, qw:qw + kw])
    v = _dot(hn, w_ref[:, qw + kw:qw + 2 * kw])
    if qk_norm:
        q = q * lax.rsqrt(_headsum(q * q, bdq_ref[...]) + EPS) * qg_ref[...]
        k = k * lax.rsqrt(_headsum(k * k, bdk_ref[...]) + EPS) * kg_ref[...]
    cos, sa, sb = cos_ref[...], sa_ref[...], sb_ref[...]
    q_ref[...] = (_rope(q, cos, sa, sb) * (HEAD_DIM ** -0.5 * LOG2_E)).T.astype(BF16)
    k_ref[...] = _rope(k, cos, sa, sb).astype(BF16)
    v_ref[...] = v.T.astype(BF16)
    if fw:
        rest[3][...] = _dot(hn, w_ref[:, qw + 2 * kw:])


def _proj(xs, mods, w_in, l, e, rope_tabs, dims, qw, kw, fw, norm_args=None):
    rows, d = xs.shape
    tm = dims["tm_proj"]
    cols = qw + 2 * kw + fw
    tiles_per_batch = dims["n"] // tm
    n_lat_tiles = dims["b"] * tiles_per_batch
    nb = dims["b"]

    def tab_idx(i):
        return (jnp.where(i < n_lat_tiles, i % tiles_per_batch, tiles_per_batch), 0)

    in_specs = [
        pl.BlockSpec((tm, d), lambda i: (i, 0)),
        pl.BlockSpec((None, None, N_MOD, d), lambda i: (l, jnp.minimum(i // tiles_per_batch, nb), 0, 0)),
        pl.BlockSpec((None, d, cols), lambda i: (e, 0, 0)),
        pl.BlockSpec((tm, LANES), tab_idx),
        pl.BlockSpec((tm, LANES), tab_idx),
        pl.BlockSpec((tm, LANES), tab_idx),
    ]
    args = [xs, mods, w_in, *rope_tabs]
    qk_norm = norm_args is not None
    if qk_norm:
        qg, kg, bdq, bdk = norm_args
        in_specs += [
            pl.BlockSpec((1, qw), lambda i: (0, 0)),
            pl.BlockSpec((1, kw), lambda i: (0, 0)),
            pl.BlockSpec((qw, qw), lambda i: (0, 0)),
            pl.BlockSpec((kw, kw), lambda i: (0, 0)),
        ]
        args += [qg, kg, bdq, bdk]
    out_specs = [
        pl.BlockSpec((qw, tm), lambda i: (0, i)),
        pl.BlockSpec((tm, kw), lambda i: (i, 0)),
        pl.BlockSpec((kw, tm), lambda i: (0, i)),
    ]
    out_shape = [
        jax.ShapeDtypeStruct((qw, rows), BF16),
        jax.ShapeDtypeStruct((rows, kw), BF16),
        jax.ShapeDtypeStruct((kw, rows), BF16),
    ]
    if fw:
        out_specs.append(pl.BlockSpec((tm, fw), lambda i: (i, 0)))
        out_shape.append(jax.ShapeDtypeStruct((rows, fw), F32))
    return pl.pallas_call(
        functools.partial(_proj_kernel, qw=qw, kw=kw, fw=fw, qk_norm=qk_norm),
        grid=(rows // tm,),
        in_specs=in_specs,
        out_specs=out_specs,
        out_shape=out_shape,
        compiler_params=_cparams(("parallel",)),
        name="mixer_proj",
    )(*args)


def _attend(qt_ref, segs, sink_ref, n_pad, o_ref):
    tq = qt_ref.shape[1]
    cols = GQA_GROUP * tq
    heads = range(qt_ref.shape[0] // (GQA_GROUP * HEAD_DIM))
    pair = lambda h: slice((h // 2) * LANES, (h // 2 + 1) * LANES)
    qpad, m, acc = [], [], []
    for h in heads:
        q4 = jnp.concatenate([qt_ref[(h * GQA_GROUP + g) * HEAD_DIM:(h * GQA_GROUP + g + 1) * HEAD_DIM, :]
                              for g in range(GQA_GROUP)], axis=1)
        z = jnp.zeros_like(q4)
        qpad.append(jnp.concatenate([q4, z] if h % 2 == 0 else [z, q4], axis=0))
        if sink_ref is not None:
            mh, lh = sink_ref[h], jnp.ones((1, cols), F32)
        else:
            mh, lh = jnp.full((1, cols), NEG_BIG, F32), jnp.zeros((1, cols), F32)
        if n_pad is not None:
            m_new = jnp.maximum(mh, jnp.where(n_pad > 0.0, 0.0, NEG_BIG))
            lh = lh * jnp.exp2(mh - m_new) + jnp.where(n_pad > 0.0, n_pad * jnp.exp2(-m_new), 0.0)
            mh = m_new
        m.append(mh)
        acc.append(jnp.concatenate([jnp.zeros((HEAD_DIM, cols), F32), jnp.broadcast_to(lh, (ONES_ROWS, cols))], axis=0))
    def scores(item):
        (k, _, valid), h = item
        s = _dot(k[:, pair(h)], qpad[h])
        if valid is not None:
            s = jnp.where(jnp.concatenate([valid] * GQA_GROUP, axis=1), s, NEG_BIG)
        return s

    items = [(seg, h) for seg in segs for h in heads]
    pending = [scores(it) for it in items[:AHEAD]]
    for idx, ((_, vt, _), h) in enumerate(items):
        s = pending.pop(0)
        if idx + AHEAD < len(items):
            pending.append(scores(items[idx + AHEAD]))
        m_new = jnp.maximum(m[h], jnp.max(s, axis=0, keepdims=True))
        alpha = jnp.exp2(m[h] - m_new)
        p = jnp.exp2(s - m_new).astype(BF16)
        v1 =jnp.concatenate([vt[h * HEAD_DIM:(h + 1) * HEAD_DIM, :], jnp.ones((ONES_ROWS, vt.shape[1]), BF16)], axis=0)
        acc[h] = alpha * acc[h] + _dot(v1, p)
        m[h] = m_new
    blocks = []
    for h in heads:
        ot = acc[h][:HEAD_DIM, :] * (1.0 / acc[h][HEAD_DIM:HEAD_DIM + 1, :])
        for g in range(0, GQA_GROUP, 2):
            blocks.append(jnp.concatenate([ot[:, g * tq:(g + 1) * tq], ot[:, (g + 1) * tq:(g + 2) * tq]], axis=0).T)
    o_ref[...] = jnp.concatenate(blocks, axis=1).astype(o_ref.dtype)


def _attn_global_kernel(qt_ref, kl_ref, vl_ref, kc_ref, vc_ref, o_ref, *, n_q_lat, key_chunk):
    i = pl.program_id(1)
    ctx_seg = (kc_ref[...], vc_ref[...], None)

    @pl.when(i < n_q_lat)
    def _():
        segs = [(kl_ref[c * key_chunk:(c + 1) * key_chunk, :], vl_ref[:, c * key_chunk:(c + 1) * key_chunk], None)
                for c in range(kl_ref.shape[0] // key_chunk)]
        _attend(qt_ref, segs + [ctx_seg], None, None, o_ref)

    @pl.when(i >= n_q_lat)
    def _():
        _attend(qt_ref, [ctx_seg], None, None, o_ref)


def _attn_window_kernel(qt_ref, k0_ref, k1_ref, k2_ref, v0_ref, v1_ref, v2_ref, kc_ref, vc_ref, sink_ref, o_ref,
                        *, n, n_q_lat):
    i = pl.program_id(1)
    tq = qt_ref.shape[1]
    ctx_seg = (kc_ref[...], vc_ref[...], None)

    @pl.when(i < n_q_lat)
    def _():
        k = jnp.concatenate([k0_ref[...], k1_ref[...], k2_ref[...]], axis=0)
        vt = jnp.concatenate([v0_ref[...], v1_ref[...], v2_ref[...]], axis=1)
        kpos = (i - 1) * tq + lax.broadcasted_iota(jnp.int32, (3 * tq, tq), 0)
        qpos = i * tq + lax.broadcasted_iota(jnp.int32, (3 * tq, tq), 1)
        valid = (jnp.abs(kpos - qpos) <= WINDOW) & (kpos >= WINDOW) & (kpos < n)
        q_row = i * tq + lax.broadcasted_iota(jnp.int32, (1, tq), 1)
        n_pad = jnp.maximum(q_row + WINDOW - n + 1, 0).astype(F32)
        n_pad = jnp.concatenate([n_pad] * GQA_GROUP, axis=1)
        _attend(qt_ref, [(k, vt, valid), ctx_seg], sink_ref, n_pad, o_ref)

    @pl.when(i >= n_q_lat)
    def _():
        _attend(qt_ref, [ctx_seg], sink_ref, None, o_ref)


def _attention(qt, k, vt, dims, sink=None, ctx_queries=True):
    qw, rows = qt.shape
    kw = k.shape[1]
    n, lc, nb = dims["n"], dims["l"], dims["b"]
    n_q_lat = n // TQ
    n_q_ctx = lc // TQ
    lat_blocks = nb * n_q_lat
    ctx_block = nb * n // lc

    def q_blk(b, i):
        return jnp.where(i < n_q_lat, b * n_q_lat + i, lat_blocks + b * n_q_ctx + (i - n_q_lat))

    ctx_specs = [pl.BlockSpec((lc, kw), lambda b, i: (ctx_block + b, 0)),
                 pl.BlockSpec((kw, lc), lambda b, i: (0, ctx_block + b))]
    qt_spec = pl.BlockSpec((qw, TQ), lambda b, i: (0, q_blk(b, i)))
    if sink is None:
        body = functools.partial(_attn_global_kernel, n_q_lat=n_q_lat, key_chunk=dims["key_chunk"])
        in_specs = [qt_spec,
                    pl.BlockSpec((n, kw), lambda b, i: (b, 0)),
                    pl.BlockSpec((kw, n), lambda b, i: (0, b))] + ctx_specs
        args = [qt, k, vt, k, vt]
    else:
        assert WINDOW == TQ
        body = functools.partial(_attn_window_kernel, n=n, n_q_lat=n_q_lat)
        nbr = lambda b, i, off: b * n_q_lat + jnp.clip(i + off, 0, n_q_lat - 1)
        n_kv = kw // HEAD_DIM
        in_specs = ([qt_spec]
                    + [pl.BlockSpec((TQ, kw), functools.partial(lambda b, i, off: (nbr(b, i, off), 0), off=off))
                       for off in (-1, 0, 1)]
                    + [pl.BlockSpec((kw, TQ), functools.partial(lambda b, i, off: (0, nbr(b, i, off)), off=off))
                       for off in (-1, 0, 1)]
                    + ctx_specs
                    + [pl.BlockSpec((n_kv, 1, GQA_GROUP * TQ), lambda b, i: (0, 0, 0))])
        sink_rows = jnp.repeat(sink.astype(F32).reshape(n_kv, 1, GQA_GROUP) * LOG2_E, TQ, axis=2)
        args = [qt, k, k, k, vt, vt, vt, k, vt, sink_rows]
    return pl.pallas_call(
        body,
        grid=(nb, n_q_lat + (n_q_ctx if ctx_queries else 0)),
        in_specs=in_specs,
        out_specs=pl.BlockSpec((TQ, qw), lambda b, i: (q_blk(b, i), 0)),
        out_shape=jax.ShapeDtypeStruct((rows if ctx_queries else nb * n, qw), BF16),
        compiler_params=_cparams(("parallel", "arbitrary")),
        name="gqa_global" if sink is None else "gqa_window",
    )(*args)


def _feat_kernel(f_ref, fp_ref, fn_ref, mu_ref, w0_ref, w2_ref, a0_ref, a2_ref, g2_ref, kk_ref, ka_ref, bd_ref,
                 r_o, k_o, v_o, kk_o, g_o, lw0_o, lw1_o, kd0_o, kd1_o, b0_o, b1_o,
                 *, n_lat_tiles, tiles_lat, tiles_ctx):
    i = pl.program_id(0)
    t = f_ref.shape[0]
    bw = B_WIDTH
    is_lat = i < n_lat_tiles
    pos = jnp.where(is_lat, i % tiles_lat, (i - n_lat_tiles) % tiles_ctx)
    last = jnp.where(is_lat, tiles_lat, tiles_ctx) - 1
    f = f_ref[...]
    prow = jnp.where(pos == 0, 0.0, fp_ref[SUBLANES - 1:SUBLANES, :])
    nrow = jnp.where(pos == last, 0.0, fn_ref[0:1, :])
    rid = lax.broadcasted_iota(jnp.int32, (t, 1), 0)
    prev = jnp.where(rid == 0, prow, pltpu.roll(f, 1, 0))
    nxt = jnp.where(rid == t - 1, nrow, pltpu.roll(f, t - 1, 0))
    fs = f + mu_ref[0:1, :] * (prev - f) + mu_ref[1:2, :] * (nxt - f)

    r = fs[:, :bw]
    k = fs[:, bw:2 * bw]
    v = fs[:, 2 * bw:3 * bw]
    wl = jnp.tanh(fs[:, 3 * bw:3 * bw + 2 * DECAY_LORA])
    al = fs[:, 3 * bw + 2 * DECAY_LORA:3 * bw + 2 * DECAY_LORA + 2 * ICLR_LORA]
    gl = fs[:, 3 * bw + 2 * DECAY_LORA + 2 * ICLR_LORA:]

    kk = k * kk_ref[...]
    kk = kk * lax.rsqrt(_headsum(kk * kk, bd_ref[...]) + EPS)
    r_o[...] = r.astype(BF16)
    k_o[...] = k.astype(BF16)
    v_o[...] = v.astype(BF16)
    kk_o[...] = kk.astype(BF16)
    g_o[...] = _dot(_sigmoid(gl).astype(BF16), g2_ref[...]).astype(BF16)
    wl = wl.astype(BF16)
    al = al.astype(BF16)
    for d, (lw_o, kd_o, b_o) in enumerate(((lw0_o, kd0_o, b0_o), (lw1_o, kd1_o, b1_o))):
        w_log = w0_ref[d:d + 1, :] + _dot(wl, w2_ref[d])
        z = -w_log
        softplus = jnp.maximum(z, 0.0) + jnp.log(1.0 + jnp.exp(-jnp.abs(z)))
        lw_o[...] = -jnp.exp(-softplus - 0.5)
        a = _sigmoid(a0_ref[d:d + 1, :] + _dot(al, a2_ref[d]))
        kd_o[...] = (k * (1.0 + (a - 1.0) * ka_ref[...])).astype(BF16)
        b_o[...] = (kk * a).astype(BF16)


def _rwkv_features(f, params, dims):
    rows, fw = f.shape
    t = dims["tm_feat"]
    n, lc, nb = dims["n"], dims["l"], dims["b"]
    tiles_lat, tiles_ctx = n // t, lc // t
    n_lat_tiles = nb * tiles_lat
    hb = t // SUBLANES
    n_halo = rows // SUBLANES
    mu, w0, w2p, a0, a2p, g2, k_k, k_a, bd = params
    bw = B_WIDTH
    const = lambda shape: pl.BlockSpec(shape, lambda i: (0,) * len(shape))
    in_specs = [
        pl.BlockSpec((t, fw), lambda i: (i, 0)),
        pl.BlockSpec((SUBLANES, fw), lambda i: (jnp.maximum(i * hb - 1, 0), 0)),
        pl.BlockSpec((SUBLANES, fw), lambda i: (jnp.minimum((i + 1) * hb, n_halo - 1), 0)),
        const((2, fw)), const((2, bw)), const((2, 2 * DECAY_LORA, bw)), const((2, bw)),
        const((2, 2 * ICLR_LORA, bw)), const((GATE_LORA, bw)), const((1, bw)), const((1, bw)), const((bw, bw)),
    ]
    out = lambda dtype: jax.ShapeDtypeStruct((rows, bw), dtype)
    return pl.pallas_call(
        functools.partial(_feat_kernel, n_lat_tiles=n_lat_tiles, tiles_lat=tiles_lat, tiles_ctx=tiles_ctx),
        grid=(rows // t,),
        in_specs=in_specs,
        out_specs=[pl.BlockSpec((t, bw), lambda i: (i, 0))] * 11,
        out_shape=[out(BF16)] * 5 + [out(F32)] * 2 + [out(BF16)] * 4,
        compiler_params=_cparams(("parallel",)),
        name="rwkv_features",
    )(f, f, f, mu, w0, w2p, a0, a2p, g2, k_k, k_a, bd)


def _stack_heads(x, head0):
    return jnp.concatenate([jnp.where(head0, x, 0.0), jnp.where(head0, 0.0, x)], axis=0).astype(BF16)


def _cumsum_rows(tri, x):
    hi = x.astype(BF16)
    r1 = x - hi.astype(F32)
    mid = r1.astype(BF16)
    lo = (r1 - mid.astype(F32)).astype(BF16)
    return _dot(tri, hi) + _dot(tri, mid) + _dot(tri, lo)


def _wkv_units(units, eye, lvl_ref):
    c = units[0]["v"].shape[0]
    p2 = 2 * c
    head0 = lax.broadcasted_iota(jnp.int32, (1, LANES), 1) < HEAD_DIM
    st = lambda x: _stack_heads(x, head0)
    fold = lambda x: x[:c] + x[c:]
    each = lambda fn: [fn(u) for u in units]

    for u in units:
        u["kq_s"], u["rq_s"], u["v_s"] = st(u["kq"]), st(u["rq"]), st(u["v"])
    a = each(lambda u: _dot_nt(jnp.concatenate([u["kq_s"], u["rq_s"]], axis=0),
                               jnp.concatenate([st(u["bi"]), st(u["ki"])], axis=0)))
    for u, au in zip(units, a):
        u["nkb"] = jnp.where(u["m_strict"], au[:p2, :p2], 0.0)
        u["akk"] = jnp.where(u["m_strict"], au[:p2, p2:], 0.0).astype(BF16)
        u["arb"] = jnp.where(u["m_incl"], au[p2:, :p2], 0.0).astype(BF16)
        u["ark"] = jnp.where(u["m_incl"], au[p2:, p2:], 0.0).astype(BF16)

    nkb_h = each(lambda u: u["nkb"].astype(BF16))
    tinv = each(lambda u: eye - u["nkb"] * lvl_ref[0].astype(F32))
    for lev in range(1, lvl_ref.shape[0]):
        t_h = [t.astype(BF16) for t in tinv]
        lt = [_dot(nh * lvl_ref[lev], th).astype(BF16) for nh, th in zip(nkb_h, t_h)]
        tinv = [t - _dot(th, x) for t, th, x in zip(tinv, t_h, lt)]
    t_h = [t.astype(BF16) for t in tinv]

    akkv = each(lambda u: _dot(u["akk"], u["v_s"]).astype(BF16))
    gp_h = [_dot(th, jnp.concatenate([u["kq_s"], x], axis=1)).astype(BF16)
            for u, th, x in zip(units, t_h, akkv)]
    corr = [_dot(u["arb"], g) for u, g in zip(units, gp_h)]
    arkv = each(lambda u: _dot(u["ark"], u["v_s"]))
    btg = [_dot_tn(st(u["bt"]), g) for u, g in zip(units, gp_h)]
    ktv = each(lambda u: _dot_tn(st(u["kt"]), u["v_s"]))

    for u, cr, av, bg, kv in zip(units, corr, arkv, btg, ktv):
        rq = u["rq"]
        rq2 = fold(jnp.concatenate([jnp.where(head0, rq, 0.0), jnp.where(head0, 0.0, rq)], axis=0) - cr[:, :LANES])
        u["rq2"] = rq2.astype(BF16)
        u["yl"] = fold(av - cr[:, LANES:])
        u["m_state"] = (jnp.where(eye > 0.0, u["decay"], 0.0) - bg[:, :LANES]).astype(BF16)
        u["n_state"] = kv - bg[:, LANES:]


def _scan_kernel(rf, vf, kkf, lwf, kdf, bf, rb, vb, kkb, lwb, kdb, bb, tri_ref, msk_ref, lvl_ref,
                 yf_o, yb_o, hf_s, hb_s):
    @pl.when(pl.program_id(1) == 0)
    def _():
        hf_s[...] = jnp.zeros_like(hf_s)
        hb_s[...] = jnp.zeros_like(hb_s)

    c = SCAN_CHUNK
    eye = msk_ref[0]
    units, chains = [], []
    for d, (refs, y_o, h_s) in enumerate((((rf, vf, kkf, lwf, kdf, bf), yf_o, hf_s),
                                          ((rb, vb, kkb, lwb, kdb, bb), yb_o, hb_s))):
        m_strict = msk_ref[1 + 2 * d] > 0.0
        m_incl = msk_ref[2 + 2 * d] > 0.0
        r, v, kk, lw, kd, b = (ref[...] for ref in refs)
        cw = _cumsum_rows(tri_ref[d], lw)
        tot = [jnp.sum(lw[j * c:(j + 1) * c], axis=0, keepdims=True) for j in range(SCAN_STEP_CHUNKS)]
        tot_rows = jnp.concatenate([jnp.broadcast_to(t, (c, t.shape[1])) for t in tot], axis=0)
        w_inv = jnp.exp(-cw)
        w_rest = jnp.exp(tot_rows - cw)
        facs = {"kq": kk * jnp.exp(cw - lw), "rq": r * jnp.exp(cw), "bi": b * w_inv, "ki": kd * w_inv,
                "bt": b * w_rest, "kt": kd * w_rest, "v": v}
        order = range(SCAN_STEP_CHUNKS) if d == 0 else range(SCAN_STEP_CHUNKS - 1, -1, -1)
        for p in range(B_WIDTH // LANES):
            ls = slice(p * LANES, (p + 1) * LANES)
            chain = []
            for j in order:
                unit = {name: x[j * c:(j + 1) * c, ls] for name, x in facs.items()}
                unit.update(decay=jnp.exp(tot[j][:, ls]), m_strict=m_strict, m_incl=m_incl, rows=slice(j * c, (j + 1) * c))
                units.append(unit)
                chain.append(unit)
            chains.append((chain, y_o, h_s, p, ls))
    _wkv_units(units, eye, lvl_ref)
    states = [h_s[p] for _, _, h_s, p, _ in chains]
    for pos in range(SCAN_STEP_CHUNKS):
        for ci, (chain, y_o, _, _, ls) in enumerate(chains):
            u = chain[pos]
            h_h = states[ci].astype(BF16)
            y_o[u["rows"], ls] = _dot(u["rq2"], h_h) + u["yl"]
            states[ci] = _dot(u["m_state"], h_h) + u["n_state"]
    for h, (_, _, h_s, p, _) in zip(states, chains):
        h_s[p] = h


def _scan_constants():
    c, p2 = SCAN_CHUNK, PAIR_ROWS
    t = np.arange(SCAN_STEP_CHUNKS * c)
    same_chunk = (t[None, :] // c) == (t[:, None] // c)
    tri = np.stack([same_chunk & (t[None, :] <= t[:, None]), same_chunk & (t[None, :] >= t[:, None])]).astype(np.float32)
    i = np.arange(p2)
    same_head = (i[:, None] // c) == (i[None, :] // c)
    ti, si = i[:, None] % c, i[None, :] % c
    msk = np.stack([
        np.eye(p2, dtype=bool),
        same_head & (si < ti), same_head & (si <= ti),
        same_head & (si > ti), same_head & (si >= ti),
    ]).astype(np.float32)
    n_lev = int(np.log2(c))
    lvl = np.stack([
        ((i[:, None] >> (k + 1)) == (i[None, :] >> (k + 1))) & ((i[:, None] >> k) != (i[None, :] >> k))
        for k in range(n_lev)
    ]).astype(np.float32)
    return jnp.asarray(tri, BF16), jnp.asarray(msk), jnp.asarray(lvl, BF16)


def _wkv_scan(feats, dims):
    r, v, kk, lw0, lw1, kd0, kd1, b0, b1 = feats
    rows, bw = r.shape
    c = SCAN_STEP_CHUNKS * SCAN_CHUNK
    n, lc, nb = dims["n"], dims["l"], dims["b"]
    n_c, l_c = n // c, lc // c
    ctx_base = nb * n_c
    tri, msk, lvl = _scan_constants()

    def fwd(b, s):
        return (jnp.where(s < l_c, ctx_base + b * l_c + s, b * n_c + (s - l_c)), 0)

    def bwd(b, s):
        return (jnp.where(s < l_c, ctx_base + b * l_c + (l_c - 1 - s), b * n_c + (n_c - 1 - (s - l_c))), 0)

    const = lambda a: pl.BlockSpec(a.shape, lambda b, s: (0,) * a.ndim)
    out = jax.ShapeDtypeStruct((rows, bw), F32)
    return pl.pallas_call(
        _scan_kernel,
        grid=(nb, l_c + n_c),
        in_specs=[pl.BlockSpec((c, bw), fwd)] * 6 + [pl.BlockSpec((c, bw), bwd)] * 6
                 + [const(tri), const(msk), const(lvl)],
        out_specs=[pl.BlockSpec((c, bw), fwd), pl.BlockSpec((c, bw), bwd)],
        out_shape=[out, out],
        scratch_shapes=[pltpu.VMEM((bw // LANES, LANES, LANES), F32)] * 2,
        compiler_params=_cparams(("parallel", "arbitrary")),
        name="wkv_scan",
    )(r, v, kk, lw0, kd0, b0, r, v, kk, lw1, kd1, b1, tri, msk, lvl)


def _outproj_kernel(x_ref, mod_ref, wo_ref, oa_ref, *rest, even):
    o_ref = rest[-1]
    if even:
        yf, yb, r, k, v, g, rk, gnw, gnb, bd = rest[:-1]
        y = yf[...] + yb[...]
        mean = _headsum(y, bd[...]) * (1.0 / HEAD_DIM)
        yc = y - mean
        var = _headsum(yc * yc, bd[...]) * (1.0 / HEAD_DIM)
        yn = yc * lax.rsqrt(var + GN_EPS) * gnw[...] + gnb[...]
        up = lambda ref: ref[...].astype(F32)
        bonus = _headsum(up(r) * up(k) * rk[...], bd[...]) * up(v)
        ob = ((yn + bonus) * up(g)).astype(BF16)
        half = oa_ref.shape[1]
        o = _dot(oa_ref[...], wo_ref[:half, :]) + _dot(ob, wo_ref[half:, :])
    else:
        o = _dot(oa_ref[...], wo_ref[...])
    o_ref[...] = x_ref[...] + mod_ref[5:6, :] * o


def _outproj(xs, mods, w_out, l, e, o_att, dims, readout=None, latent_only=False):
    rows, d = xs.shape
    tm = dims["tm_proj"]
    tiles_per_batch = dims["n"] // tm
    nb = dims["b"]
    if latent_only:
        rows = nb * dims["n"]
    even = readout is not None
    in_specs = [
        pl.BlockSpec((tm, d), lambda i: (i, 0)),
        pl.BlockSpec((None, None, N_MOD, d), lambda i: (l, jnp.minimum(i // tiles_per_batch, nb), 0, 0)),
        pl.BlockSpec((None, d, d), lambda i: (e, 0, 0)),
        pl.BlockSpec((tm, o_att.shape[1]), lambda i: (i, 0)),
    ]
    args = [xs, mods, w_out, o_att]
    if even:
        yf, yb, r, k, v, g, rk, gnw, gnb, bd = readout
        bw = B_WIDTH
        in_specs += [pl.BlockSpec((tm, bw), lambda i: (i, 0))] * 6
        in_specs += [pl.BlockSpec((1, bw), lambda i: (0, 0))] * 3 + [pl.BlockSpec((bw, bw), lambda i: (0, 0))]
        args += [yf, yb, r, k, v, g, rk, gnw, gnb, bd]
    return pl.pallas_call(
        functools.partial(_outproj_kernel, even=even),
        grid=(rows // tm,),
        in_specs=in_specs,
        out_specs=pl.BlockSpec((tm, d), lambda i: (i, 0)),
        out_shape=jax.ShapeDtypeStruct((rows, d), F32),
        compiler_params=_cparams(("parallel",)),
        name="mixer_out",
    )(*args)


def _rope_tables(n, tm):
    rows = n // GRID_W
    row = jnp.repeat(jnp.arange(rows, dtype=F32), GRID_W)
    col = jnp.tile(jnp.arange(GRID_W, dtype=F32), rows)
    n_freq = HEAD_DIM // 4
    inv_freq = ROPE_THETA ** (-jnp.arange(n_freq, dtype=F32) / n_freq)
    ang = jnp.concatenate([row[:, None] * inv_freq, col[:, None] * inv_freq], axis=-1)
    cos = jnp.repeat(jnp.cos(ang), 2, axis=-1)
    sin = jnp.repeat(jnp.sin(ang), 2, axis=-1)
    even_lane = (jnp.arange(HEAD_DIM) % 2 == 0)[None, :]
    sin_a = jnp.where(even_lane, -sin, 0.0)
    sin_b = jnp.where(even_lane, 0.0, sin)
    pad = lambda tab, fill: jnp.concatenate(
        [jnp.tile(tab, (1, LANES // HEAD_DIM)), jnp.full((tm, LANES), fill, F32)], axis=0)
    return pad(cos, 1.0), pad(sin_a, 0.0), pad(sin_b, 0.0)


def _block_diag(width, value, dtype):
    h = np.arange(width) // HEAD_DIM
    return jnp.asarray((h[:, None] == h[None, :]) * value, dtype)


def _pad_lora(w):
    r = w.shape[1]
    z = jnp.zeros_like(w[0])
    return jnp.stack([jnp.concatenate([w[0], z], axis=0), jnp.concatenate([z, w[1]], axis=0)])


def _largest_tile(limit, *sizes):
    t = limit
    while any(s % t for s in sizes):
        t //= 2
    return t


def kernel(x, c, ctx, c_ctx, w_mod, b_mod, ffn_in, ffn_out, even_w_in, even_w_out, q_gain, k_gain, rwkv_mu, rwkv_w0,
           rwkv_w2, rwkv_a0, rwkv_a2, rwkv_g2, rwkv_k_k, rwkv_k_a, rwkv_r_k, rwkv_gn_w, rwkv_gn_b, odd_w_in, odd_w_out,
           sink, final_gain):
    nb, n, d = x.shape
    lc = ctx.shape[1]
    depth = w_mod.shape[0]
    dff = ffn_out.shape[2]
    assert n % TQ == 0 and lc % TQ == 0 and n >= TQ + 2 * WINDOW and n % GRID_W == 0
    scan_rows = SCAN_STEP_CHUNKS * SCAN_CHUNK
    assert n % scan_rows == 0 and lc % scan_rows == 0 and nb + 1 <= MOD_ROWS
    dims = {
        "b": nb, "n": n, "l": lc,
        "tm_ffn": _largest_tile(512, n, nb * lc),
        "tm_proj": _largest_tile(512, n, nb * lc),
        "tm_feat": _largest_tile(256, n, lc),
        "key_chunk": _largest_tile(512, n),
    }

    cvec = jnp.zeros((MOD_ROWS, d), F32).at[:nb].set(c).at[nb].set(c_ctx)
    mods = _mod_table(cvec, w_mod, b_mod).reshape(depth, MOD_ROWS, N_MOD, d)

    ffn_in_h = ffn_in.astype(BF16)
    ffn_out_h = ffn_out.astype(BF16)
    even_in_h = even_w_in.astype(BF16)
    even_out_h = even_w_out.astype(BF16)
    odd_in_h = odd_w_in.astype(BF16)
    odd_out_h = odd_w_out.astype(BF16)

    rope_tabs = _rope_tables(n, dims["tm_proj"])
    a_kw = A_KV_HEADS * HEAD_DIM
    a_qw = A_HEADS * HEAD_DIM
    bd_q = _block_diag(a_qw, 1.0 / HEAD_DIM, BF16)
    bd_k = _block_diag(a_kw, 1.0 / HEAD_DIM, BF16)
    bd_ones = _block_diag(B_WIDTH, 1.0, BF16)
    fw = even_w_in.shape[2] - a_qw - 2 * a_kw

    xs = x.reshape(nb * n, d)
    for l in range(depth):
        last = l == depth - 1
        xs = _ffn(xs, mods, ffn_in_h, ffn_out_h, l, 0, 0, dims, ctx_rows=ctx.reshape(nb * lc, d) if l == 0 else None)
        if l % 2 == 0:
            e = l // 2
            tile_h = lambda g, reps: jnp.tile(g, reps).reshape(1, -1)
            norm_args = (tile_h(q_gain[e], A_HEADS), tile_h(k_gain[e], A_KV_HEADS), bd_q, bd_k)
            q, k, v, f = _proj(xs, mods, even_in_h, l, e, rope_tabs, dims, a_qw, a_kw, fw, norm_args)
            oa = _attention(q, k, v, dims, ctx_queries=not last)
            feat_params = (rwkv_mu[e], rwkv_w0[e], _pad_lora(rwkv_w2[e]).astype(BF16), rwkv_a0[e],
                           _pad_lora(rwkv_a2[e]).astype(BF16), rwkv_g2[e].astype(BF16),
                           rwkv_k_k[e].reshape(1, -1), rwkv_k_a[e].reshape(1, -1), bd_ones)
            r, kr, vr, kk, g, lw0, lw1, kd0, kd1, b0, b1 = _rwkv_features(f, feat_params, dims)
            yf, yb = _wkv_scan((r, vr, kk, lw0, lw1, kd0, kd1, b0, b1), dims)
            readout = (yf, yb, r, kr, vr, g, rwkv_r_k[e].reshape(1, -1), rwkv_gn_w[e].reshape(1, -1),
                       rwkv_gn_b[e].reshape(1, -1), bd_ones)
            xs = _outproj(xs, mods, even_out_h, l, e, oa, dims, readout, latent_only=last)
        else:
            o = l // 2
            c_kw = C_KV_HEADS * HEAD_DIM
            c_qw = C_HEADS * HEAD_DIM
            q, k, v = _proj(xs, mods, odd_in_h, l, o, rope_tabs, dims, c_qw, c_kw, 0)
            oc = _attention(q, k, v, dims, sink=sink[o], ctx_queries=not last)
            xs = _outproj(xs, mods, odd_out_h, l, o, oc, dims, latent_only=last)
        xs = _ffn(xs, mods, ffn_in_h, ffn_out_h, l, 1, 6, dims, final_gain=final_gain if last else None)
    return xs.reshape(nb, n, d)
```

```python
import functools

import jax
import jax.numpy as jnp
import numpy as np
from jax import lax
from jax.experimental import pallas as pl
from jax.experimental.pallas import tpu as pltpu

F32 = jnp.float32
BF16 = jnp.bfloat16

HEAD_DIM = 64
GRID_W = 64
A_HEADS, A_KV_HEADS = 8, 2
B_HEADS = 8
B_WIDTH = B_HEADS * HEAD_DIM
DECAY_LORA, ICLR_LORA, GATE_LORA = 64, 64, 128
C_HEADS, C_KV_HEADS = 16, 4
GQA_GROUP = 4
WINDOW = 128
N_MOD = 9
ROPE_THETA = 10000.0
EPS = 1e-6
GN_EPS = 64e-5
NEG_BIG = -1e30
LOG2_E = 1.4426950408889634
EXP_NEG_HALF = 0.6065306597126334

LANES = 128
SUBLANES = 8
VMEM_LIMIT_BYTES = 56 * 1024 * 1024

SCAN_CHUNK = 64
SCAN_STEP_CHUNKS = 2
PAIR_ROWS = 2 * SCAN_CHUNK
TQ = 128
AHEAD = 4
ONES_ROWS = 16
MOD_ROWS = 16


def _cparams(sem):
    return pltpu.CompilerParams(dimension_semantics=sem, vmem_limit_bytes=VMEM_LIMIT_BYTES)


def _dot(a, b, precision=None):
    return jnp.dot(a, b, preferred_element_type=F32, precision=precision)


def _dot_nt(a, b, precision=None):
    return lax.dot_general(a, b, (((1,), (1,)), ((), ())), preferred_element_type=F32, precision=precision)


def _dot_tn(a, b, precision=None):
    return lax.dot_general(a, b, (((0,), (0,)), ((), ())), preferred_element_type=F32, precision=precision)


def _rms(x):
    return x * lax.rsqrt(jnp.mean(x * x, axis=-1, keepdims=True) + EPS)


def _sigmoid(x):
    return 1.0 / (1.0 + jnp.exp(-x))


def _headsum(x, bd, split=False):
    hi = x.astype(BF16)
    if not split:
        return _dot(hi, bd)
    lo = (x - hi.astype(F32)).astype(BF16)
    return _dot(hi, bd) + _dot(lo, bd)


def _mod_kernel(c_ref, w_ref, b_ref, o_ref):
    c = c_ref[...]
    s = c * _sigmoid(c)
    hi = s.astype(BF16)
    lo = (s - hi.astype(F32)).astype(BF16)
    w = w_ref[...].astype(BF16)
    o_ref[...] = _dot(hi, w) + _dot(lo, w) + b_ref[...]


def _mod_table(cvec, w_mod, b_mod):
    depth, d, nd = w_mod.shape
    tn = 1536 if nd % 1536 == 0 else nd
    return pl.pallas_call(
        _mod_kernel,
        grid=(depth, nd // tn),
        in_specs=[
            pl.BlockSpec((MOD_ROWS, d), lambda l, n: (0, 0)),
            pl.BlockSpec((None, d, tn), lambda l, n: (l, 0, n)),
            pl.BlockSpec((None, 1, tn), lambda l, n: (l, 0, n)),
        ],
        out_specs=pl.BlockSpec((None, MOD_ROWS, tn), lambda l, n: (l, 0, n)),
        out_shape=jax.ShapeDtypeStruct((depth, MOD_ROWS, nd), F32),
        compiler_params=_cparams(("parallel", "parallel")),
        name="mod_table",
    )(cvec, w_mod, b_mod.reshape(depth, 1, nd))


def _ffn_kernel(x_ref, mod_ref, wg_ref, wu_ref, wo_ref, *rest, i_shift, final, n_lat_tiles):
    o_ref = rest[-1]
    x = x_ref[...]
    if n_lat_tiles is not None:
        x = jnp.where(pl.program_id(0) < n_lat_tiles, x, rest[0][...])
    hn = (_rms(x) * (1.0 + mod_ref[i_shift + 1:i_shift + 2, :]) + mod_ref[i_shift:i_shift + 1, :]).astype(BF16)
    g = _dot(hn, wg_ref[...])
    u = _dot(hn, wu_ref[...])
    a = (g * _sigmoid(g) * u).astype(BF16)
    y = x + (0.5 * mod_ref[i_shift + 2:i_shift + 3, :]) * _dot(a, wo_ref[...])
    if final:
        y = _rms(y) * rest[-2][...]
    o_ref[...] = y


def _ffn(xs, mods, w_in, w_out, l, j, i_shift, dims, final_gain=None, ctx_rows=None):
    rows, d = xs.shape
    dff = w_out.shape[2]
    tm = dims["tm_ffn"]
    tiles_per_batch = dims["n"] // tm
    nb = dims["b"]
    n_lat_tiles = nb * tiles_per_batch
    final = final_gain is not None
    if final:
        rows = nb * dims["n"]
    split = ctx_rows is not None
    if split:
        rows = rows + ctx_rows.shape[0]
    in_specs = [
        pl.BlockSpec((tm, d), (lambda i: (jnp.minimum(i, n_lat_tiles - 1), 0)) if split else (lambda i: (i, 0))),
        pl.BlockSpec((None, None, N_MOD, d), lambda i: (l, jnp.minimum(i // tiles_per_batch, nb), 0, 0)),
        pl.BlockSpec((None, None, d, dff), lambda i: (l, j, 0, 0)),
        pl.BlockSpec((None, None, d, dff), lambda i: (l, j, 0, 1)),
        pl.BlockSpec((None, None, dff, d), lambda i: (l, j, 0, 0)),
    ]
    args = [xs, mods, w_in, w_in, w_out]
    if split:
        in_specs.append(pl.BlockSpec((tm, d), lambda i: (jnp.maximum(i - n_lat_tiles, 0), 0)))
        args.append(ctx_rows)
    if final:
        in_specs.append(pl.BlockSpec((1, d), lambda i: (0, 0)))
        args.append(final_gain.reshape(1, d))
    return pl.pallas_call(
        functools.partial(_ffn_kernel, i_shift=i_shift, final=final, n_lat_tiles=n_lat_tiles if split else None),
        grid=(rows // tm,),
        in_specs=in_specs,
        out_specs=pl.BlockSpec((tm, d), lambda i: (i, 0)),
        out_shape=jax.ShapeDtypeStruct((rows, d), F32),
        compiler_params=_cparams(("parallel",)),
        name="ffn",
    )(*args)


def _rope(x, cos, sin_a, sin_b):
    outs = []
    for g in range(x.shape[1] // LANES):
        xg = x[:, g * LANES:(g + 1) * LANES]
        nxt = pltpu.roll(xg, LANES - 1, 1)
        prv = pltpu.roll(xg, 1, 1)
        outs.append(xg * cos + nxt * sin_a + prv * sin_b)
    return outs[0] if len(outs) == 1 else jnp.concatenate(outs, axis=1)


def _proj_kernel(x_ref, mod_ref, w_ref, cos_ref, sa_ref, sb_ref, *rest, qw, kw, fw, qk_norm):
    if qk_norm:
        qg_ref, kg_ref, bdq_ref, bdk_ref = rest[:4]
        rest = rest[4:]
    q_ref, k_ref, v_ref = rest[:3]
    hn = (_rms(x_ref[...]) * (1.0 + mod_ref[4:5, :]) + mod_ref[3:4, :]).astype(BF16)
    q = _dot(hn, w_ref[:, :qw])
    k = _dot(hn, w_ref[:, qw:qw + kw])
    v = _dot(hn, w_ref[:, qw + kw:qw + 2 * kw])
    if qk_norm:
        q = q * lax.rsqrt(_headsum(q * q, bdq_ref[...]) + EPS) * qg_ref[...]
        k = k * lax.rsqrt(_headsum(k * k, bdk_ref[...]) + EPS) * kg_ref[...]
    cos, sa, sb = cos_ref[...], sa_ref[...], sb_ref[...]
    q_ref[...] = (_rope(q, cos, sa, sb) * (HEAD_DIM ** -0.5 * LOG2_E)).T.astype(BF16)
    k_ref[...] = _rope(k, cos, sa, sb).astype(BF16)
    v_ref[...] = v.T.astype(BF16)
    if fw:
        rest[3][...] = _dot(hn, w_ref[:, qw + 2 * kw:])


def _proj(xs, mods, w_in, l, e, rope_tabs, dims, qw, kw, fw, norm_args=None):
    rows, d = xs.shape
    tm = dims["tm_proj"]
    cols = qw + 2 * kw + fw
    tiles_per_batch = dims["n"] // tm
    n_lat_tiles = dims["b"] * tiles_per_batch
    nb = dims["b"]

    def tab_idx(i):
        return (jnp.where(i < n_lat_tiles, i % tiles_per_batch, tiles_per_batch), 0)

    in_specs = [
        pl.BlockSpec((tm, d), lambda i: (i, 0)),
        pl.BlockSpec((None, None, N_MOD, d), lambda i: (l, jnp.minimum(i // tiles_per_batch, nb), 0, 0)),
        pl.BlockSpec((None, d, cols), lambda i: (e, 0, 0)),
        pl.BlockSpec((tm, LANES), tab_idx),
        pl.BlockSpec((tm, LANES), tab_idx),
        pl.BlockSpec((tm, LANES), tab_idx),
    ]
    args = [xs, mods, w_in, *rope_tabs]
    qk_norm = norm_args is not None
    if qk_norm:
        qg, kg, bdq, bdk = norm_args
        in_specs += [
            pl.BlockSpec((1, qw), lambda i: (0, 0)),
            pl.BlockSpec((1, kw), lambda i: (0, 0)),
            pl.BlockSpec((qw, qw), lambda i: (0, 0)),
            pl.BlockSpec((kw, kw), lambda i: (0, 0)),
        ]
        args += [qg, kg, bdq, bdk]
    out_specs = [
        pl.BlockSpec((qw, tm), lambda i: (0, i)),
        pl.BlockSpec((tm, kw), lambda i: (i, 0)),
        pl.BlockSpec((kw, tm), lambda i: (0, i)),
    ]
    out_shape = [
        jax.ShapeDtypeStruct((qw, rows), BF16),
        jax.ShapeDtypeStruct((rows, kw), BF16),
        jax.ShapeDtypeStruct((kw, rows), BF16),
    ]
    if fw:
        out_specs.append(pl.BlockSpec((tm, fw), lambda i: (i, 0)))
        out_shape.append(jax.ShapeDtypeStruct((rows, fw), F32))
    return pl.pallas_call(
        functools.partial(_proj_kernel, qw=qw, kw=kw, fw=fw, qk_norm=qk_norm),
        grid=(rows // tm,),
        in_specs=in_specs,
        out_specs=out_specs,
        out_shape=out_shape,
        compiler_params=_cparams(("parallel",)),
        name="mixer_proj",
    )(*args)


def _attend(qt_ref, segs, sink_ref, n_pad, o_ref):
    tq = qt_ref.shape[1]
    cols = GQA_GROUP * tq
    heads = range(qt_ref.shape[0] // (GQA_GROUP * HEAD_DIM))
    pair = lambda h: slice((h // 2) * LANES, (h // 2 + 1) * LANES)
    qpad, m, acc = [], [], []
    for h in heads:
        q4 = jnp.concatenate([qt_ref[(h * GQA_GROUP + g) * HEAD_DIM:(h * GQA_GROUP + g + 1) * HEAD_DIM, :]
                              for g in range(GQA_GROUP)], axis=1)
        z = jnp.zeros_like(q4)
        qpad.append(jnp.concatenate([q4, z] if h % 2 == 0 else [z, q4], axis=0))
        if sink_ref is not None:
            mh, lh = sink_ref[h], jnp.ones((1, cols), F32)
        else:
            mh, lh = jnp.full((1, cols), NEG_BIG, F32), jnp.zeros((1, cols), F32)
        if n_pad is not None:
            m_new = jnp.maximum(mh, jnp.where(n_pad > 0.0, 0.0, NEG_BIG))
            lh = lh * jnp.exp2(mh - m_new) + jnp.where(n_pad > 0.0, n_pad * jnp.exp2(-m_new), 0.0)
            mh = m_new
        m.append(mh)
        acc.append(jnp.concatenate([jnp.zeros((HEAD_DIM, cols), F32), jnp.broadcast_to(lh, (ONES_ROWS, cols))], axis=0))
    def scores(item):
        (k, _, valid), h = item
        s = _dot(k[:, pair(h)], qpad[h])
        if valid is not None:
            s = jnp.where(jnp.concatenate([valid] * GQA_GROUP, axis=1), s, NEG_BIG)
        return s

    items = [(seg, h) for seg in segs for h in heads]
    pending = [scores(it) for it in items[:AHEAD]]
    for idx, ((_, vt, _), h) in enumerate(items):
        s = pending.pop(0)
        if idx + AHEAD < len(items):
            pending.append(scores(items[idx + AHEAD]))
        m_new = jnp.maximum(m[h], jnp.max(s, axis=0, keepdims=True))
        alpha = jnp.exp2(m[h] - m_new)
        p = jnp.exp2(s - m_new).astype(BF16)
        v1 =jnp.concatenate([vt[h * HEAD_DIM:(h + 1) * HEAD_DIM, :], jnp.ones((ONES_ROWS, vt.shape[1]), BF16)], axis=0)
        acc[h] = alpha * acc[h] + _dot(v1, p)
        m[h] = m_new
    blocks = []
    for h in heads:
        ot = acc[h][:HEAD_DIM, :] * (1.0 / acc[h][HEAD_DIM:HEAD_DIM + 1, :])
        for g in range(0, GQA_GROUP, 2):
            blocks.append(jnp.concatenate([ot[:, g * tq:(g + 1) * tq], ot[:, (g + 1) * tq:(g + 2) * tq]], axis=0).T)
    o_ref[...] = jnp.concatenate(blocks, axis=1).astype(o_ref.dtype)


def _attn_global_kernel(qt_ref, kl_ref, vl_ref, kc_ref, vc_ref, o_ref, *, n_q_lat, key_chunk):
    i = pl.program_id(1)
    ctx_seg = (kc_ref[...], vc_ref[...], None)

    @pl.when(i < n_q_lat)
    def _():
        segs = [(kl_ref[c * key_chunk:(c + 1) * key_chunk, :], vl_ref[:, c * key_chunk:(c + 1) * key_chunk], None)
                for c in range(kl_ref.shape[0] // key_chunk)]
        _attend(qt_ref, segs + [ctx_seg], None, None, o_ref)

    @pl.when(i >= n_q_lat)
    def _():
        _attend(qt_ref, [ctx_seg], None, None, o_ref)


def _attn_window_kernel(qt_ref, k0_ref, k1_ref, k2_ref, v0_ref, v1_ref, v2_ref, kc_ref, vc_ref, sink_ref, o_ref,
                        *, n, n_q_lat):
    i = pl.program_id(1)
    tq = qt_ref.shape[1]
    ctx_seg = (kc_ref[...], vc_ref[...], None)

    @pl.when(i < n_q_lat)
    def _():
        k = jnp.concatenate([k0_ref[...], k1_ref[...], k2_ref[...]], axis=0)
        vt = jnp.concatenate([v0_ref[...], v1_ref[...], v2_ref[...]], axis=1)
        kpos = (i - 1) * tq + lax.broadcasted_iota(jnp.int32, (3 * tq, tq), 0)
        qpos = i * tq + lax.broadcasted_iota(jnp.int32, (3 * tq, tq), 1)
        valid = (jnp.abs(kpos - qpos) <= WINDOW) & (kpos >= WINDOW) & (kpos < n)
        q_row = i * tq + lax.broadcasted_iota(jnp.int32, (1, tq), 1)
        n_pad = jnp.maximum(q_row + WINDOW - n + 1, 0).astype(F32)
        n_pad = jnp.concatenate([n_pad] * GQA_GROUP, axis=1)
        _attend(qt_ref, [(k, vt, valid), ctx_seg], sink_ref, n_pad, o_ref)

    @pl.when(i >= n_q_lat)
    def _():
        _attend(qt_ref, [ctx_seg], sink_ref, None, o_ref)


def _attention(qt, k, vt, dims, sink=None, ctx_queries=True):
    qw, rows = qt.shape
    kw = k.shape[1]
    n, lc, nb = dims["n"], dims["l"], dims["b"]
    tq = TQ if sink is not None else dims["tq_global"]
    n_q_lat = n // tq
    n_q_ctx = lc // tq
    lat_blocks = nb * n_q_lat
    ctx_block = nb * n // lc

    def q_blk(b, i):
        return jnp.where(i < n_q_lat, b * n_q_lat + i, lat_blocks + b * n_q_ctx + (i - n_q_lat))

    ctx_specs = [pl.BlockSpec((lc, kw), lambda b, i: (ctx_block + b, 0)),
                 pl.BlockSpec((kw, lc), lambda b, i: (0, ctx_block + b))]
    qt_spec = pl.BlockSpec((qw, tq), lambda b, i: (0, q_blk(b, i)))
    if sink is None:
        body = functools.partial(_attn_global_kernel, n_q_lat=n_q_lat, key_chunk=dims["key_chunk"])
        in_specs = [qt_spec,
                    pl.BlockSpec((n, kw), lambda b, i: (b, 0)),
                    pl.BlockSpec((kw, n), lambda b, i: (0, b))] + ctx_specs
        args = [qt, k, vt, k, vt]
    else:
        assert WINDOW == TQ
        body = functools.partial(_attn_window_kernel, n=n, n_q_lat=n_q_lat)
        nbr = lambda b, i, off: b * n_q_lat + jnp.clip(i + off, 0, n_q_lat - 1)
        n_kv = kw // HEAD_DIM
        in_specs = ([qt_spec]
                    + [pl.BlockSpec((TQ, kw), functools.partial(lambda b, i, off: (nbr(b, i, off), 0), off=off))
                       for off in (-1, 0, 1)]
                    + [pl.BlockSpec((kw, TQ), functools.partial(lambda b, i, off: (0, nbr(b, i, off)), off=off))
                       for off in (-1, 0, 1)]
                    + ctx_specs
                    + [pl.BlockSpec((n_kv, 1, GQA_GROUP * TQ), lambda b, i: (0, 0, 0))])
        sink_rows = jnp.repeat(sink.astype(F32).reshape(n_kv, 1, GQA_GROUP) * LOG2_E, TQ, axis=2)
        args = [qt, k, k, k, vt, vt, vt, k, vt, sink_rows]
    return pl.pallas_call(
        body,
        grid=(nb, n_q_lat + (n_q_ctx if ctx_queries else 0)),
        in_specs=in_specs,
        out_specs=pl.BlockSpec((tq, qw), lambda b, i: (q_blk(b, i), 0)),
        out_shape=jax.ShapeDtypeStruct((rows if ctx_queries else nb * n, qw), BF16),
        compiler_params=_cparams(("parallel", "arbitrary")),
        name="gqa_global" if sink is None else "gqa_window",
    )(*args)


def _feat_kernel(f_ref, fp_ref, fn_ref, mu_ref, w0_ref, w2_ref, a0_ref, a2_ref, g2_ref, kk_ref, ka_ref, bd_ref,
                 r_o, k_o, v_o, kk_o, g_o, lw0_o, lw1_o, kd0_o, kd1_o, b0_o, b1_o,
                 *, n_lat_tiles, tiles_lat, tiles_ctx):
    i = pl.program_id(0)
    t = f_ref.shape[0]
    bw = B_WIDTH
    is_lat = i < n_lat_tiles
    pos = jnp.where(is_lat, i % tiles_lat, (i - n_lat_tiles) % tiles_ctx)
    last = jnp.where(is_lat, tiles_lat, tiles_ctx) - 1
    f = f_ref[...]
    prow = jnp.where(pos == 0, 0.0, fp_ref[SUBLANES - 1:SUBLANES, :])
    nrow = jnp.where(pos == last, 0.0, fn_ref[0:1, :])
    rid = lax.broadcasted_iota(jnp.int32, (SUBLANES, 1), 0)
    prev = pltpu.roll(f, 1, 0)
    prev = jnp.concatenate([jnp.where(rid == 0, prow, prev[:SUBLANES]), prev[SUBLANES:]], axis=0)
    nxt = pltpu.roll(f, t - 1, 0)
    nxt = jnp.concatenate([nxt[:t - SUBLANES], jnp.where(rid == SUBLANES - 1, nrow, nxt[t - SUBLANES:])], axis=0)
    mu_p, mu_n = mu_ref[0:1, :], mu_ref[1:2, :]
    fs = f * (1.0 - mu_p - mu_n) + mu_p * prev + mu_n * nxt

    r = fs[:, :bw]
    k = fs[:, bw:2 * bw]
    v = fs[:, 2 * bw:3 * bw]
    wl = jnp.tanh(fs[:, 3 * bw:3 * bw + 2 * DECAY_LORA])
    al = fs[:, 3 * bw + 2 * DECAY_LORA:3 * bw + 2 * DECAY_LORA + 2 * ICLR_LORA]
    gl = fs[:, 3 * bw + 2 * DECAY_LORA + 2 * ICLR_LORA:]

    kk = k * kk_ref[...]
    kk = kk * lax.rsqrt(_headsum(kk * kk, bd_ref[...]) + EPS)
    r_o[...] = r.astype(BF16)
    k_o[...] = k.astype(BF16)
    v_o[...] = v.astype(BF16)
    kk_o[...] = kk.astype(BF16)
    g_o[...] = _dot(_sigmoid(gl).astype(BF16), g2_ref[...]).astype(BF16)
    wl = wl.astype(BF16)
    al = al.astype(BF16)
    for d, (lw_o, kd_o, b_o) in enumerate(((lw0_o, kd0_o, b0_o), (lw1_o, kd1_o, b1_o))):
        w_log = w0_ref[d:d + 1, :] + _dot(wl, w2_ref[d])
        lw_o[...] = -EXP_NEG_HALF * _sigmoid(w_log)
        a = _sigmoid(a0_ref[d:d + 1, :] + _dot(al, a2_ref[d]))
        kd_o[...] = (k * ((1.0 - ka_ref[...]) + a * ka_ref[...])).astype(BF16)
        b_o[...] = (kk * a).astype(BF16)


def _rwkv_features(f, params, dims):
    rows, fw = f.shape
    t = dims["tm_feat"]
    n, lc, nb = dims["n"], dims["l"], dims["b"]
    tiles_lat, tiles_ctx = n // t, lc // t
    n_lat_tiles = nb * tiles_lat
    hb = t // SUBLANES
    n_halo = rows // SUBLANES
    mu, w0, w2p, a0, a2p, g2, k_k, k_a, bd = params
    bw = B_WIDTH
    const = lambda shape: pl.BlockSpec(shape, lambda i: (0,) * len(shape))
    in_specs = [
        pl.BlockSpec((t, fw), lambda i: (i, 0)),
        pl.BlockSpec((SUBLANES, fw), lambda i: (jnp.maximum(i * hb - 1, 0), 0)),
        pl.BlockSpec((SUBLANES, fw), lambda i: (jnp.minimum((i + 1) * hb, n_halo - 1), 0)),
        const((2, fw)), const((2, bw)), const((2, 2 * DECAY_LORA, bw)), const((2, bw)),
        const((2, 2 * ICLR_LORA, bw)), const((GATE_LORA, bw)), const((1, bw)), const((1, bw)), const((bw, bw)),
    ]
    out = lambda dtype: jax.ShapeDtypeStruct((rows, bw), dtype)
    return pl.pallas_call(
        functools.partial(_feat_kernel, n_lat_tiles=n_lat_tiles, tiles_lat=tiles_lat, tiles_ctx=tiles_ctx),
        grid=(rows // t,),
        in_specs=in_specs,
        out_specs=[pl.BlockSpec((t, bw), lambda i: (i, 0))] * 11,
        out_shape=[out(BF16)] * 5 + [out(F32)] * 2 + [out(BF16)] * 4,
        compiler_params=_cparams(("parallel",)),
        name="rwkv_features",
    )(f, f, f, mu, w0, w2p, a0, a2p, g2, k_k, k_a, bd)


def _stack_heads(x, head0):
    return jnp.concatenate([jnp.where(head0, x, 0.0), jnp.where(head0, 0.0, x)], axis=0).astype(BF16)


def _cumsum_rows(tri, x):
    hi = x.astype(BF16)
    r1 = x - hi.astype(F32)
    mid = r1.astype(BF16)
    lo = (r1 - mid.astype(F32)).astype(BF16)
    return _dot(tri, hi) + _dot(tri, mid) + _dot(tri, lo)


def _wkv_units(units, eye, lvl_ref):
    c = units[0]["v"].shape[0]
    p2 = 2 * c
    head0 = lax.broadcasted_iota(jnp.int32, (1, LANES), 1) < HEAD_DIM
    st = lambda x: _stack_heads(x, head0)
    fold = lambda x: x[:c] + x[c:]
    each = lambda fn: [fn(u) for u in units]

    for u in units:
        u["kq_s"], u["rq_s"], u["v_s"] = st(u["kq"]), st(u["rq"]), st(u["v"])
    a = each(lambda u: _dot_nt(jnp.concatenate([u["kq_s"], u["rq_s"]], axis=0),
                               jnp.concatenate([st(u["bi"]), st(u["ki"])], axis=0)))
    for u, au in zip(units, a):
        u["nkb"] = jnp.where(u["m_strict"], au[:p2, :p2], 0.0)
        u["akk"] = jnp.where(u["m_strict"], au[:p2, p2:], 0.0).astype(BF16)
        u["arb"] = jnp.where(u["m_incl"], au[p2:, :p2], 0.0).astype(BF16)
        u["ark"] = jnp.where(u["m_incl"], au[p2:, p2:], 0.0).astype(BF16)

    nkb_h = each(lambda u: u["nkb"].astype(BF16))
    tinv = each(lambda u: eye - u["nkb"] * lvl_ref[0].astype(F32))
    for lev in range(1, lvl_ref.shape[0]):
        t_h = [t.astype(BF16) for t in tinv]
        lt = [_dot(nh * lvl_ref[lev], th).astype(BF16) for nh, th in zip(nkb_h, t_h)]
        tinv = [t - _dot(th, x) for t, th, x in zip(tinv, t_h, lt)]
    t_h = [t.astype(BF16) for t in tinv]

    akkv = each(lambda u: _dot(u["akk"], u["v_s"]).astype(BF16))
    gp_h = [_dot(th, jnp.concatenate([u["kq_s"], x], axis=1)).astype(BF16)
            for u, th, x in zip(units, t_h, akkv)]
    corr = [_dot(u["arb"], g) for u, g in zip(units, gp_h)]
    arkv = each(lambda u: _dot(u["ark"], u["v_s"]))
    btg = [_dot_tn(st(u["bt"]), g) for u, g in zip(units, gp_h)]
    ktv = each(lambda u: _dot_tn(st(u["kt"]), u["v_s"]))

    for u, cr, av, bg, kv in zip(units, corr, arkv, btg, ktv):
        rq = u["rq"]
        rq2 = fold(jnp.concatenate([jnp.where(head0, rq, 0.0), jnp.where(head0, 0.0, rq)], axis=0) - cr[:, :LANES])
        u["rq2"] = rq2.astype(BF16)
        u["yl"] = fold(av - cr[:, LANES:])
        u["m_state"] = (jnp.where(eye > 0.0, u["decay"], 0.0) - bg[:, :LANES]).astype(BF16)
        u["n_state"] = kv - bg[:, LANES:]


def _scan_kernel(rf, vf, kkf, lwf, kdf, bf, rb, vb, kkb, lwb, kdb, bb, tri_ref, msk_ref, lvl_ref,
                 yf_o, yb_o, hf_s, hb_s):
    @pl.when(pl.program_id(1) == 0)
    def _():
        hf_s[...] = jnp.zeros_like(hf_s)
        hb_s[...] = jnp.zeros_like(hb_s)

    c = SCAN_CHUNK
    eye = msk_ref[0]
    units, chains = [], []
    for d, (refs, y_o, h_s) in enumerate((((rf, vf, kkf, lwf, kdf, bf), yf_o, hf_s),
                                          ((rb, vb, kkb, lwb, kdb, bb), yb_o, hb_s))):
        m_strict = msk_ref[1 + 2 * d] > 0.0
        m_incl = msk_ref[2 + 2 * d] > 0.0
        r, v, kk, lw, kd, b = (ref[...] for ref in refs)
        cw = _cumsum_rows(tri_ref[d], lw)
        tot = [jnp.sum(lw[j * c:(j + 1) * c], axis=0, keepdims=True) for j in range(SCAN_STEP_CHUNKS)]
        tot_rows = jnp.concatenate([jnp.broadcast_to(t, (c, t.shape[1])) for t in tot], axis=0)
        w_inv = jnp.exp(-cw)
        w_rest = jnp.exp(tot_rows - cw)
        facs = {"kq": kk * jnp.exp(cw - lw), "rq": r * jnp.exp(cw), "bi": b * w_inv, "ki": kd * w_inv,
                "bt": b * w_rest, "kt": kd * w_rest, "v": v}
        order = range(SCAN_STEP_CHUNKS) if d == 0 else range(SCAN_STEP_CHUNKS - 1, -1, -1)
        for p in range(B_WIDTH // LANES):
            ls = slice(p * LANES, (p + 1) * LANES)
            chain = []
            for j in order:
                unit = {name: x[j * c:(j + 1) * c, ls] for name, x in facs.items()}
                unit.update(decay=jnp.exp(tot[j][:, ls]), m_strict=m_strict, m_incl=m_incl, rows=slice(j * c, (j + 1) * c))
                units.append(unit)
                chain.append(unit)
            chains.append((chain, y_o, h_s, p, ls))
    _wkv_units(units, eye, lvl_ref)
    states = [h_s[p] for _, _, h_s, p, _ in chains]
    for pos in range(SCAN_STEP_CHUNKS):
        for ci, (chain, y_o, _, _, ls) in enumerate(chains):
            u = chain[pos]
            h_h = states[ci].astype(BF16)
            y_o[u["rows"], ls] = _dot(u["rq2"], h_h) + u["yl"]
            states[ci] = _dot(u["m_state"], h_h) + u["n_state"]
    for h, (_, _, h_s, p, _) in zip(states, chains):
        h_s[p] = h


def _scan_constants():
    c, p2 = SCAN_CHUNK, PAIR_ROWS
    t = np.arange(SCAN_STEP_CHUNKS * c)
    same_chunk = (t[None, :] // c) == (t[:, None] // c)
    tri = np.stack([same_chunk & (t[None, :] <= t[:, None]), same_chunk & (t[None, :] >= t[:, None])]).astype(np.float32)
    i = np.arange(p2)
    same_head = (i[:, None] // c) == (i[None, :] // c)
    ti, si = i[:, None] % c, i[None, :] % c
    msk = np.stack([
        np.eye(p2, dtype=bool),
        same_head & (si < ti), same_head & (si <= ti),
        same_head & (si > ti), same_head & (si >= ti),
    ]).astype(np.float32)
    n_lev = int(np.log2(c))
    lvl = np.stack([
        ((i[:, None] >> (k + 1)) == (i[None, :] >> (k + 1))) & ((i[:, None] >> k) != (i[None, :] >> k))
        for k in range(n_lev)
    ]).astype(np.float32)
    return jnp.asarray(tri, BF16), jnp.asarray(msk), jnp.asarray(lvl, BF16)


def _wkv_scan(feats, dims):
    r, v, kk, lw0, lw1, kd0, kd1, b0, b1 = feats
    rows, bw = r.shape
    c = SCAN_STEP_CHUNKS * SCAN_CHUNK
    n, lc, nb = dims["n"], dims["l"], dims["b"]
    n_c, l_c = n // c, lc // c
    ctx_base = nb * n_c
    tri, msk, lvl = _scan_constants()

    def fwd(b, s):
        return (jnp.where(s < l_c, ctx_base + b * l_c + s, b * n_c + (s - l_c)), 0)

    def bwd(b, s):
        return (jnp.where(s < l_c, ctx_base + b * l_c + (l_c - 1 - s), b * n_c + (n_c - 1 - (s - l_c))), 0)

    const = lambda a: pl.BlockSpec(a.shape, lambda b, s: (0,) * a.ndim)
    out = jax.ShapeDtypeStruct((rows, bw), F32)
    return pl.pallas_call(
        _scan_kernel,
        grid=(nb, l_c + n_c),
        in_specs=[pl.BlockSpec((c, bw), fwd)] * 6 + [pl.BlockSpec((c, bw), bwd)] * 6
                 + [const(tri), const(msk), const(lvl)],
        out_specs=[pl.BlockSpec((c, bw), fwd), pl.BlockSpec((c, bw), bwd)],
        out_shape=[out, out],
        scratch_shapes=[pltpu.VMEM((bw // LANES, LANES, LANES), F32)] * 2,
        compiler_params=_cparams(("parallel", "arbitrary")),
        name="wkv_scan",
    )(r, v, kk, lw0, kd0, b0, r, v, kk, lw1, kd1, b1, tri, msk, lvl)


def _outproj_kernel(x_ref, mod_ref, wo_ref, oa_ref, *rest, even):
    o_ref = rest[-1]
    if even:
        yf, yb, r, k, v, g, rk, gnw, gnb, bd = rest[:-1]
        y = yf[...] + yb[...]
        mean = _headsum(y, bd[...], split=True) * (1.0 / HEAD_DIM)
        yc = y - mean
        var = _headsum(yc * yc, bd[...]) * (1.0 / HEAD_DIM)
        yn = yc * lax.rsqrt(var + GN_EPS) * gnw[...] + gnb[...]
        up = lambda ref: ref[...].astype(F32)
        bonus = _headsum(up(r) * up(k) * rk[...], bd[...]) * up(v)
        ob = ((yn + bonus) * up(g)).astype(BF16)
        half = oa_ref.shape[1]
        o = _dot(oa_ref[...], wo_ref[:half, :]) + _dot(ob, wo_ref[half:, :])
    else:
        o = _dot(oa_ref[...], wo_ref[...])
    o_ref[...] = x_ref[...] + mod_ref[5:6, :] * o


def _outproj(xs, mods, w_out, l, e, o_att, dims, readout=None, latent_only=False):
    rows, d = xs.shape
    tm = dims["tm_proj"]
    tiles_per_batch = dims["n"] // tm
    nb = dims["b"]
    if latent_only:
        rows = nb * dims["n"]
    even = readout is not None
    in_specs = [
        pl.BlockSpec((tm, d), lambda i: (i, 0)),
        pl.BlockSpec((None, None, N_MOD, d), lambda i: (l, jnp.minimum(i // tiles_per_batch, nb), 0, 0)),
        pl.BlockSpec((None, d, d), lambda i: (e, 0, 0)),
        pl.BlockSpec((tm, o_att.shape[1]), lambda i: (i, 0)),
    ]
    args = [xs, mods, w_out, o_att]
    if even:
        yf, yb, r, k, v, g, rk, gnw, gnb, bd = readout
        bw = B_WIDTH
        in_specs += [pl.BlockSpec((tm, bw), lambda i: (i, 0))] * 6
        in_specs += [pl.BlockSpec((1, bw), lambda i: (0, 0))] * 3 + [pl.BlockSpec((bw, bw), lambda i: (0, 0))]
        args += [yf, yb, r, k, v, g, rk, gnw, gnb, bd]
    return pl.pallas_call(
        functools.partial(_outproj_kernel, even=even),
        grid=(rows // tm,),
        in_specs=in_specs,
        out_specs=pl.BlockSpec((tm, d), lambda i: (i, 0)),
        out_shape=jax.ShapeDtypeStruct((rows, d), F32),
        compiler_params=_cparams(("parallel",)),
        name="mixer_out",
    )(*args)


def _rope_tables(n, tm):
    rows = n // GRID_W
    row = jnp.repeat(jnp.arange(rows, dtype=F32), GRID_W)
    col = jnp.tile(jnp.arange(GRID_W, dtype=F32), rows)
    n_freq = HEAD_DIM // 4
    inv_freq = ROPE_THETA ** (-jnp.arange(n_freq, dtype=F32) / n_freq)
    ang = jnp.concatenate([row[:, None] * inv_freq, col[:, None] * inv_freq], axis=-1)
    cos = jnp.repeat(jnp.cos(ang), 2, axis=-1)
    sin = jnp.repeat(jnp.sin(ang), 2, axis=-1)
    even_lane = (jnp.arange(HEAD_DIM) % 2 == 0)[None, :]
    sin_a = jnp.where(even_lane, -sin, 0.0)
    sin_b = jnp.where(even_lane, 0.0, sin)
    pad = lambda tab, fill: jnp.concatenate(
        [jnp.tile(tab, (1, LANES // HEAD_DIM)), jnp.full((tm, LANES), fill, F32)], axis=0)
    return pad(cos, 1.0), pad(sin_a, 0.0), pad(sin_b, 0.0)


def _block_diag(width, value, dtype):
    h = np.arange(width) // HEAD_DIM
    return jnp.asarray((h[:, None] == h[None, :]) * value, dtype)


def _pad_lora(w):
    r = w.shape[1]
    z = jnp.zeros_like(w[0])
    return jnp.stack([jnp.concatenate([w[0], z], axis=0), jnp.concatenate([z, w[1]], axis=0)])


def _largest_tile(limit, *sizes):
    t = limit
    while any(s % t for s in sizes):
        t //= 2
    return t


def kernel(x, c, ctx, c_ctx, w_mod, b_mod, ffn_in, ffn_out, even_w_in, even_w_out, q_gain, k_gain, rwkv_mu, rwkv_w0,
           rwkv_w2, rwkv_a0, rwkv_a2, rwkv_g2, rwkv_k_k, rwkv_k_a, rwkv_r_k, rwkv_gn_w, rwkv_gn_b, odd_w_in, odd_w_out,
           sink, final_gain):
    nb, n, d = x.shape
    lc = ctx.shape[1]
    depth = w_mod.shape[0]
    dff = ffn_out.shape[2]
    assert n % TQ == 0 and lc % TQ == 0 and n >= TQ + 2 * WINDOW and n % GRID_W == 0
    scan_rows = SCAN_STEP_CHUNKS * SCAN_CHUNK
    assert n % scan_rows == 0 and lc % scan_rows == 0 and nb + 1 <= MOD_ROWS
    dims = {
        "b": nb, "n": n, "l": lc,
        "tm_ffn": _largest_tile(512, n, nb * lc),
        "tm_proj": _largest_tile(512, n, nb * lc),
        "tm_feat": _largest_tile(256, n, lc),
        "key_chunk": _largest_tile(512, n),
        "tq_global": _largest_tile(256, n, lc),
    }

    cvec = jnp.zeros((MOD_ROWS, d), F32).at[:nb].set(c).at[nb].set(c_ctx)
    mods = _mod_table(cvec, w_mod, b_mod).reshape(depth, MOD_ROWS, N_MOD, d)

    ffn_in_h = ffn_in.astype(BF16)
    ffn_out_h = ffn_out.astype(BF16)
    even_in_h = even_w_in.astype(BF16)
    even_out_h = even_w_out.astype(BF16)
    odd_in_h = odd_w_in.astype(BF16)
    odd_out_h = odd_w_out.astype(BF16)

    rope_tabs = _rope_tables(n, dims["tm_proj"])
    a_kw = A_KV_HEADS * HEAD_DIM
    a_qw = A_HEADS * HEAD_DIM
    bd_q = _block_diag(a_qw, 1.0 / HEAD_DIM, BF16)
    bd_k = _block_diag(a_kw, 1.0 / HEAD_DIM, BF16)
    bd_ones = _block_diag(B_WIDTH, 1.0, BF16)
    fw = even_w_in.shape[2] - a_qw - 2 * a_kw

    xs = x.reshape(nb * n, d)
    for l in range(depth):
        last = l == depth - 1
        xs = _ffn(xs, mods, ffn_in_h, ffn_out_h, l, 0, 0, dims, ctx_rows=ctx.reshape(nb * lc, d) if l == 0 else None)
        if l % 2 == 0:
            e = l // 2
            tile_h = lambda g, reps: jnp.tile(g, reps).reshape(1, -1)
            norm_args = (tile_h(q_gain[e], A_HEADS), tile_h(k_gain[e], A_KV_HEADS), bd_q, bd_k)
            q, k, v, f = _proj(xs, mods, even_in_h, l, e, rope_tabs, dims, a_qw, a_kw, fw, norm_args)
            oa = _attention(q, k, v, dims, ctx_queries=not last)
            feat_params = (rwkv_mu[e], rwkv_w0[e], _pad_lora(rwkv_w2[e]).astype(BF16), rwkv_a0[e],
                           _pad_lora(rwkv_a2[e]).astype(BF16), rwkv_g2[e].astype(BF16),
                           rwkv_k_k[e].reshape(1, -1), rwkv_k_a[e].reshape(1, -1), bd_ones)
            r, kr, vr, kk, g, lw0, lw1, kd0, kd1, b0, b1 = _rwkv_features(f, feat_params, dims)
            yf, yb = _wkv_scan((r, vr, kk, lw0, lw1, kd0, kd1, b0, b1), dims)
            readout = (yf, yb, r, kr, vr, g, rwkv_r_k[e].reshape(1, -1), rwkv_gn_w[e].reshape(1, -1),
                       rwkv_gn_b[e].reshape(1, -1), bd_ones)
            xs = _outproj(xs, mods, even_out_h, l, e, oa, dims, readout, latent_only=last)
        else:
            o = l // 2
            c_kw = C_KV_HEADS * HEAD_DIM
            c_qw = C_HEADS * HEAD_DIM
            q, k, v = _proj(xs, mods, odd_in_h, l, o, rope_tabs, dims, c_qw, c_kw, 0)
            oc = _attention(q, k, v, dims, sink=sink[o], ctx_queries=not last)
            xs = _outproj(xs, mods, odd_out_h, l, o, oc, dims, latent_only=last)
        xs = _ffn(xs, mods, ffn_in_h, ffn_out_h, l, 1, 6, dims, final_gain=final_gain if last else None)
    return xs.reshape(nb, n, d)
```

```python
import functools

import jax
import jax.numpy as jnp
import numpy as np
from jax import lax
from jax.experimental import pallas as pl
from jax.experimental.pallas import tpu as pltpu

F32 = jnp.float32
BF16 = jnp.bfloat16

HEAD_DIM = 64
GRID_W = 64
A_HEADS, A_KV_HEADS = 8, 2
B_HEADS = 8
B_WIDTH = B_HEADS * HEAD_DIM
DECAY_LORA, ICLR_LORA, GATE_LORA = 64, 64, 128
C_HEADS, C_KV_HEADS = 16, 4
GQA_GROUP = 4
WINDOW = 128
N_MOD = 9
ROPE_THETA = 10000.0
EPS = 1e-6
GN_EPS = 64e-5
NEG_BIG = -1e30
LOG2_E = 1.4426950408889634
EXP_NEG_HALF = 0.6065306597126334

LANES = 128
SUBLANES = 8
VMEM_LIMIT_BYTES = 56 * 1024 * 1024

SCAN_CHUNK = 64
SCAN_STEP_CHUNKS = 2
PAIR_ROWS = 2 * SCAN_CHUNK
TQ = 128
AHEAD = 4
ONES_ROWS = 16
MOD_ROWS = 16


def _cparams(sem):
    return pltpu.CompilerParams(dimension_semantics=sem, vmem_limit_bytes=VMEM_LIMIT_BYTES)


def _dot(a, b, precision=None):
    return jnp.dot(a, b, preferred_element_type=F32, precision=precision)


def _dot_nt(a, b, precision=None):
    return lax.dot_general(a, b, (((1,), (1,)), ((), ())), preferred_element_type=F32, precision=precision)


def _dot_tn(a, b, precision=None):
    return lax.dot_general(a, b, (((0,), (0,)), ((), ())), preferred_element_type=F32, precision=precision)


def _rms(x):
    return x * lax.rsqrt(jnp.mean(x * x, axis=-1, keepdims=True) + EPS)


def _sigmoid(x):
    return 1.0 / (1.0 + jnp.exp(-x))


def _headsum(x, bd, split=False):
    hi = x.astype(BF16)
    if not split:
        return _dot(hi, bd)
    lo = (x - hi.astype(F32)).astype(BF16)
    return _dot(hi, bd) + _dot(lo, bd)


def _mod_kernel(c_ref, w_ref, b_ref, o_ref):
    c = c_ref[...]
    s = c * _sigmoid(c)
    hi = s.astype(BF16)
    lo = (s - hi.astype(F32)).astype(BF16)
    w = w_ref[...].astype(BF16)
    o_ref[...] = _dot(hi, w) + _dot(lo, w) + b_ref[...]


def _mod_table(cvec, w_mod, b_mod):
    depth, d, nd = w_mod.shape
    tn = 1536 if nd % 1536 == 0 else nd
    return pl.pallas_call(
        _mod_kernel,
        grid=(depth, nd // tn),
        in_specs=[
            pl.BlockSpec((MOD_ROWS, d), lambda l, n: (0, 0)),
            pl.BlockSpec((None, d, tn), lambda l, n: (l, 0, n)),
            pl.BlockSpec((None, 1, tn), lambda l, n: (l, 0, n)),
        ],
        out_specs=pl.BlockSpec((None, MOD_ROWS, tn), lambda l, n: (l, 0, n)),
        out_shape=jax.ShapeDtypeStruct((depth, MOD_ROWS, nd), F32),
        compiler_params=_cparams(("parallel", "parallel")),
        name="mod_table",
    )(cvec, w_mod, b_mod.reshape(depth, 1, nd))


def _mixer_out(oa_ref, wm_ref, readout):
    if readout is None:
        return _dot(oa_ref[...], wm_ref[...])
    yf, yb, r, k, v, g, rk, gnw, gnb, bd = readout
    y = yf[...] + yb[...]
    mean = _headsum(y, bd[...], split=True) * (1.0 / HEAD_DIM)
    yc = y - mean
    var = _headsum(yc * yc, bd[...]) * (1.0 / HEAD_DIM)
    yn = yc * lax.rsqrt(var + GN_EPS) * gnw[...] + gnb[...]
    up = lambda ref: ref[...].astype(F32)
    bonus = _headsum(up(r) * up(k) * rk[...], bd[...]) * up(v)
    ob = ((yn + bonus) * up(g)).astype(BF16)
    half = oa_ref.shape[1]
    return _dot(oa_ref[...], wm_ref[:half, :]) + _dot(ob, wm_ref[half:, :])


def _ffn_kernel(x_ref, mod_ref, wg_ref, wu_ref, wo_ref, *rest, i_shift, final, n_lat_tiles, mixer):
    o_ref = rest[-1]
    x = x_ref[...]
    if n_lat_tiles is not None:
        x = jnp.where(pl.program_id(0) < n_lat_tiles, x, rest[0][...])
        rest = rest[1:]
    if mixer is not None:
        readout = rest[2:12] if mixer == "even" else None
        x = x + mod_ref[5:6, :] * _mixer_out(rest[1], rest[0], readout)
    hn = (_rms(x) * (1.0 + mod_ref[i_shift + 1:i_shift + 2, :]) + mod_ref[i_shift:i_shift + 1, :]).astype(BF16)
    g = _dot(hn, wg_ref[...])
    u = _dot(hn, wu_ref[...])
    a = (g * _sigmoid(g) * u).astype(BF16)
    y = x + (0.5 * mod_ref[i_shift + 2:i_shift + 3, :]) * _dot(a, wo_ref[...])
    if final:
        y = _rms(y) * rest[-2][...]
    o_ref[...] = y


def _ffn(xs, mods, w_in, w_out, l, j, i_shift, dims, final_gain=None, ctx_rows=None, mixer=None):
    rows, d = xs.shape
    dff = w_out.shape[2]
    tm = dims["tm_ffn"]
    tiles_per_batch = dims["n"] // tm
    nb = dims["b"]
    n_lat_tiles = nb * tiles_per_batch
    final = final_gain is not None
    if final:
        rows = nb * dims["n"]
    split = ctx_rows is not None
    if split:
        rows = rows + ctx_rows.shape[0]
    in_specs = [
        pl.BlockSpec((tm, d), (lambda i: (jnp.minimum(i, n_lat_tiles - 1), 0)) if split else (lambda i: (i, 0))),
        pl.BlockSpec((None, None, N_MOD, d), lambda i: (l, jnp.minimum(i // tiles_per_batch, nb), 0, 0)),
        pl.BlockSpec((None, None, d, dff), lambda i: (l, j, 0, 0)),
        pl.BlockSpec((None, None, d, dff), lambda i: (l, j, 0, 1)),
        pl.BlockSpec((None, None, dff, d), lambda i: (l, j, 0, 0)),
    ]
    args = [xs, mods, w_in, w_in, w_out]
    if split:
        in_specs.append(pl.BlockSpec((tm, d), lambda i: (jnp.maximum(i - n_lat_tiles, 0), 0)))
        args.append(ctx_rows)
    mixer_kind = None
    if mixer is not None:
        w_mix, e, o_att, readout = mixer
        mixer_kind = "odd" if readout is None else "even"
        in_specs += [pl.BlockSpec((None, d, d), lambda i: (e, 0, 0)),
                     pl.BlockSpec((tm, o_att.shape[1]), lambda i: (i, 0))]
        args += [w_mix, o_att]
        if readout is not None:
            bw = B_WIDTH
            in_specs += [pl.BlockSpec((tm, bw), lambda i: (i, 0))] * 6
            in_specs += [pl.BlockSpec((1, bw), lambda i: (0, 0))] * 3 + [pl.BlockSpec((bw, bw), lambda i: (0, 0))]
            args += list(readout)
    if final:
        in_specs.append(pl.BlockSpec((1, d), lambda i: (0, 0)))
        args.append(final_gain.reshape(1, d))
    return pl.pallas_call(
        functools.partial(_ffn_kernel, i_shift=i_shift, final=final, n_lat_tiles=n_lat_tiles if split else None,
                          mixer=mixer_kind),
        grid=(rows // tm,),
        in_specs=in_specs,
        out_specs=pl.BlockSpec((tm, d), lambda i: (i, 0)),
        out_shape=jax.ShapeDtypeStruct((rows, d), F32),
        compiler_params=_cparams(("parallel",)),
        name="ffn",
    )(*args)


def _rope(x, cos, sin_a, sin_b):
    outs = []
    for g in range(x.shape[1] // LANES):
        xg = x[:, g * LANES:(g + 1) * LANES]
        nxt = pltpu.roll(xg, LANES - 1, 1)
        prv = pltpu.roll(xg, 1, 1)
        outs.append(xg * cos + nxt * sin_a + prv * sin_b)
    return outs[0] if len(outs) == 1 else jnp.concatenate(outs, axis=1)


def _proj_kernel(x_ref, mod_ref, w_ref, cos_ref, sa_ref, sb_ref, *rest, qw, kw, fw, qk_norm):
    if qk_norm:
        qg_ref, kg_ref, bdq_ref, bdk_ref = rest[:4]
        rest = rest[4:]
    q_ref, k_ref, v_ref = rest[:3]
    hn = (_rms(x_ref[...]) * (1.0 + mod_ref[4:5, :]) + mod_ref[3:4, :]).astype(BF16)
    q = _dot(hn, w_ref[:, :qw])
    k = _dot(hn, w_ref[:, qw:qw + kw])
    v = _dot(hn, w_ref[:, qw + kw:qw + 2 * kw])
    if qk_norm:
        q = q * lax.rsqrt(_headsum(q * q, bdq_ref[...]) + EPS) * qg_ref[...]
        k = k * lax.rsqrt(_headsum(k * k, bdk_ref[...]) + EPS) * kg_ref[...]
    cos, sa, sb = cos_ref[...], sa_ref[...], sb_ref[...]
    q_ref[...] = (_rope(q, cos, sa, sb) * (HEAD_DIM ** -0.5 * LOG2_E)).T.astype(BF16)
    k_ref[...] = _rope(k, cos, sa, sb).astype(BF16)
    v_ref[...] = v.T.astype(BF16)
    if fw:
        rest[3][...] = _dot(hn, w_ref[:, qw + 2 * kw:])


def _proj(xs, mods, w_in, l, e, rope_tabs, dims, qw, kw, fw, norm_args=None):
    rows, d = xs.shape
    tm = dims["tm_proj"]
    cols = qw + 2 * kw + fw
    tiles_per_batch = dims["n"] // tm
    n_lat_tiles = dims["b"] * tiles_per_batch
    nb = dims["b"]

    def tab_idx(i):
        return (jnp.where(i < n_lat_tiles, i % tiles_per_batch, tiles_per_batch), 0)

    in_specs = [
        pl.BlockSpec((tm, d), lambda i: (i, 0)),
        pl.BlockSpec((None, None, N_MOD, d), lambda i: (l, jnp.minimum(i // tiles_per_batch, nb), 0, 0)),
        pl.BlockSpec((None, d, cols), lambda i: (e, 0, 0)),
        pl.BlockSpec((tm, LANES), tab_idx),
        pl.BlockSpec((tm, LANES), tab_idx),
        pl.BlockSpec((tm, LANES), tab_idx),
    ]
    args = [xs, mods, w_in, *rope_tabs]
    qk_norm = norm_args is not None
    if qk_norm:
        qg, kg, bdq, bdk = norm_args
        in_specs += [
            pl.BlockSpec((1, qw), lambda i: (0, 0)),
            pl.BlockSpec((1, kw), lambda i: (0, 0)),
            pl.BlockSpec((qw, qw), lambda i: (0, 0)),
            pl.BlockSpec((kw, kw), lambda i: (0, 0)),
        ]
        args += [qg, kg, bdq, bdk]
    out_specs = [
        pl.BlockSpec((qw, tm), lambda i: (0, i)),
        pl.BlockSpec((tm, kw), lambda i: (i, 0)),
        pl.BlockSpec((kw, tm), lambda i: (0, i)),
    ]
    out_shape = [
        jax.ShapeDtypeStruct((qw, rows), BF16),
        jax.ShapeDtypeStruct((rows, kw), BF16),
        jax.ShapeDtypeStruct((kw, rows), BF16),
    ]
    if fw:
        out_specs.append(pl.BlockSpec((tm, fw), lambda i: (i, 0)))
        out_shape.append(jax.ShapeDtypeStruct((rows, fw), F32))
    return pl.pallas_call(
        functools.partial(_proj_kernel, qw=qw, kw=kw, fw=fw, qk_norm=qk_norm),
        grid=(rows // tm,),
        in_specs=in_specs,
        out_specs=out_specs,
        out_shape=out_shape,
        compiler_params=_cparams(("parallel",)),
        name="mixer_proj",
    )(*args)


def _attend(qt_ref, segs, sink_ref, n_pad, o_ref):
    tq = qt_ref.shape[1]
    cols = GQA_GROUP * tq
    heads = range(qt_ref.shape[0] // (GQA_GROUP * HEAD_DIM))
    pair = lambda h: slice((h // 2) * LANES, (h // 2 + 1) * LANES)
    qpad, m, acc = [], [], []
    for h in heads:
        q4 = jnp.concatenate([qt_ref[(h * GQA_GROUP + g) * HEAD_DIM:(h * GQA_GROUP + g + 1) * HEAD_DIM, :]
                              for g in range(GQA_GROUP)], axis=1)
        z = jnp.zeros_like(q4)
        qpad.append(jnp.concatenate([q4, z] if h % 2 == 0 else [z, q4], axis=0))
        if sink_ref is not None:
            mh, lh = sink_ref[h], jnp.ones((1, cols), F32)
        else:
            mh, lh = jnp.full((1, cols), NEG_BIG, F32), jnp.zeros((1, cols), F32)
        if n_pad is not None:
            m_new = jnp.maximum(mh, jnp.where(n_pad > 0.0, 0.0, NEG_BIG))
            lh = lh * jnp.exp2(mh - m_new) + jnp.where(n_pad > 0.0, n_pad * jnp.exp2(-m_new), 0.0)
            mh = m_new
        m.append(mh)
        acc.append(jnp.concatenate([jnp.zeros((HEAD_DIM, cols), F32), jnp.broadcast_to(lh, (ONES_ROWS, cols))], axis=0))
    def scores(item):
        (k, _, valid), h = item
        s = _dot(k[:, pair(h)], qpad[h])
        if valid is not None:
            s = jnp.where(jnp.concatenate([valid] * GQA_GROUP, axis=1), s, NEG_BIG)
        return s

    items = [(seg, h) for seg in segs for h in heads]
    pending = [scores(it) for it in items[:AHEAD]]
    for idx, ((_, vt, _), h) in enumerate(items):
        s = pending.pop(0)
        if idx + AHEAD < len(items):
            pending.append(scores(items[idx + AHEAD]))
        m_new = jnp.maximum(m[h], jnp.max(s, axis=0, keepdims=True))
        alpha = jnp.exp2(m[h] - m_new)
        p = jnp.exp2(s - m_new).astype(BF16)
        v1 =jnp.concatenate([vt[h * HEAD_DIM:(h + 1) * HEAD_DIM, :], jnp.ones((ONES_ROWS, vt.shape[1]), BF16)], axis=0)
        acc[h] = alpha * acc[h] + _dot(v1, p)
        m[h] = m_new
    blocks = []
    for h in heads:
        ot = acc[h][:HEAD_DIM, :] * (1.0 / acc[h][HEAD_DIM:HEAD_DIM + 1, :])
        for g in range(0, GQA_GROUP, 2):
            blocks.append(jnp.concatenate([ot[:, g * tq:(g + 1) * tq], ot[:, (g + 1) * tq:(g + 2) * tq]], axis=0).T)
    o_ref[...] = jnp.concatenate(blocks, axis=1).astype(o_ref.dtype)


def _attn_global_kernel(qt_ref, kl_ref, vl_ref, kc_ref, vc_ref, o_ref, *, n_q_lat, key_chunk):
    i = pl.program_id(1)
    ctx_seg = (kc_ref[...], vc_ref[...], None)

    @pl.when(i < n_q_lat)
    def _():
        segs = [(kl_ref[c * key_chunk:(c + 1) * key_chunk, :], vl_ref[:, c * key_chunk:(c + 1) * key_chunk], None)
                for c in range(kl_ref.shape[0] // key_chunk)]
        _attend(qt_ref, segs + [ctx_seg], None, None, o_ref)

    @pl.when(i >= n_q_lat)
    def _():
        _attend(qt_ref, [ctx_seg], None, None, o_ref)


def _attn_window_kernel(qt_ref, k0_ref, k1_ref, k2_ref, v0_ref, v1_ref, v2_ref, kc_ref, vc_ref, sink_ref, o_ref,
                        *, n, n_q_lat):
    i = pl.program_id(1)
    tq = qt_ref.shape[1]
    ctx_seg = (kc_ref[...], vc_ref[...], None)

    @pl.when(i < n_q_lat)
    def _():
        k = jnp.concatenate([k0_ref[...], k1_ref[...], k2_ref[...]], axis=0)
        vt = jnp.concatenate([v0_ref[...], v1_ref[...], v2_ref[...]], axis=1)
        kpos = (i - 1) * tq + lax.broadcasted_iota(jnp.int32, (3 * tq, tq), 0)
        qpos = i * tq + lax.broadcasted_iota(jnp.int32, (3 * tq, tq), 1)
        valid = (jnp.abs(kpos - qpos) <= WINDOW) & (kpos >= WINDOW) & (kpos < n)
        q_row = i * tq + lax.broadcasted_iota(jnp.int32, (1, tq), 1)
        n_pad = jnp.maximum(q_row + WINDOW - n + 1, 0).astype(F32)
        n_pad = jnp.concatenate([n_pad] * GQA_GROUP, axis=1)
        _attend(qt_ref, [(k, vt, valid), ctx_seg], sink_ref, n_pad, o_ref)

    @pl.when(i >= n_q_lat)
    def _():
        _attend(qt_ref, [ctx_seg], sink_ref, None, o_ref)


def _attention(qt, k, vt, dims, sink=None, ctx_queries=True):
    qw, rows = qt.shape
    kw = k.shape[1]
    n, lc, nb = dims["n"], dims["l"], dims["b"]
    tq = TQ if sink is not None else dims["tq_global"]
    n_q_lat = n // tq
    n_q_ctx = lc // tq
    lat_blocks = nb * n_q_lat
    ctx_block = nb * n // lc

    def q_blk(b, i):
        return jnp.where(i < n_q_lat, b * n_q_lat + i, lat_blocks + b * n_q_ctx + (i - n_q_lat))

    ctx_specs = [pl.BlockSpec((lc, kw), lambda b, i: (ctx_block + b, 0)),
                 pl.BlockSpec((kw, lc), lambda b, i: (0, ctx_block + b))]
    qt_spec = pl.BlockSpec((qw, tq), lambda b, i: (0, q_blk(b, i)))
    if sink is None:
        body = functools.partial(_attn_global_kernel, n_q_lat=n_q_lat, key_chunk=dims["key_chunk"])
        in_specs = [qt_spec,
                    pl.BlockSpec((n, kw), lambda b, i: (b, 0)),
                    pl.BlockSpec((kw, n), lambda b, i: (0, b))] + ctx_specs
        args = [qt, k, vt, k, vt]
    else:
        assert WINDOW == TQ
        body = functools.partial(_attn_window_kernel, n=n, n_q_lat=n_q_lat)
        nbr = lambda b, i, off: b * n_q_lat + jnp.clip(i + off, 0, n_q_lat - 1)
        n_kv = kw // HEAD_DIM
        in_specs = ([qt_spec]
                    + [pl.BlockSpec((TQ, kw), functools.partial(lambda b, i, off: (nbr(b, i, off), 0), off=off))
                       for off in (-1, 0, 1)]
                    + [pl.BlockSpec((kw, TQ), functools.partial(lambda b, i, off: (0, nbr(b, i, off)), off=off))
                       for off in (-1, 0, 1)]
                    + ctx_specs
                    + [pl.BlockSpec((n_kv, 1, GQA_GROUP * TQ), lambda b, i: (0, 0, 0))])
        sink_rows = jnp.repeat(sink.astype(F32).reshape(n_kv, 1, GQA_GROUP) * LOG2_E, TQ, axis=2)
        args = [qt, k, k, k, vt, vt, vt, k, vt, sink_rows]
    return pl.pallas_call(
        body,
        grid=(nb, n_q_lat + (n_q_ctx if ctx_queries else 0)),
        in_specs=in_specs,
        out_specs=pl.BlockSpec((tq, qw), lambda b, i: (q_blk(b, i), 0)),
        out_shape=jax.ShapeDtypeStruct((rows if ctx_queries else nb * n, qw), BF16),
        compiler_params=_cparams(("parallel", "arbitrary")),
        name="gqa_global" if sink is None else "gqa_window",
    )(*args)


def _feat_kernel(f_ref, fp_ref, fn_ref, mu_ref, w0_ref, w2_ref, a0_ref, a2_ref, g2_ref, kk_ref, ka_ref, bd_ref,
                 r_o, k_o, v_o, kk_o, g_o, lw0_o, lw1_o, kd0_o, kd1_o, b0_o, b1_o,
                 *, n_lat_tiles, tiles_lat, tiles_ctx):
    i = pl.program_id(0)
    t = f_ref.shape[0]
    bw = B_WIDTH
    is_lat = i < n_lat_tiles
    pos = jnp.where(is_lat, i % tiles_lat, (i - n_lat_tiles) % tiles_ctx)
    last = jnp.where(is_lat, tiles_lat, tiles_ctx) - 1
    f = f_ref[...]
    prow = jnp.where(pos == 0, 0.0, fp_ref[SUBLANES - 1:SUBLANES, :])
    nrow = jnp.where(pos == last, 0.0, fn_ref[0:1, :])
    rid = lax.broadcasted_iota(jnp.int32, (SUBLANES, 1), 0)
    prev = pltpu.roll(f, 1, 0)
    prev = jnp.concatenate([jnp.where(rid == 0, prow, prev[:SUBLANES]), prev[SUBLANES:]], axis=0)
    nxt = pltpu.roll(f, t - 1, 0)
    nxt = jnp.concatenate([nxt[:t - SUBLANES], jnp.where(rid == SUBLANES - 1, nrow, nxt[t - SUBLANES:])], axis=0)
    mu_p, mu_n = mu_ref[0:1, :], mu_ref[1:2, :]
    fs = f * (1.0 - mu_p - mu_n) + mu_p * prev + mu_n * nxt

    r = fs[:, :bw]
    k = fs[:, bw:2 * bw]
    v = fs[:, 2 * bw:3 * bw]
    wl = jnp.tanh(fs[:, 3 * bw:3 * bw + 2 * DECAY_LORA])
    al = fs[:, 3 * bw + 2 * DECAY_LORA:3 * bw + 2 * DECAY_LORA + 2 * ICLR_LORA]
    gl = fs[:, 3 * bw + 2 * DECAY_LORA + 2 * ICLR_LORA:]

    kk = k * kk_ref[...]
    kk = kk * lax.rsqrt(_headsum(kk * kk, bd_ref[...]) + EPS)
    r_o[...] = r.astype(BF16)
    k_o[...] = k.astype(BF16)
    v_o[...] = v.astype(BF16)
    kk_o[...] = kk.astype(BF16)
    g_o[...] = _dot(_sigmoid(gl).astype(BF16), g2_ref[...]).astype(BF16)
    wl = wl.astype(BF16)
    al = al.astype(BF16)
    for d, (lw_o, kd_o, b_o) in enumerate(((lw0_o, kd0_o, b0_o), (lw1_o, kd1_o, b1_o))):
        w_log = w0_ref[d:d + 1, :] + _dot(wl, w2_ref[d])
        lw_o[...] = -EXP_NEG_HALF * _sigmoid(w_log)
        a = _sigmoid(a0_ref[d:d + 1, :] + _dot(al, a2_ref[d]))
        kd_o[...] = (k * ((1.0 - ka_ref[...]) + a * ka_ref[...])).astype(BF16)
        b_o[...] = (kk * a).astype(BF16)


def _rwkv_features(f, params, dims):
    rows, fw = f.shape
    t = dims["tm_feat"]
    n, lc, nb = dims["n"], dims["l"], dims["b"]
    tiles_lat, tiles_ctx = n // t, lc // t
    n_lat_tiles = nb * tiles_lat
    hb = t // SUBLANES
    n_halo = rows // SUBLANES
    mu, w0, w2p, a0, a2p, g2, k_k, k_a, bd = params
    bw = B_WIDTH
    const = lambda shape: pl.BlockSpec(shape, lambda i: (0,) * len(shape))
    in_specs = [
        pl.BlockSpec((t, fw), lambda i: (i, 0)),
        pl.BlockSpec((SUBLANES, fw), lambda i: (jnp.maximum(i * hb - 1, 0), 0)),
        pl.BlockSpec((SUBLANES, fw), lambda i: (jnp.minimum((i + 1) * hb, n_halo - 1), 0)),
        const((2, fw)), const((2, bw)), const((2, 2 * DECAY_LORA, bw)), const((2, bw)),
        const((2, 2 * ICLR_LORA, bw)), const((GATE_LORA, bw)), const((1, bw)), const((1, bw)), const((bw, bw)),
    ]
    out = lambda dtype: jax.ShapeDtypeStruct((rows, bw), dtype)
    return pl.pallas_call(
        functools.partial(_feat_kernel, n_lat_tiles=n_lat_tiles, tiles_lat=tiles_lat, tiles_ctx=tiles_ctx),
        grid=(rows // t,),
        in_specs=in_specs,
        out_specs=[pl.BlockSpec((t, bw), lambda i: (i, 0))] * 11,
        out_shape=[out(BF16)] * 5 + [out(F32)] * 2 + [out(BF16)] * 4,
        compiler_params=_cparams(("parallel",)),
        name="rwkv_features",
    )(f, f, f, mu, w0, w2p, a0, a2p, g2, k_k, k_a, bd)


def _stack_heads(x, head0):
    return jnp.concatenate([jnp.where(head0, x, 0.0), jnp.where(head0, 0.0, x)], axis=0).astype(BF16)


def _cumsum_rows(tri, x):
    hi = x.astype(BF16)
    r1 = x - hi.astype(F32)
    mid = r1.astype(BF16)
    lo = (r1 - mid.astype(F32)).astype(BF16)
    return _dot(tri, hi) + _dot(tri, mid) + _dot(tri, lo)


def _wkv_units(units, eye, lvl_ref):
    c = units[0]["v"].shape[0]
    p2 = 2 * c
    head0 = lax.broadcasted_iota(jnp.int32, (1, LANES), 1) < HEAD_DIM
    st = lambda x: _stack_heads(x, head0)
    fold = lambda x: x[:c] + x[c:]
    each = lambda fn: [fn(u) for u in units]

    for u in units:
        u["kq_s"], u["rq_s"], u["v_s"] = st(u["kq"]), st(u["rq"]), st(u["v"])
    a = each(lambda u: _dot_nt(jnp.concatenate([u["kq_s"], u["rq_s"]], axis=0),
                               jnp.concatenate([st(u["bi"]), st(u["ki"])], axis=0)))
    for u, au in zip(units, a):
        u["nkb"] = jnp.where(u["m_strict"], au[:p2, :p2], 0.0)
        u["akk"] = jnp.where(u["m_strict"], au[:p2, p2:], 0.0).astype(BF16)
        u["arb"] = jnp.where(u["m_incl"], au[p2:, :p2], 0.0).astype(BF16)
        u["ark"] = jnp.where(u["m_incl"], au[p2:, p2:], 0.0).astype(BF16)

    nkb_h = each(lambda u: u["nkb"].astype(BF16))
    tinv = each(lambda u: eye - u["nkb"] * lvl_ref[0].astype(F32))
    for lev in range(1, lvl_ref.shape[0]):
        t_h = [t.astype(BF16) for t in tinv]
        lt = [_dot(nh * lvl_ref[lev], th).astype(BF16) for nh, th in zip(nkb_h, t_h)]
        tinv = [t - _dot(th, x) for t, th, x in zip(tinv, t_h, lt)]
    t_h = [t.astype(BF16) for t in tinv]

    akkv = each(lambda u: _dot(u["akk"], u["v_s"]).astype(BF16))
    gp_h = [_dot(th, jnp.concatenate([u["kq_s"], x], axis=1)).astype(BF16)
            for u, th, x in zip(units, t_h, akkv)]
    corr = [_dot(u["arb"], g) for u, g in zip(units, gp_h)]
    arkv = each(lambda u: _dot(u["ark"], u["v_s"]))
    btg = [_dot_tn(st(u["bt"]), g) for u, g in zip(units, gp_h)]
    ktv = each(lambda u: _dot_tn(st(u["kt"]), u["v_s"]))

    for u, cr, av, bg, kv in zip(units, corr, arkv, btg, ktv):
        rq = u["rq"]
        rq2 = fold(jnp.concatenate([jnp.where(head0, rq, 0.0), jnp.where(head0, 0.0, rq)], axis=0) - cr[:, :LANES])
        u["rq2"] = rq2.astype(BF16)
        u["yl"] = fold(av - cr[:, LANES:])
        u["m_state"] = (jnp.where(eye > 0.0, u["decay"], 0.0) - bg[:, :LANES]).astype(BF16)
        u["n_state"] = kv - bg[:, LANES:]


def _scan_kernel(rf, vf, kkf, lwf, kdf, bf, rb, vb, kkb, lwb, kdb, bb, tri_ref, msk_ref, lvl_ref,
                 yf_o, yb_o, hf_s, hb_s):
    @pl.when(pl.program_id(1) == 0)
    def _():
        hf_s[...] = jnp.zeros_like(hf_s)
        hb_s[...] = jnp.zeros_like(hb_s)

    c = SCAN_CHUNK
    eye = msk_ref[0]
    units, chains = [], []
    for d, (refs, y_o, h_s) in enumerate((((rf, vf, kkf, lwf, kdf, bf), yf_o, hf_s),
                                          ((rb, vb, kkb, lwb, kdb, bb), yb_o, hb_s))):
        m_strict = msk_ref[1 + 2 * d] > 0.0
        m_incl = msk_ref[2 + 2 * d] > 0.0
        r, v, kk, lw, kd, b = (ref[...] for ref in refs)
        cw = _cumsum_rows(tri_ref[d], lw)
        tot = [jnp.sum(lw[j * c:(j + 1) * c], axis=0, keepdims=True) for j in range(SCAN_STEP_CHUNKS)]
        tot_rows = jnp.concatenate([jnp.broadcast_to(t, (c, t.shape[1])) for t in tot], axis=0)
        w_inv = jnp.exp(-cw)
        w_rest = jnp.exp(tot_rows - cw)
        facs = {"kq": kk * jnp.exp(cw - lw), "rq": r * jnp.exp(cw), "bi": b * w_inv, "ki": kd * w_inv,
                "bt": b * w_rest, "kt": kd * w_rest, "v": v}
        order = range(SCAN_STEP_CHUNKS) if d == 0 else range(SCAN_STEP_CHUNKS - 1, -1, -1)
        for p in range(B_WIDTH // LANES):
            ls = slice(p * LANES, (p + 1) * LANES)
            chain = []
            for j in order:
                unit = {name: x[j * c:(j + 1) * c, ls] for name, x in facs.items()}
                unit.update(decay=jnp.exp(tot[j][:, ls]), m_strict=m_strict, m_incl=m_incl, rows=slice(j * c, (j + 1) * c))
                units.append(unit)
                chain.append(unit)
            chains.append((chain, y_o, h_s, p, ls))
    _wkv_units(units, eye, lvl_ref)
    states = [h_s[p] for _, _, h_s, p, _ in chains]
    for pos in range(SCAN_STEP_CHUNKS):
        for ci, (chain, y_o, _, _, ls) in enumerate(chains):
            u = chain[pos]
            h_h = states[ci].astype(BF16)
            y_o[u["rows"], ls] = _dot(u["rq2"], h_h) + u["yl"]
            states[ci] = _dot(u["m_state"], h_h) + u["n_state"]
    for h, (_, _, h_s, p, _) in zip(states, chains):
        h_s[p] = h


def _scan_constants():
    c, p2 = SCAN_CHUNK, PAIR_ROWS
    t = np.arange(SCAN_STEP_CHUNKS * c)
    same_chunk = (t[None, :] // c) == (t[:, None] // c)
    tri = np.stack([same_chunk & (t[None, :] <= t[:, None]), same_chunk & (t[None, :] >= t[:, None])]).astype(np.float32)
    i = np.arange(p2)
    same_head = (i[:, None] // c) == (i[None, :] // c)
    ti, si = i[:, None] % c, i[None, :] % c
    msk = np.stack([
        np.eye(p2, dtype=bool),
        same_head & (si < ti), same_head & (si <= ti),
        same_head & (si > ti), same_head & (si >= ti),
    ]).astype(np.float32)
    n_lev = int(np.log2(c))
    lvl = np.stack([
        ((i[:, None] >> (k + 1)) == (i[None, :] >> (k + 1))) & ((i[:, None] >> k) != (i[None, :] >> k))
        for k in range(n_lev)
    ]).astype(np.float32)
    return jnp.asarray(tri, BF16), jnp.asarray(msk), jnp.asarray(lvl, BF16)


def _wkv_scan(feats, dims):
    r, v, kk, lw0, lw1, kd0, kd1, b0, b1 = feats
    rows, bw = r.shape
    c = SCAN_STEP_CHUNKS * SCAN_CHUNK
    n, lc, nb = dims["n"], dims["l"], dims["b"]
    n_c, l_c = n // c, lc // c
    ctx_base = nb * n_c
    tri, msk, lvl = _scan_constants()

    def fwd(b, s):
        return (jnp.where(s < l_c, ctx_base + b * l_c + s, b * n_c + (s - l_c)), 0)

    def bwd(b, s):
        return (jnp.where(s < l_c, ctx_base + b * l_c + (l_c - 1 - s), b * n_c + (n_c - 1 - (s - l_c))), 0)

    const = lambda a: pl.BlockSpec(a.shape, lambda b, s: (0,) * a.ndim)
    out = jax.ShapeDtypeStruct((rows, bw), F32)
    return pl.pallas_call(
        _scan_kernel,
        grid=(nb, l_c + n_c),
        in_specs=[pl.BlockSpec((c, bw), fwd)] * 6 + [pl.BlockSpec((c, bw), bwd)] * 6
                 + [const(tri), const(msk), const(lvl)],
        out_specs=[pl.BlockSpec((c, bw), fwd), pl.BlockSpec((c, bw), bwd)],
        out_shape=[out, out],
        scratch_shapes=[pltpu.VMEM((bw // LANES, LANES, LANES), F32)] * 2,
        compiler_params=_cparams(("parallel", "arbitrary")),
        name="wkv_scan",
    )(r, v, kk, lw0, kd0, b0, r, v, kk, lw1, kd1, b1, tri, msk, lvl)


def _rope_tables(n, tm):
    rows = n // GRID_W
    row = jnp.repeat(jnp.arange(rows, dtype=F32), GRID_W)
    col = jnp.tile(jnp.arange(GRID_W, dtype=F32), rows)
    n_freq = HEAD_DIM // 4
    inv_freq = ROPE_THETA ** (-jnp.arange(n_freq, dtype=F32) / n_freq)
    ang = jnp.concatenate([row[:, None] * inv_freq, col[:, None] * inv_freq], axis=-1)
    cos = jnp.repeat(jnp.cos(ang), 2, axis=-1)
    sin = jnp.repeat(jnp.sin(ang), 2, axis=-1)
    even_lane = (jnp.arange(HEAD_DIM) % 2 == 0)[None, :]
    sin_a = jnp.where(even_lane, -sin, 0.0)
    sin_b = jnp.where(even_lane, 0.0, sin)
    pad = lambda tab, fill: jnp.concatenate(
        [jnp.tile(tab, (1, LANES // HEAD_DIM)), jnp.full((tm, LANES), fill, F32)], axis=0)
    return pad(cos, 1.0), pad(sin_a, 0.0), pad(sin_b, 0.0)


def _block_diag(width, value, dtype):
    h = np.arange(width) // HEAD_DIM
    return jnp.asarray((h[:, None] == h[None, :]) * value, dtype)


def _pad_lora(w):
    r = w.shape[1]
    z = jnp.zeros_like(w[0])
    return jnp.stack([jnp.concatenate([w[0], z], axis=0), jnp.concatenate([z, w[1]], axis=0)])


def _largest_tile(limit, *sizes):
    t = limit
    while any(s % t for s in sizes):
        t //= 2
    return t


def kernel(x, c, ctx, c_ctx, w_mod, b_mod, ffn_in, ffn_out, even_w_in, even_w_out, q_gain, k_gain, rwkv_mu, rwkv_w0,
           rwkv_w2, rwkv_a0, rwkv_a2, rwkv_g2, rwkv_k_k, rwkv_k_a, rwkv_r_k, rwkv_gn_w, rwkv_gn_b, odd_w_in, odd_w_out,
           sink, final_gain):
    nb, n, d = x.shape
    lc = ctx.shape[1]
    depth = w_mod.shape[0]
    dff = ffn_out.shape[2]
    assert n % TQ == 0 and lc % TQ == 0 and n >= TQ + 2 * WINDOW and n % GRID_W == 0
    scan_rows = SCAN_STEP_CHUNKS * SCAN_CHUNK
    assert n % scan_rows == 0 and lc % scan_rows == 0 and nb + 1 <= MOD_ROWS
    dims = {
        "b": nb, "n": n, "l": lc,
        "tm_ffn": _largest_tile(512, n, nb * lc),
        "tm_proj": _largest_tile(512, n, nb * lc),
        "tm_feat": _largest_tile(256, n, lc),
        "key_chunk": _largest_tile(512, n),
        "tq_global": _largest_tile(256, n, lc),
    }

    cvec = jnp.zeros((MOD_ROWS, d), F32).at[:nb].set(c).at[nb].set(c_ctx)
    mods = _mod_table(cvec, w_mod, b_mod).reshape(depth, MOD_ROWS, N_MOD, d)

    ffn_in_h = ffn_in.astype(BF16)
    ffn_out_h = ffn_out.astype(BF16)
    even_in_h = even_w_in.astype(BF16)
    even_out_h = even_w_out.astype(BF16)
    odd_in_h = odd_w_in.astype(BF16)
    odd_out_h = odd_w_out.astype(BF16)

    rope_tabs = _rope_tables(n, dims["tm_proj"])
    a_kw = A_KV_HEADS * HEAD_DIM
    a_qw = A_HEADS * HEAD_DIM
    bd_q = _block_diag(a_qw, 1.0 / HEAD_DIM, BF16)
    bd_k = _block_diag(a_kw, 1.0 / HEAD_DIM, BF16)
    bd_ones = _block_diag(B_WIDTH, 1.0, BF16)
    fw = even_w_in.shape[2] - a_qw - 2 * a_kw

    xs = x.reshape(nb * n, d)
    for l in range(depth):
        last = l == depth - 1
        xs = _ffn(xs, mods, ffn_in_h, ffn_out_h, l, 0, 0, dims, ctx_rows=ctx.reshape(nb * lc, d) if l == 0 else None)
        if l % 2 == 0:
            e = l // 2
            tile_h = lambda g, reps: jnp.tile(g, reps).reshape(1, -1)
            norm_args = (tile_h(q_gain[e], A_HEADS), tile_h(k_gain[e], A_KV_HEADS), bd_q, bd_k)
            q, k, v, f = _proj(xs, mods, even_in_h, l, e, rope_tabs, dims, a_qw, a_kw, fw, norm_args)
            oa = _attention(q, k, v, dims, ctx_queries=not last)
            feat_params = (rwkv_mu[e], rwkv_w0[e], _pad_lora(rwkv_w2[e]).astype(BF16), rwkv_a0[e],
                           _pad_lora(rwkv_a2[e]).astype(BF16), rwkv_g2[e].astype(BF16),
                           rwkv_k_k[e].reshape(1, -1), rwkv_k_a[e].reshape(1, -1), bd_ones)
            r, kr, vr, kk, g, lw0, lw1, kd0, kd1, b0, b1 = _rwkv_features(f, feat_params, dims)
            yf, yb = _wkv_scan((r, vr, kk, lw0, lw1, kd0, kd1, b0, b1), dims)
            readout = (yf, yb, r, kr, vr, g, rwkv_r_k[e].reshape(1, -1), rwkv_gn_w[e].reshape(1, -1),
                       rwkv_gn_b[e].reshape(1, -1), bd_ones)
            mixer = (even_out_h, e, oa, readout)
        else:
            o = l // 2
            c_kw = C_KV_HEADS * HEAD_DIM
            c_qw = C_HEADS * HEAD_DIM
            q, k, v = _proj(xs, mods, odd_in_h, l, o, rope_tabs, dims, c_qw, c_kw, 0)
            oc = _attention(q, k, v, dims, sink=sink[o], ctx_queries=not last)
            mixer = (odd_out_h, o, oc, None)
        xs = _ffn(xs, mods, ffn_in_h, ffn_out_h, l, 1, 6, dims, final_gain=final_gain if last else None, mixer=mixer)
    return xs.reshape(nb, n, d)
```

```python
import functools

import jax
import jax.numpy as jnp
import numpy as np
from jax import lax
from jax.experimental import pallas as pl
from jax.experimental.pallas import tpu as pltpu

F32 = jnp.float32
BF16 = jnp.bfloat16

HEAD_DIM = 64
GRID_W = 64
A_HEADS, A_KV_HEADS = 8, 2
B_HEADS = 8
B_WIDTH = B_HEADS * HEAD_DIM
DECAY_LORA, ICLR_LORA, GATE_LORA = 64, 64, 128
C_HEADS, C_KV_HEADS = 16, 4
GQA_GROUP = 4
WINDOW = 128
N_MOD = 9
ROPE_THETA = 10000.0
EPS = 1e-6
GN_EPS = 64e-5
NEG_BIG = -1e30
LOG2_E = 1.4426950408889634
EXP_NEG_HALF = 0.6065306597126334

LANES = 128
SUBLANES = 8
VMEM_LIMIT_BYTES = 56 * 1024 * 1024

SCAN_CHUNK = 64
SCAN_STEP_CHUNKS = 2
PAIR_ROWS = 2 * SCAN_CHUNK
TQ = 128
AHEAD = 4
ONES_ROWS = 16
MOD_ROWS = 16


def _cparams(sem):
    return pltpu.CompilerParams(dimension_semantics=sem, vmem_limit_bytes=VMEM_LIMIT_BYTES)


def _dot(a, b, precision=None):
    return jnp.dot(a, b, preferred_element_type=F32, precision=precision)


def _dot_nt(a, b, precision=None):
    return lax.dot_general(a, b, (((1,), (1,)), ((), ())), preferred_element_type=F32, precision=precision)


def _dot_tn(a, b, precision=None):
    return lax.dot_general(a, b, (((0,), (0,)), ((), ())), preferred_element_type=F32, precision=precision)


def _rms(x):
    return x * lax.rsqrt(jnp.mean(x * x, axis=-1, keepdims=True) + EPS)


def _sigmoid(x):
    return 1.0 / (1.0 + jnp.exp(-x))


def _headsum(x, bd, split=False):
    hi = x.astype(BF16)
    if not split:
        return _dot(hi, bd)
    lo = (x - hi.astype(F32)).astype(BF16)
    return _dot(hi, bd) + _dot(lo, bd)


def _mod_kernel(c_ref, w_ref, b_ref, o_ref):
    c = c_ref[...]
    s = c * _sigmoid(c)
    hi = s.astype(BF16)
    lo = (s - hi.astype(F32)).astype(BF16)
    w = w_ref[...].astype(BF16)
    o_ref[...] = _dot(hi, w) + _dot(lo, w) + b_ref[...]


def _mod_table(cvec, w_mod, b_mod):
    depth, d, nd = w_mod.shape
    tn = 1536 if nd % 1536 == 0 else nd
    return pl.pallas_call(
        _mod_kernel,
        grid=(depth, nd // tn),
        in_specs=[
            pl.BlockSpec((MOD_ROWS, d), lambda l, n: (0, 0)),
            pl.BlockSpec((None, d, tn), lambda l, n: (l, 0, n)),
            pl.BlockSpec((None, 1, tn), lambda l, n: (l, 0, n)),
        ],
        out_specs=pl.BlockSpec((None, MOD_ROWS, tn), lambda l, n: (l, 0, n)),
        out_shape=jax.ShapeDtypeStruct((depth, MOD_ROWS, nd), F32),
        compiler_params=_cparams(("parallel", "parallel")),
        name="mod_table",
    )(cvec, w_mod, b_mod.reshape(depth, 1, nd))


def _mixer_out(oa_ref, wm_ref, readout):
    if readout is None:
        return _dot(oa_ref[...], wm_ref[...])
    yf, yb, r, k, v, g, rk, gnw, gnb, bd = readout
    y = yf[...] + yb[...]
    mean = _headsum(y, bd[...], split=True) * (1.0 / HEAD_DIM)
    yc = y - mean
    var = _headsum(yc * yc, bd[...]) * (1.0 / HEAD_DIM)
    yn = yc * lax.rsqrt(var + GN_EPS) * gnw[...] + gnb[...]
    up = lambda ref: ref[...].astype(F32)
    bonus = _headsum(up(r) * up(k) * rk[...], bd[...]) * up(v)
    ob = ((yn + bonus) * up(g)).astype(BF16)
    half = oa_ref.shape[1]
    return _dot(oa_ref[...], wm_ref[:half, :]) + _dot(ob, wm_ref[half:, :])


def _ffn_kernel(x_ref, mod_ref, wg_ref, wu_ref, wo_ref, *rest, i_shift, final, n_lat_tiles, mixer):
    o_ref = rest[-1]
    ctx_ref = None
    if n_lat_tiles is not None:
        ctx_ref, rest = rest[0], rest[1:]
    x = x_ref[...]
    if ctx_ref is not None:
        x = jnp.where(pl.program_id(0) < n_lat_tiles, x, ctx_ref[...])
    if mixer is not None:
        readout = rest[2:12] if mixer == "even" else None
        x = x + mod_ref[5:6, :] * _mixer_out(rest[1], rest[0], readout)
    half = x.shape[0] // 2
    shift, scale = mod_ref[i_shift:i_shift + 1, :], 1.0 + mod_ref[i_shift + 1:i_shift + 2, :]
    gate = 0.5 * mod_ref[i_shift + 2:i_shift + 3, :]
    xs, gu = [x[:half], x[half:]], []
    for xh in xs:
        h = (_rms(xh) * scale + shift).astype(BF16)
        gu.append((_dot(h, wg_ref[...]), _dot(h, wu_ref[...])))
    for idx, (x, (g, u)) in enumerate(zip(xs, gu)):
        a = (g * _sigmoid(g) * u).astype(BF16)
        y = x + gate * _dot(a, wo_ref[...])
        if final:
            y = _rms(y) * rest[-2][...]
        o_ref[idx * half:(idx + 1) * half, :] = y


def _ffn(xs, mods, w_in, w_out, l, j, i_shift, dims, final_gain=None, ctx_rows=None, mixer=None):
    rows, d = xs.shape
    dff = w_out.shape[2]
    tm = dims["tm_ffn"]
    tiles_per_batch = dims["n"] // tm
    nb = dims["b"]
    n_lat_tiles = nb * tiles_per_batch
    final = final_gain is not None
    if final:
        rows = nb * dims["n"]
    split = ctx_rows is not None
    if split:
        rows = rows + ctx_rows.shape[0]
    in_specs = [
        pl.BlockSpec((tm, d), (lambda i: (jnp.minimum(i, n_lat_tiles - 1), 0)) if split else (lambda i: (i, 0))),
        pl.BlockSpec((None, None, N_MOD, d), lambda i: (l, jnp.minimum(i // tiles_per_batch, nb), 0, 0)),
        pl.BlockSpec((None, None, d, dff), lambda i: (l, j, 0, 0)),
        pl.BlockSpec((None, None, d, dff), lambda i: (l, j, 0, 1)),
        pl.BlockSpec((None, None, dff, d), lambda i: (l, j, 0, 0)),
    ]
    args = [xs, mods, w_in, w_in, w_out]
    if split:
        in_specs.append(pl.BlockSpec((tm, d), lambda i: (jnp.maximum(i - n_lat_tiles, 0), 0)))
        args.append(ctx_rows)
    mixer_kind = None
    if mixer is not None:
        w_mix, e, o_att, readout = mixer
        mixer_kind = "odd" if readout is None else "even"
        in_specs += [pl.BlockSpec((None, d, d), lambda i: (e, 0, 0)),
                     pl.BlockSpec((tm, o_att.shape[1]), lambda i: (i, 0))]
        args += [w_mix, o_att]
        if readout is not None:
            bw = B_WIDTH
            in_specs += [pl.BlockSpec((tm, bw), lambda i: (i, 0))] * 6
            in_specs += [pl.BlockSpec((1, bw), lambda i: (0, 0))] * 3 + [pl.BlockSpec((bw, bw), lambda i: (0, 0))]
            args += list(readout)
    if final:
        in_specs.append(pl.BlockSpec((1, d), lambda i: (0, 0)))
        args.append(final_gain.reshape(1, d))
    return pl.pallas_call(
        functools.partial(_ffn_kernel, i_shift=i_shift, final=final, n_lat_tiles=n_lat_tiles if split else None,
                          mixer=mixer_kind),
        grid=(rows // tm,),
        in_specs=in_specs,
        out_specs=pl.BlockSpec((tm, d), lambda i: (i, 0)),
        out_shape=jax.ShapeDtypeStruct((rows, d), F32),
        compiler_params=_cparams(("parallel",)),
        name="ffn",
    )(*args)


def _rope(x, cos, sin_a, sin_b):
    outs = []
    for g in range(x.shape[1] // LANES):
        xg = x[:, g * LANES:(g + 1) * LANES]
        nxt = pltpu.roll(xg, LANES - 1, 1)
        prv = pltpu.roll(xg, 1, 1)
        outs.append(xg * cos + nxt * sin_a + prv * sin_b)
    return outs[0] if len(outs) == 1 else jnp.concatenate(outs, axis=1)


def _proj_kernel(x_ref, mod_ref, w_ref, cos_ref, sa_ref, sb_ref, *rest, qw, kw, fw, qk_norm):
    if qk_norm:
        qg_ref, kg_ref, bdq_ref, bdk_ref = rest[:4]
        rest = rest[4:]
    q_ref, k_ref, v_ref = rest[:3]
    hn = (_rms(x_ref[...]) * (1.0 + mod_ref[4:5, :]) + mod_ref[3:4, :]).astype(BF16)
    q = _dot(hn, w_ref[:, :qw])
    k = _dot(hn, w_ref[:, qw:qw + kw])
    v = _dot(hn, w_ref[:, qw + kw:qw + 2 * kw])
    if qk_norm:
        q = q * lax.rsqrt(_headsum(q * q, bdq_ref[...]) + EPS) * qg_ref[...]
        k = k * lax.rsqrt(_headsum(k * k, bdk_ref[...]) + EPS) * kg_ref[...]
    cos, sa, sb = cos_ref[...], sa_ref[...], sb_ref[...]
    q_ref[...] = (_rope(q, cos, sa, sb) * (HEAD_DIM ** -0.5 * LOG2_E)).T.astype(BF16)
    k_ref[...] = _rope(k, cos, sa, sb).astype(BF16)
    v_ref[...] = v.T.astype(BF16)
    if fw:
        rest[3][...] = _dot(hn, w_ref[:, qw + 2 * kw:])


def _proj(xs, mods, w_in, l, e, rope_tabs, dims, qw, kw, fw, norm_args=None):
    rows, d = xs.shape
    tm = dims["tm_proj"]
    cols = qw + 2 * kw + fw
    tiles_per_batch = dims["n"] // tm
    n_lat_tiles = dims["b"] * tiles_per_batch
    nb = dims["b"]

    def tab_idx(i):
        return (jnp.where(i < n_lat_tiles, i % tiles_per_batch, tiles_per_batch), 0)

    in_specs = [
        pl.BlockSpec((tm, d), lambda i: (i, 0)),
        pl.BlockSpec((None, None, N_MOD, d), lambda i: (l, jnp.minimum(i // tiles_per_batch, nb), 0, 0)),
        pl.BlockSpec((None, d, cols), lambda i: (e, 0, 0)),
        pl.BlockSpec((tm, LANES), tab_idx),
        pl.BlockSpec((tm, LANES), tab_idx),
        pl.BlockSpec((tm, LANES), tab_idx),
    ]
    args = [xs, mods, w_in, *rope_tabs]
    qk_norm = norm_args is not None
    if qk_norm:
        qg, kg, bdq, bdk = norm_args
        in_specs += [
            pl.BlockSpec((1, qw), lambda i: (0, 0)),
            pl.BlockSpec((1, kw), lambda i: (0, 0)),
            pl.BlockSpec((qw, qw), lambda i: (0, 0)),
            pl.BlockSpec((kw, kw), lambda i: (0, 0)),
        ]
        args += [qg, kg, bdq, bdk]
    out_specs = [
        pl.BlockSpec((qw, tm), lambda i: (0, i)),
        pl.BlockSpec((tm, kw), lambda i: (i, 0)),
        pl.BlockSpec((kw, tm), lambda i: (0, i)),
    ]
    out_shape = [
        jax.ShapeDtypeStruct((qw, rows), BF16),
        jax.ShapeDtypeStruct((rows, kw), BF16),
        jax.ShapeDtypeStruct((kw, rows), BF16),
    ]
    if fw:
        out_specs.append(pl.BlockSpec((tm, fw), lambda i: (i, 0)))
        out_shape.append(jax.ShapeDtypeStruct((rows, fw), F32))
    return pl.pallas_call(
        functools.partial(_proj_kernel, qw=qw, kw=kw, fw=fw, qk_norm=qk_norm),
        grid=(rows // tm,),
        in_specs=in_specs,
        out_specs=out_specs,
        out_shape=out_shape,
        compiler_params=_cparams(("parallel",)),
        name="mixer_proj",
    )(*args)


def _attend(qt_ref, segs, sink_ref, n_pad, o_ref):
    tq = qt_ref.shape[1]
    cols = GQA_GROUP * tq
    heads = range(qt_ref.shape[0] // (GQA_GROUP * HEAD_DIM))
    pair = lambda h: slice((h // 2) * LANES, (h // 2 + 1) * LANES)
    qpad, m, acc = [], [], []
    for h in heads:
        q4 = jnp.concatenate([qt_ref[(h * GQA_GROUP + g) * HEAD_DIM:(h * GQA_GROUP + g + 1) * HEAD_DIM, :]
                              for g in range(GQA_GROUP)], axis=1)
        z = jnp.zeros_like(q4)
        qpad.append(jnp.concatenate([q4, z] if h % 2 == 0 else [z, q4], axis=0))
        if sink_ref is not None:
            mh, lh = sink_ref[h], jnp.ones((1, cols), F32)
        else:
            mh, lh = jnp.full((1, cols), NEG_BIG, F32), jnp.zeros((1, cols), F32)
        if n_pad is not None:
            m_new = jnp.maximum(mh, jnp.where(n_pad > 0.0, 0.0, NEG_BIG))
            lh = lh * jnp.exp2(mh - m_new) + jnp.where(n_pad > 0.0, n_pad * jnp.exp2(-m_new), 0.0)
            mh = m_new
        m.append(mh)
        acc.append(jnp.concatenate([jnp.zeros((HEAD_DIM, cols), F32), jnp.broadcast_to(lh, (ONES_ROWS, cols))], axis=0))
    def scores(item):
        (k, _, valid), h = item
        s = _dot(k[:, pair(h)], qpad[h])
        if valid is not None:
            s = jnp.where(jnp.concatenate([valid] * GQA_GROUP, axis=1), s, NEG_BIG)
        return s

    items = [(seg, h) for seg in segs for h in heads]
    pending = [scores(it) for it in items[:AHEAD]]
    for idx, ((_, vt, _), h) in enumerate(items):
        s = pending.pop(0)
        if idx + AHEAD < len(items):
            pending.append(scores(items[idx + AHEAD]))
        m_new = jnp.maximum(m[h], jnp.max(s, axis=0, keepdims=True))
        alpha = jnp.exp2(m[h] - m_new)
        p = jnp.exp2(s - m_new).astype(BF16)
        v1 =jnp.concatenate([vt[h * HEAD_DIM:(h + 1) * HEAD_DIM, :], jnp.ones((ONES_ROWS, vt.shape[1]), BF16)], axis=0)
        acc[h] = alpha * acc[h] + _dot(v1, p)
        m[h] = m_new
    blocks = []
    for h in heads:
        ot = acc[h][:HEAD_DIM, :] * (1.0 / acc[h][HEAD_DIM:HEAD_DIM + 1, :])
        for g in range(0, GQA_GROUP, 2):
            blocks.append(jnp.concatenate([ot[:, g * tq:(g + 1) * tq], ot[:, (g + 1) * tq:(g + 2) * tq]], axis=0).T)
    o_ref[...] = jnp.concatenate(blocks, axis=1).astype(o_ref.dtype)


def _attn_global_kernel(qt_ref, kl_ref, vl_ref, kc_ref, vc_ref, o_ref, *, n_q_lat, key_chunk):
    i = pl.program_id(1)
    ctx_seg = (kc_ref[...], vc_ref[...], None)

    @pl.when(i < n_q_lat)
    def _():
        segs = [(kl_ref[c * key_chunk:(c + 1) * key_chunk, :], vl_ref[:, c * key_chunk:(c + 1) * key_chunk], None)
                for c in range(kl_ref.shape[0] // key_chunk)]
        _attend(qt_ref, segs + [ctx_seg], None, None, o_ref)

    @pl.when(i >= n_q_lat)
    def _():
        _attend(qt_ref, [ctx_seg], None, None, o_ref)


def _attn_window_kernel(qt_ref, k0_ref, k1_ref, k2_ref, v0_ref, v1_ref, v2_ref, kc_ref, vc_ref, sink_ref, o_ref,
                        *, n, n_q_lat):
    i = pl.program_id(1)
    tq = qt_ref.shape[1]
    ctx_seg = (kc_ref[...], vc_ref[...], None)

    @pl.when(i < n_q_lat)
    def _():
        k = jnp.concatenate([k0_ref[...], k1_ref[...], k2_ref[...]], axis=0)
        vt = jnp.concatenate([v0_ref[...], v1_ref[...], v2_ref[...]], axis=1)
        kpos = (i - 1) * tq + lax.broadcasted_iota(jnp.int32, (3 * tq, tq), 0)
        qpos = i * tq + lax.broadcasted_iota(jnp.int32, (3 * tq, tq), 1)
        valid = (jnp.abs(kpos - qpos) <= WINDOW) & (kpos >= WINDOW) & (kpos < n)
        q_row = i * tq + lax.broadcasted_iota(jnp.int32, (1, tq), 1)
        n_pad = jnp.maximum(q_row + WINDOW - n + 1, 0).astype(F32)
        n_pad = jnp.concatenate([n_pad] * GQA_GROUP, axis=1)
        _attend(qt_ref, [(k, vt, valid), ctx_seg], sink_ref, n_pad, o_ref)

    @pl.when(i >= n_q_lat)
    def _():
        _attend(qt_ref, [ctx_seg], sink_ref, None, o_ref)


def _attention(qt, k, vt, dims, sink=None, ctx_queries=True):
    qw, rows = qt.shape
    kw = k.shape[1]
    n, lc, nb = dims["n"], dims["l"], dims["b"]
    tq = TQ if sink is not None else dims["tq_global"]
    n_q_lat = n // tq
    n_q_ctx = lc // tq
    lat_blocks = nb * n_q_lat
    ctx_block = nb * n // lc

    def q_blk(b, i):
        return jnp.where(i < n_q_lat, b * n_q_lat + i, lat_blocks + b * n_q_ctx + (i - n_q_lat))

    ctx_specs = [pl.BlockSpec((lc, kw), lambda b, i: (ctx_block + b, 0)),
                 pl.BlockSpec((kw, lc), lambda b, i: (0, ctx_block + b))]
    qt_spec = pl.BlockSpec((qw, tq), lambda b, i: (0, q_blk(b, i)))
    if sink is None:
        body = functools.partial(_attn_global_kernel, n_q_lat=n_q_lat, key_chunk=dims["key_chunk"])
        in_specs = [qt_spec,
                    pl.BlockSpec((n, kw), lambda b, i: (b, 0)),
                    pl.BlockSpec((kw, n), lambda b, i: (0, b))] + ctx_specs
        args = [qt, k, vt, k, vt]
    else:
        assert WINDOW == TQ
        body = functools.partial(_attn_window_kernel, n=n, n_q_lat=n_q_lat)
        nbr = lambda b, i, off: b * n_q_lat + jnp.clip(i + off, 0, n_q_lat - 1)
        n_kv = kw // HEAD_DIM
        in_specs = ([qt_spec]
                    + [pl.BlockSpec((TQ, kw), functools.partial(lambda b, i, off: (nbr(b, i, off), 0), off=off))
                       for off in (-1, 0, 1)]
                    + [pl.BlockSpec((kw, TQ), functools.partial(lambda b, i, off: (0, nbr(b, i, off)), off=off))
                       for off in (-1, 0, 1)]
                    + ctx_specs
                    + [pl.BlockSpec((n_kv, 1, GQA_GROUP * TQ), lambda b, i: (0, 0, 0))])
        sink_rows = jnp.repeat(sink.astype(F32).reshape(n_kv, 1, GQA_GROUP) * LOG2_E, TQ, axis=2)
        args = [qt, k, k, k, vt, vt, vt, k, vt, sink_rows]
    return pl.pallas_call(
        body,
        grid=(nb, n_q_lat + (n_q_ctx if ctx_queries else 0)),
        in_specs=in_specs,
        out_specs=pl.BlockSpec((tq, qw), lambda b, i: (q_blk(b, i), 0)),
        out_shape=jax.ShapeDtypeStruct((rows if ctx_queries else nb * n, qw), BF16),
        compiler_params=_cparams(("parallel", "arbitrary")),
        name="gqa_global" if sink is None else "gqa_window",
    )(*args)


def _feat_kernel(f_ref, fp_ref, fn_ref, mu_ref, w0_ref, w2_ref, a0_ref, a2_ref, g2_ref, kk_ref, ka_ref, bd_ref,
                 r_o, k_o, v_o, kk_o, g_o, lw0_o, lw1_o, kd0_o, kd1_o, b0_o, b1_o,
                 *, n_lat_tiles, tiles_lat, tiles_ctx):
    i = pl.program_id(0)
    t = f_ref.shape[0]
    bw = B_WIDTH
    is_lat = i < n_lat_tiles
    pos = jnp.where(is_lat, i % tiles_lat, (i - n_lat_tiles) % tiles_ctx)
    last = jnp.where(is_lat, tiles_lat, tiles_ctx) - 1
    f = f_ref[...]
    prow = jnp.where(pos == 0, 0.0, fp_ref[SUBLANES - 1:SUBLANES, :])
    nrow = jnp.where(pos == last, 0.0, fn_ref[0:1, :])
    rid = lax.broadcasted_iota(jnp.int32, (SUBLANES, 1), 0)
    prev = pltpu.roll(f, 1, 0)
    prev = jnp.concatenate([jnp.where(rid == 0, prow, prev[:SUBLANES]), prev[SUBLANES:]], axis=0)
    nxt = pltpu.roll(f, t - 1, 0)
    nxt = jnp.concatenate([nxt[:t - SUBLANES], jnp.where(rid == SUBLANES - 1, nrow, nxt[t - SUBLANES:])], axis=0)
    mu_p, mu_n = mu_ref[0:1, :], mu_ref[1:2, :]
    fs = f * (1.0 - mu_p - mu_n) + mu_p * prev + mu_n * nxt

    r = fs[:, :bw]
    k = fs[:, bw:2 * bw]
    v = fs[:, 2 * bw:3 * bw]
    wl = jnp.tanh(fs[:, 3 * bw:3 * bw + 2 * DECAY_LORA])
    al = fs[:, 3 * bw + 2 * DECAY_LORA:3 * bw + 2 * DECAY_LORA + 2 * ICLR_LORA]
    gl = fs[:, 3 * bw + 2 * DECAY_LORA + 2 * ICLR_LORA:]

    kk = k * kk_ref[...]
    kk = kk * lax.rsqrt(_headsum(kk * kk, bd_ref[...]) + EPS)
    r_o[...] = r.astype(BF16)
    k_o[...] = k.astype(BF16)
    v_o[...] = v.astype(BF16)
    kk_o[...] = kk.astype(BF16)
    g_o[...] = _dot(_sigmoid(gl).astype(BF16), g2_ref[...]).astype(BF16)
    wl = wl.astype(BF16)
    al = al.astype(BF16)
    for d, (lw_o, kd_o, b_o) in enumerate(((lw0_o, kd0_o, b0_o), (lw1_o, kd1_o, b1_o))):
        w_log = w0_ref[d:d + 1, :] + _dot(wl, w2_ref[d])
        lw_o[...] = -EXP_NEG_HALF * _sigmoid(w_log)
        a = _sigmoid(a0_ref[d:d + 1, :] + _dot(al, a2_ref[d]))
        kd_o[...] = (k * ((1.0 - ka_ref[...]) + a * ka_ref[...])).astype(BF16)
        b_o[...] = (kk * a).astype(BF16)


def _rwkv_features(f, params, dims):
    rows, fw = f.shape
    t = dims["tm_feat"]
    n, lc, nb = dims["n"], dims["l"], dims["b"]
    tiles_lat, tiles_ctx = n // t, lc // t
    n_lat_tiles = nb * tiles_lat
    hb = t // SUBLANES
    n_halo = rows // SUBLANES
    mu, w0, w2p, a0, a2p, g2, k_k, k_a, bd = params
    bw = B_WIDTH
    const = lambda shape: pl.BlockSpec(shape, lambda i: (0,) * len(shape))
    in_specs = [
        pl.BlockSpec((t, fw), lambda i: (i, 0)),
        pl.BlockSpec((SUBLANES, fw), lambda i: (jnp.maximum(i * hb - 1, 0), 0)),
        pl.BlockSpec((SUBLANES, fw), lambda i: (jnp.minimum((i + 1) * hb, n_halo - 1), 0)),
        const((2, fw)), const((2, bw)), const((2, 2 * DECAY_LORA, bw)), const((2, bw)),
        const((2, 2 * ICLR_LORA, bw)), const((GATE_LORA, bw)), const((1, bw)), const((1, bw)), const((bw, bw)),
    ]
    out = lambda dtype: jax.ShapeDtypeStruct((rows, bw), dtype)
    return pl.pallas_call(
        functools.partial(_feat_kernel, n_lat_tiles=n_lat_tiles, tiles_lat=tiles_lat, tiles_ctx=tiles_ctx),
        grid=(rows // t,),
        in_specs=in_specs,
        out_specs=[pl.BlockSpec((t, bw), lambda i: (i, 0))] * 11,
        out_shape=[out(BF16)] * 5 + [out(F32)] * 2 + [out(BF16)] * 4,
        compiler_params=_cparams(("parallel",)),
        name="rwkv_features",
    )(f, f, f, mu, w0, w2p, a0, a2p, g2, k_k, k_a, bd)


def _stack_heads(x, head0):
    return jnp.concatenate([jnp.where(head0, x, 0.0), jnp.where(head0, 0.0, x)], axis=0).astype(BF16)


def _cumsum_rows(tri, x):
    hi = x.astype(BF16)
    r1 = x - hi.astype(F32)
    mid = r1.astype(BF16)
    lo = (r1 - mid.astype(F32)).astype(BF16)
    return _dot(tri, hi) + _dot(tri, mid) + _dot(tri, lo)


def _wkv_units(units, eye, lvl_ref):
    c = units[0]["v"].shape[0]
    p2 = 2 * c
    head0 = lax.broadcasted_iota(jnp.int32, (1, LANES), 1) < HEAD_DIM
    st = lambda x: _stack_heads(x, head0)
    fold = lambda x: x[:c] + x[c:]
    each = lambda fn: [fn(u) for u in units]

    for u in units:
        u["kq_s"], u["rq_s"], u["v_s"] = st(u["kq"]), st(u["rq"]), st(u["v"])
    a = each(lambda u: _dot_nt(jnp.concatenate([u["kq_s"], u["rq_s"]], axis=0),
                               jnp.concatenate([st(u["bi"]), st(u["ki"])], axis=0)))
    for u, au in zip(units, a):
        u["nkb"] = jnp.where(u["m_strict"], au[:p2, :p2], 0.0)
        u["akk"] = jnp.where(u["m_strict"], au[:p2, p2:], 0.0).astype(BF16)
        u["arb"] = jnp.where(u["m_incl"], au[p2:, :p2], 0.0).astype(BF16)
        u["ark"] = jnp.where(u["m_incl"], au[p2:, p2:], 0.0).astype(BF16)

    nkb_h = each(lambda u: u["nkb"].astype(BF16))
    tinv = each(lambda u: eye - u["nkb"] * lvl_ref[0].astype(F32))
    for lev in range(1, lvl_ref.shape[0]):
        t_h = [t.astype(BF16) for t in tinv]
        lt = [_dot(nh * lvl_ref[lev], th).astype(BF16) for nh, th in zip(nkb_h, t_h)]
        tinv = [t - _dot(th, x) for t, th, x in zip(tinv, t_h, lt)]
    t_h = [t.astype(BF16) for t in tinv]

    akkv = each(lambda u: _dot(u["akk"], u["v_s"]).astype(BF16))
    gp_h = [_dot(th, jnp.concatenate([u["kq_s"], x], axis=1)).astype(BF16)
            for u, th, x in zip(units, t_h, akkv)]
    corr = [_dot(u["arb"], g) for u, g in zip(units, gp_h)]
    arkv = each(lambda u: _dot(u["ark"], u["v_s"]))
    btg = [_dot_tn(st(u["bt"]), g) for u, g in zip(units, gp_h)]
    ktv = each(lambda u: _dot_tn(st(u["kt"]), u["v_s"]))

    for u, cr, av, bg, kv in zip(units, corr, arkv, btg, ktv):
        rq = u["rq"]
        rq2 = fold(jnp.concatenate([jnp.where(head0, rq, 0.0), jnp.where(head0, 0.0, rq)], axis=0) - cr[:, :LANES])
        u["rq2"] = rq2.astype(BF16)
        u["yl"] = fold(av - cr[:, LANES:])
        u["m_state"] = (jnp.where(eye > 0.0, u["decay"], 0.0) - bg[:, :LANES]).astype(BF16)
        u["n_state"] = kv - bg[:, LANES:]


def _scan_kernel(rf, vf, kkf, lwf, kdf, bf, rb, vb, kkb, lwb, kdb, bb, tri_ref, msk_ref, lvl_ref,
                 yf_o, yb_o, hf_s, hb_s):
    @pl.when(pl.program_id(1) == 0)
    def _():
        hf_s[...] = jnp.zeros_like(hf_s)
        hb_s[...] = jnp.zeros_like(hb_s)

    c = SCAN_CHUNK
    eye = msk_ref[0]
    units, chains = [], []
    for d, (refs, y_o, h_s) in enumerate((((rf, vf, kkf, lwf, kdf, bf), yf_o, hf_s),
                                          ((rb, vb, kkb, lwb, kdb, bb), yb_o, hb_s))):
        m_strict = msk_ref[1 + 2 * d] > 0.0
        m_incl = msk_ref[2 + 2 * d] > 0.0
        r, v, kk, lw, kd, b = (ref[...] for ref in refs)
        cw = _cumsum_rows(tri_ref[d], lw)
        tot = [jnp.sum(lw[j * c:(j + 1) * c], axis=0, keepdims=True) for j in range(SCAN_STEP_CHUNKS)]
        tot_rows = jnp.concatenate([jnp.broadcast_to(t, (c, t.shape[1])) for t in tot], axis=0)
        w_inv = jnp.exp(-cw)
        w_rest = jnp.exp(tot_rows - cw)
        facs = {"kq": kk * jnp.exp(cw - lw), "rq": r * jnp.exp(cw), "bi": b * w_inv, "ki": kd * w_inv,
                "bt": b * w_rest, "kt": kd * w_rest, "v": v}
        order = range(SCAN_STEP_CHUNKS) if d == 0 else range(SCAN_STEP_CHUNKS - 1, -1, -1)
        for p in range(B_WIDTH // LANES):
            ls = slice(p * LANES, (p + 1) * LANES)
            chain = []
            for j in order:
                unit = {name: x[j * c:(j + 1) * c, ls] for name, x in facs.items()}
                unit.update(decay=jnp.exp(tot[j][:, ls]), m_strict=m_strict, m_incl=m_incl, rows=slice(j * c, (j + 1) * c))
                units.append(unit)
                chain.append(unit)
            chains.append((chain, y_o, h_s, p, ls))
    _wkv_units(units, eye, lvl_ref)
    states = [h_s[p] for _, _, h_s, p, _ in chains]
    for pos in range(SCAN_STEP_CHUNKS):
        for ci, (chain, y_o, _, _, ls) in enumerate(chains):
            u = chain[pos]
            h_h = states[ci].astype(BF16)
            y_o[u["rows"], ls] = _dot(u["rq2"], h_h) + u["yl"]
            states[ci] = _dot(u["m_state"], h_h) + u["n_state"]
    for h, (_, _, h_s, p, _) in zip(states, chains):
        h_s[p] = h


def _scan_constants():
    c, p2 = SCAN_CHUNK, PAIR_ROWS
    t = np.arange(SCAN_STEP_CHUNKS * c)
    same_chunk = (t[None, :] // c) == (t[:, None] // c)
    tri = np.stack([same_chunk & (t[None, :] <= t[:, None]), same_chunk & (t[None, :] >= t[:, None])]).astype(np.float32)
    i = np.arange(p2)
    same_head = (i[:, None] // c) == (i[None, :] // c)
    ti, si = i[:, None] % c, i[None, :] % c
    msk = np.stack([
        np.eye(p2, dtype=bool),
        same_head & (si < ti), same_head & (si <= ti),
        same_head & (si > ti), same_head & (si >= ti),
    ]).astype(np.float32)
    n_lev = int(np.log2(c))
    lvl = np.stack([
        ((i[:, None] >> (k + 1)) == (i[None, :] >> (k + 1))) & ((i[:, None] >> k) != (i[None, :] >> k))
        for k in range(n_lev)
    ]).astype(np.float32)
    return jnp.asarray(tri, BF16), jnp.asarray(msk), jnp.asarray(lvl, BF16)


def _wkv_scan(feats, dims):
    r, v, kk, lw0, lw1, kd0, kd1, b0, b1 = feats
    rows, bw = r.shape
    c = SCAN_STEP_CHUNKS * SCAN_CHUNK
    n, lc, nb = dims["n"], dims["l"], dims["b"]
    n_c, l_c = n // c, lc // c
    ctx_base = nb * n_c
    tri, msk, lvl = _scan_constants()

    def fwd(b, s):
        return (jnp.where(s < l_c, ctx_base + b * l_c + s, b * n_c + (s - l_c)), 0)

    def bwd(b, s):
        return (jnp.where(s < l_c, ctx_base + b * l_c + (l_c - 1 - s), b * n_c + (n_c - 1 - (s - l_c))), 0)

    const = lambda a: pl.BlockSpec(a.shape, lambda b, s: (0,) * a.ndim)
    out = jax.ShapeDtypeStruct((rows, bw), F32)
    return pl.pallas_call(
        _scan_kernel,
        grid=(nb, l_c + n_c),
        in_specs=[pl.BlockSpec((c, bw), fwd)] * 6 + [pl.BlockSpec((c, bw), bwd)] * 6
                 + [const(tri), const(msk), const(lvl)],
        out_specs=[pl.BlockSpec((c, bw), fwd), pl.BlockSpec((c, bw), bwd)],
        out_shape=[out, out],
        scratch_shapes=[pltpu.VMEM((bw // LANES, LANES, LANES), F32)] * 2,
        compiler_params=_cparams(("parallel", "arbitrary")),
        name="wkv_scan",
    )(r, v, kk, lw0, kd0, b0, r, v, kk, lw1, kd1, b1, tri, msk, lvl)


def _rope_tables(n, tm):
    rows = n // GRID_W
    row = jnp.repeat(jnp.arange(rows, dtype=F32), GRID_W)
    col = jnp.tile(jnp.arange(GRID_W, dtype=F32), rows)
    n_freq = HEAD_DIM // 4
    inv_freq = ROPE_THETA ** (-jnp.arange(n_freq, dtype=F32) / n_freq)
    ang = jnp.concatenate([row[:, None] * inv_freq, col[:, None] * inv_freq], axis=-1)
    cos = jnp.repeat(jnp.cos(ang), 2, axis=-1)
    sin = jnp.repeat(jnp.sin(ang), 2, axis=-1)
    even_lane = (jnp.arange(HEAD_DIM) % 2 == 0)[None, :]
    sin_a = jnp.where(even_lane, -sin, 0.0)
    sin_b = jnp.where(even_lane, 0.0, sin)
    pad = lambda tab, fill: jnp.concatenate(
        [jnp.tile(tab, (1, LANES // HEAD_DIM)), jnp.full((tm, LANES), fill, F32)], axis=0)
    return pad(cos, 1.0), pad(sin_a, 0.0), pad(sin_b, 0.0)


def _block_diag(width, value, dtype):
    h = np.arange(width) // HEAD_DIM
    return jnp.asarray((h[:, None] == h[None, :]) * value, dtype)


def _pad_lora(w):
    r = w.shape[1]
    z = jnp.zeros_like(w[0])
    return jnp.stack([jnp.concatenate([w[0], z], axis=0), jnp.concatenate([z, w[1]], axis=0)])


def _largest_tile(limit, *sizes):
    t = limit
    while any(s % t for s in sizes):
        t //= 2
    return t


def kernel(x, c, ctx, c_ctx, w_mod, b_mod, ffn_in, ffn_out, even_w_in, even_w_out, q_gain, k_gain, rwkv_mu, rwkv_w0,
           rwkv_w2, rwkv_a0, rwkv_a2, rwkv_g2, rwkv_k_k, rwkv_k_a, rwkv_r_k, rwkv_gn_w, rwkv_gn_b, odd_w_in, odd_w_out,
           sink, final_gain):
    nb, n, d = x.shape
    lc = ctx.shape[1]
    depth = w_mod.shape[0]
    dff = ffn_out.shape[2]
    assert n % TQ == 0 and lc % TQ == 0 and n >= TQ + 2 * WINDOW and n % GRID_W == 0
    scan_rows = SCAN_STEP_CHUNKS * SCAN_CHUNK
    assert n % scan_rows == 0 and lc % scan_rows == 0 and nb + 1 <= MOD_ROWS
    dims = {
        "b": nb, "n": n, "l": lc,
        "tm_ffn": _largest_tile(512, n, nb * lc),
        "tm_proj": _largest_tile(512, n, nb * lc),
        "tm_feat": _largest_tile(256, n, lc),
        "key_chunk": _largest_tile(256, n),
        "tq_global": _largest_tile(256, n, lc),
    }

    cvec = jnp.zeros((MOD_ROWS, d), F32).at[:nb].set(c).at[nb].set(c_ctx)
    mods = _mod_table(cvec, w_mod, b_mod).reshape(depth, MOD_ROWS, N_MOD, d)

    ffn_in_h = ffn_in.astype(BF16)
    ffn_out_h = ffn_out.astype(BF16)
    even_in_h = even_w_in.astype(BF16)
    even_out_h = even_w_out.astype(BF16)
    odd_in_h = odd_w_in.astype(BF16)
    odd_out_h = odd_w_out.astype(BF16)

    rope_tabs = _rope_tables(n, dims["tm_proj"])
    a_kw = A_KV_HEADS * HEAD_DIM
    a_qw = A_HEADS * HEAD_DIM
    bd_q = _block_diag(a_qw, 1.0 / HEAD_DIM, BF16)
    bd_k = _block_diag(a_kw, 1.0 / HEAD_DIM, BF16)
    bd_ones = _block_diag(B_WIDTH, 1.0, BF16)
    fw = even_w_in.shape[2] - a_qw - 2 * a_kw

    xs = x.reshape(nb * n, d)
    for l in range(depth):
        last = l == depth - 1
        xs = _ffn(xs, mods, ffn_in_h, ffn_out_h, l, 0, 0, dims, ctx_rows=ctx.reshape(nb * lc, d) if l == 0 else None)
        if l % 2 == 0:
            e = l // 2
            tile_h = lambda g, reps: jnp.tile(g, reps).reshape(1, -1)
            norm_args = (tile_h(q_gain[e], A_HEADS), tile_h(k_gain[e], A_KV_HEADS), bd_q, bd_k)
            q, k, v, f = _proj(xs, mods, even_in_h, l, e, rope_tabs, dims, a_qw, a_kw, fw, norm_args)
            oa = _attention(q, k, v, dims, ctx_queries=not last)
            feat_params = (rwkv_mu[e], rwkv_w0[e], _pad_lora(rwkv_w2[e]).astype(BF16), rwkv_a0[e],
                           _pad_lora(rwkv_a2[e]).astype(BF16), rwkv_g2[e].astype(BF16),
                           rwkv_k_k[e].reshape(1, -1), rwkv_k_a[e].reshape(1, -1), bd_ones)
            r, kr, vr, kk, g, lw0, lw1, kd0, kd1, b0, b1 = _rwkv_features(f, feat_params, dims)
            yf, yb = _wkv_scan((r, vr, kk, lw0, lw1, kd0, kd1, b0, b1), dims)
            readout = (yf, yb, r, kr, vr, g, rwkv_r_k[e].reshape(1, -1), rwkv_gn_w[e].reshape(1, -1),
                       rwkv_gn_b[e].reshape(1, -1), bd_ones)
            mixer = (even_out_h, e, oa, readout)
        else:
            o = l // 2
            c_kw = C_KV_HEADS * HEAD_DIM
            c_qw = C_HEADS * HEAD_DIM
            q, k, v = _proj(xs, mods, odd_in_h, l, o, rope_tabs, dims, c_qw, c_kw, 0)
            oc = _attention(q, k, v, dims, sink=sink[o], ctx_queries=not last)
            mixer = (odd_out_h, o, oc, None)
        xs = _ffn(xs, mods, ffn_in_h, ffn_out_h, l, 1, 6, dims, final_gain=final_gain if last else None, mixer=mixer)
    return xs.reshape(nb, n, d)
```

```python
import functools

import jax
import jax.numpy as jnp
import numpy as np
from jax import lax
from jax.experimental import pallas as pl
from jax.experimental.pallas import tpu as pltpu

F32 = jnp.float32
BF16 = jnp.bfloat16

HEAD_DIM = 64
GRID_W = 64
A_HEADS, A_KV_HEADS = 8, 2
B_HEADS = 8
B_WIDTH = B_HEADS * HEAD_DIM
DECAY_LORA, ICLR_LORA, GATE_LORA = 64, 64, 128
C_HEADS, C_KV_HEADS = 16, 4
GQA_GROUP = 4
WINDOW = 128
N_MOD = 9
ROPE_THETA = 10000.0
EPS = 1e-6
GN_EPS = 64e-5
NEG_BIG = -1e30
LOG2_E = 1.4426950408889634
EXP_NEG_HALF = 0.6065306597126334

LANES = 128
SUBLANES = 8
VMEM_LIMIT_BYTES = 56 * 1024 * 1024

SCAN_CHUNK = 64
SCAN_STEP_CHUNKS = 2
PAIR_ROWS = 2 * SCAN_CHUNK
TQ = 128
AHEAD = 4
ONES_ROWS = 16
MOD_ROWS = 16


def _cparams(sem):
    return pltpu.CompilerParams(dimension_semantics=sem, vmem_limit_bytes=VMEM_LIMIT_BYTES)


def _dot(a, b, precision=None):
    return jnp.dot(a, b, preferred_element_type=F32, precision=precision)


def _dot_nt(a, b, precision=None):
    return lax.dot_general(a, b, (((1,), (1,)), ((), ())), preferred_element_type=F32, precision=precision)


def _dot_tn(a, b, precision=None):
    return lax.dot_general(a, b, (((0,), (0,)), ((), ())), preferred_element_type=F32, precision=precision)


def _rms(x):
    return x * lax.rsqrt(jnp.mean(x * x, axis=-1, keepdims=True) + EPS)


def _sigmoid(x):
    return 1.0 / (1.0 + jnp.exp(-x))


def _headsum(x, bd, split=False):
    hi = x.astype(BF16)
    if not split:
        return _dot(hi, bd)
    lo = (x - hi.astype(F32)).astype(BF16)
    return _dot(hi, bd) + _dot(lo, bd)


def _mod_kernel(c_ref, w_ref, b_ref, o_ref):
    c = c_ref[...]
    s = c * _sigmoid(c)
    hi = s.astype(BF16)
    lo = (s - hi.astype(F32)).astype(BF16)
    w = w_ref[...].astype(BF16)
    o_ref[...] = _dot(hi, w) + _dot(lo, w) + b_ref[...]


def _mod_table(cvec, w_mod, b_mod):
    depth, d, nd = w_mod.shape
    tn = 1536 if nd % 1536 == 0 else nd
    return pl.pallas_call(
        _mod_kernel,
        grid=(depth, nd // tn),
        in_specs=[
            pl.BlockSpec((MOD_ROWS, d), lambda l, n: (0, 0)),
            pl.BlockSpec((None, d, tn), lambda l, n: (l, 0, n)),
            pl.BlockSpec((None, 1, tn), lambda l, n: (l, 0, n)),
        ],
        out_specs=pl.BlockSpec((None, MOD_ROWS, tn), lambda l, n: (l, 0, n)),
        out_shape=jax.ShapeDtypeStruct((depth, MOD_ROWS, nd), F32),
        compiler_params=_cparams(("parallel", "parallel")),
        name="mod_table",
    )(cvec, w_mod, b_mod.reshape(depth, 1, nd))


def _mixer_out(oa_ref, wm_ref, readout):
    if readout is None:
        return _dot(oa_ref[...], wm_ref[...])
    yf, yb, r, k, v, g, rk, gnw, gnb, bd = readout
    y = yf[...] + yb[...]
    mean = _headsum(y, bd[...], split=True) * (1.0 / HEAD_DIM)
    yc = y - mean
    var = _headsum(yc * yc, bd[...]) * (1.0 / HEAD_DIM)
    yn = yc * lax.rsqrt(var + GN_EPS) * gnw[...] + gnb[...]
    up = lambda ref: ref[...].astype(F32)
    bonus = _headsum(up(r) * up(k) * rk[...], bd[...]) * up(v)
    ob = ((yn + bonus) * up(g)).astype(BF16)
    half = oa_ref.shape[1]
    return _dot(oa_ref[...], wm_ref[:half, :]) + _dot(ob, wm_ref[half:, :])


def _ffn_kernel(x_ref, mod_ref, wg_ref, wu_ref, wo_ref, *rest, i_shift, final, n_lat_tiles, mixer):
    o_ref = rest[-1]
    ctx_ref = None
    if n_lat_tiles is not None:
        ctx_ref, rest = rest[0], rest[1:]
    x = x_ref[...]
    if ctx_ref is not None:
        x = jnp.where(pl.program_id(0) < n_lat_tiles, x, ctx_ref[...])
    if mixer is not None:
        readout = rest[2:12] if mixer == "even" else None
        x = x + mod_ref[5:6, :] * _mixer_out(rest[1], rest[0], readout)
    half = x.shape[0] // 2
    shift, scale = mod_ref[i_shift:i_shift + 1, :], 1.0 + mod_ref[i_shift + 1:i_shift + 2, :]
    gate = 0.5 * mod_ref[i_shift + 2:i_shift + 3, :]
    xs, gu = [x[:half], x[half:]], []
    for xh in xs:
        h = (_rms(xh) * scale + shift).astype(BF16)
        gu.append((_dot(h, wg_ref[...]), _dot(h, wu_ref[...])))
    for idx, (x, (g, u)) in enumerate(zip(xs, gu)):
        a = (g * _sigmoid(g) * u).astype(BF16)
        y = x + gate * _dot(a, wo_ref[...])
        if final:
            y = _rms(y) * rest[-2][...]
        o_ref[idx * half:(idx + 1) * half, :] = y


def _ffn(xs, mods, w_in, w_out, l, j, i_shift, dims, final_gain=None, ctx_rows=None, mixer=None):
    rows, d = xs.shape
    dff = w_out.shape[2]
    tm = dims["tm_ffn"]
    tiles_per_batch = dims["n"] // tm
    nb = dims["b"]
    n_lat_tiles = nb * tiles_per_batch
    final = final_gain is not None
    if final:
        rows = nb * dims["n"]
    split = ctx_rows is not None
    if split:
        rows = rows + ctx_rows.shape[0]
    in_specs = [
        pl.BlockSpec((tm, d), (lambda i: (jnp.minimum(i, n_lat_tiles - 1), 0)) if split else (lambda i: (i, 0))),
        pl.BlockSpec((None, None, N_MOD, d), lambda i: (l, jnp.minimum(i // tiles_per_batch, nb), 0, 0)),
        pl.BlockSpec((None, None, d, dff), lambda i: (l, j, 0, 0)),
        pl.BlockSpec((None, None, d, dff), lambda i: (l, j, 0, 1)),
        pl.BlockSpec((None, None, dff, d), lambda i: (l, j, 0, 0)),
    ]
    args = [xs, mods, w_in, w_in, w_out]
    if split:
        in_specs.append(pl.BlockSpec((tm, d), lambda i: (jnp.maximum(i - n_lat_tiles, 0), 0)))
        args.append(ctx_rows)
    mixer_kind = None
    if mixer is not None:
        w_mix, e, o_att, readout = mixer
        mixer_kind = "odd" if readout is None else "even"
        in_specs += [pl.BlockSpec((None, d, d), lambda i: (e, 0, 0)),
                     pl.BlockSpec((tm, o_att.shape[1]), lambda i: (i, 0))]
        args += [w_mix, o_att]
        if readout is not None:
            bw = B_WIDTH
            in_specs += [pl.BlockSpec((tm, bw), lambda i: (i, 0))] * 6
            in_specs += [pl.BlockSpec((1, bw), lambda i: (0, 0))] * 3 + [pl.BlockSpec((bw, bw), lambda i: (0, 0))]
            args += list(readout)
    if final:
        in_specs.append(pl.BlockSpec((1, d), lambda i: (0, 0)))
        args.append(final_gain.reshape(1, d))
    return pl.pallas_call(
        functools.partial(_ffn_kernel, i_shift=i_shift, final=final, n_lat_tiles=n_lat_tiles if split else None,
                          mixer=mixer_kind),
        grid=(rows // tm,),
        in_specs=in_specs,
        out_specs=pl.BlockSpec((tm, d), lambda i: (i, 0)),
        out_shape=jax.ShapeDtypeStruct((rows, d), F32),
        compiler_params=_cparams(("parallel",)),
        name="ffn",
    )(*args)


def _rope(x, cos, sin_a, sin_b):
    outs = []
    for g in range(x.shape[1] // LANES):
        xg = x[:, g * LANES:(g + 1) * LANES]
        nxt = pltpu.roll(xg, LANES - 1, 1)
        prv = pltpu.roll(xg, 1, 1)
        outs.append(xg * cos + nxt * sin_a + prv * sin_b)
    return outs[0] if len(outs) == 1 else jnp.concatenate(outs, axis=1)


def _proj_kernel(x_ref, mod_ref, w_ref, cos_ref, sa_ref, sb_ref, *rest, qw, kw, fw, qk_norm):
    if qk_norm:
        qg_ref, kg_ref, bdq_ref, bdk_ref = rest[:4]
        rest = rest[4:]
    q_ref, k_ref, v_ref = rest[:3]
    half = x_ref.shape[0] // 2
    halves = [slice(0, half), slice(half, 2 * half)]
    hn, qkv = [], []
    for rows in halves:
        h = (_rms(x_ref[rows, :]) * (1.0 + mod_ref[4:5, :]) + mod_ref[3:4, :]).astype(BF16)
        hn.append(h)
        kv = _dot(h, w_ref[:, qw:qw + 2 * kw])
        qkv.append((_dot(h, w_ref[:, :qw]), kv[:, :kw], kv[:, kw:]))
    for rows, h, (q, k, v) in zip(halves, hn, qkv):
        if qk_norm:
            q = q * lax.rsqrt(_headsum(q * q, bdq_ref[...]) + EPS) * qg_ref[...]
            k = k * lax.rsqrt(_headsum(k * k, bdk_ref[...]) + EPS) * kg_ref[...]
        cos, sa, sb = cos_ref[rows, :], sa_ref[rows, :], sb_ref[rows, :]
        q_ref[:, rows] = (_rope(q, cos, sa, sb) * (HEAD_DIM ** -0.5 * LOG2_E)).T.astype(BF16)
        k_ref[rows, :] = _rope(k, cos, sa, sb).astype(BF16)
        v_ref[:, rows] = v.T.astype(BF16)
        if fw:
            rest[3][rows, :] = _dot(h, w_ref[:, qw + 2 * kw:])


def _proj(xs, mods, w_in, l, e, rope_tabs, dims, qw, kw, fw, norm_args=None):
    rows, d = xs.shape
    tm = dims["tm_proj"]
    cols = qw + 2 * kw + fw
    tiles_per_batch = dims["n"] // tm
    n_lat_tiles = dims["b"] * tiles_per_batch
    nb = dims["b"]

    def tab_idx(i):
        return (jnp.where(i < n_lat_tiles, i % tiles_per_batch, tiles_per_batch), 0)

    in_specs = [
        pl.BlockSpec((tm, d), lambda i: (i, 0)),
        pl.BlockSpec((None, None, N_MOD, d), lambda i: (l, jnp.minimum(i // tiles_per_batch, nb), 0, 0)),
        pl.BlockSpec((None, d, cols), lambda i: (e, 0, 0)),
        pl.BlockSpec((tm, LANES), tab_idx),
        pl.BlockSpec((tm, LANES), tab_idx),
        pl.BlockSpec((tm, LANES), tab_idx),
    ]
    args = [xs, mods, w_in, *rope_tabs]
    qk_norm = norm_args is not None
    if qk_norm:
        qg, kg, bdq, bdk = norm_args
        in_specs += [
            pl.BlockSpec((1, qw), lambda i: (0, 0)),
            pl.BlockSpec((1, kw), lambda i: (0, 0)),
            pl.BlockSpec((qw, qw), lambda i: (0, 0)),
            pl.BlockSpec((kw, kw), lambda i: (0, 0)),
        ]
        args += [qg, kg, bdq, bdk]
    out_specs = [
        pl.BlockSpec((qw, tm), lambda i: (0, i)),
        pl.BlockSpec((tm, kw), lambda i: (i, 0)),
        pl.BlockSpec((kw, tm), lambda i: (0, i)),
    ]
    out_shape = [
        jax.ShapeDtypeStruct((qw, rows), BF16),
        jax.ShapeDtypeStruct((rows, kw), BF16),
        jax.ShapeDtypeStruct((kw, rows), BF16),
    ]
    if fw:
        out_specs.append(pl.BlockSpec((tm, fw), lambda i: (i, 0)))
        out_shape.append(jax.ShapeDtypeStruct((rows, fw), F32))
    return pl.pallas_call(
        functools.partial(_proj_kernel, qw=qw, kw=kw, fw=fw, qk_norm=qk_norm),
        grid=(rows // tm,),
        in_specs=in_specs,
        out_specs=out_specs,
        out_shape=out_shape,
        compiler_params=_cparams(("parallel",)),
        name="mixer_proj",
    )(*args)


def _attend(qt_ref, segs, sink_ref, n_pad, o_ref):
    tq = qt_ref.shape[1]
    cols = GQA_GROUP * tq
    heads = range(qt_ref.shape[0] // (GQA_GROUP * HEAD_DIM))
    pair = lambda h: slice((h // 2) * LANES, (h // 2 + 1) * LANES)
    qpad, m, acc = [], [], []
    for h in heads:
        q4 = jnp.concatenate([qt_ref[(h * GQA_GROUP + g) * HEAD_DIM:(h * GQA_GROUP + g + 1) * HEAD_DIM, :]
                              for g in range(GQA_GROUP)], axis=1)
        z = jnp.zeros_like(q4)
        qpad.append(jnp.concatenate([q4, z] if h % 2 == 0 else [z, q4], axis=0))
        if sink_ref is not None:
            mh, lh = sink_ref[h], jnp.ones((1, cols), F32)
        else:
            mh, lh = jnp.full((1, cols), NEG_BIG, F32), jnp.zeros((1, cols), F32)
        if n_pad is not None:
            m_new = jnp.maximum(mh, jnp.where(n_pad > 0.0, 0.0, NEG_BIG))
            lh = lh * jnp.exp2(mh - m_new) + jnp.where(n_pad > 0.0, n_pad * jnp.exp2(-m_new), 0.0)
            mh = m_new
        m.append(mh)
        acc.append(jnp.concatenate([jnp.zeros((HEAD_DIM, cols), F32), jnp.broadcast_to(lh, (ONES_ROWS, cols))], axis=0))
    def scores(item):
        (k, _, valid), h = item
        s = _dot(k[:, pair(h)], qpad[h])
        if valid is not None:
            s = jnp.where(jnp.concatenate([valid] * GQA_GROUP, axis=1), s, NEG_BIG)
        return s

    items = [(seg, h) for seg in segs for h in heads]
    pending = [scores(it) for it in items[:AHEAD]]
    for idx, ((_, vt, _), h) in enumerate(items):
        s = pending.pop(0)
        if idx + AHEAD < len(items):
            pending.append(scores(items[idx + AHEAD]))
        m_new = jnp.maximum(m[h], jnp.max(s, axis=0, keepdims=True))
        alpha = jnp.exp2(m[h] - m_new)
        p = jnp.exp2(s - m_new).astype(BF16)
        v1 =jnp.concatenate([vt[h * HEAD_DIM:(h + 1) * HEAD_DIM, :], jnp.ones((ONES_ROWS, vt.shape[1]), BF16)], axis=0)
        acc[h] = alpha * acc[h] + _dot(v1, p)
        m[h] = m_new
    blocks = []
    for h in heads:
        ot = acc[h][:HEAD_DIM, :] * (1.0 / acc[h][HEAD_DIM:HEAD_DIM + 1, :])
        for g in range(0, GQA_GROUP, 2):
            blocks.append(jnp.concatenate([ot[:, g * tq:(g + 1) * tq], ot[:, (g + 1) * tq:(g + 2) * tq]], axis=0).T)
    o_ref[...] = jnp.concatenate(blocks, axis=1).astype(o_ref.dtype)


def _attn_global_kernel(qt_ref, kl_ref, vl_ref, kc_ref, vc_ref, o_ref, *, n_q_lat, key_chunk):
    i = pl.program_id(1)
    ctx_seg = (kc_ref[...], vc_ref[...], None)

    @pl.when(i < n_q_lat)
    def _():
        segs = [(kl_ref[c * key_chunk:(c + 1) * key_chunk, :], vl_ref[:, c * key_chunk:(c + 1) * key_chunk], None)
                for c in range(kl_ref.shape[0] // key_chunk)]
        _attend(qt_ref, segs + [ctx_seg], None, None, o_ref)

    @pl.when(i >= n_q_lat)
    def _():
        _attend(qt_ref, [ctx_seg], None, None, o_ref)


def _attn_window_kernel(qt_ref, k0_ref, k1_ref, k2_ref, v0_ref, v1_ref, v2_ref, kc_ref, vc_ref, sink_ref, o_ref,
                        *, n, n_q_lat):
    i = pl.program_id(1)
    tq = qt_ref.shape[1]
    ctx_seg = (kc_ref[...], vc_ref[...], None)

    @pl.when(i < n_q_lat)
    def _():
        k = jnp.concatenate([k0_ref[...], k1_ref[...], k2_ref[...]], axis=0)
        vt = jnp.concatenate([v0_ref[...], v1_ref[...], v2_ref[...]], axis=1)
        kpos = (i - 1) * tq + lax.broadcasted_iota(jnp.int32, (3 * tq, tq), 0)
        qpos = i * tq + lax.broadcasted_iota(jnp.int32, (3 * tq, tq), 1)
        valid = (jnp.abs(kpos - qpos) <= WINDOW) & (kpos >= WINDOW) & (kpos < n)
        q_row = i * tq + lax.broadcasted_iota(jnp.int32, (1, tq), 1)
        n_pad = jnp.maximum(q_row + WINDOW - n + 1, 0).astype(F32)
        n_pad = jnp.concatenate([n_pad] * GQA_GROUP, axis=1)
        _attend(qt_ref, [(k, vt, valid), ctx_seg], sink_ref, n_pad, o_ref)

    @pl.when(i >= n_q_lat)
    def _():
        _attend(qt_ref, [ctx_seg], sink_ref, None, o_ref)


def _attention(qt, k, vt, dims, sink=None, ctx_queries=True):
    qw, rows = qt.shape
    kw = k.shape[1]
    n, lc, nb = dims["n"], dims["l"], dims["b"]
    tq = TQ if sink is not None else dims["tq_global"]
    n_q_lat = n // tq
    n_q_ctx = lc // tq
    lat_blocks = nb * n_q_lat
    ctx_block = nb * n // lc

    def q_blk(b, i):
        return jnp.where(i < n_q_lat, b * n_q_lat + i, lat_blocks + b * n_q_ctx + (i - n_q_lat))

    ctx_specs = [pl.BlockSpec((lc, kw), lambda b, i: (ctx_block + b, 0)),
                 pl.BlockSpec((kw, lc), lambda b, i: (0, ctx_block + b))]
    qt_spec = pl.BlockSpec((qw, tq), lambda b, i: (0, q_blk(b, i)))
    if sink is None:
        body = functools.partial(_attn_global_kernel, n_q_lat=n_q_lat, key_chunk=dims["key_chunk"])
        in_specs = [qt_spec,
                    pl.BlockSpec((n, kw), lambda b, i: (b, 0)),
                    pl.BlockSpec((kw, n), lambda b, i: (0, b))] + ctx_specs
        args = [qt, k, vt, k, vt]
    else:
        assert WINDOW == TQ
        body = functools.partial(_attn_window_kernel, n=n, n_q_lat=n_q_lat)
        nbr = lambda b, i, off: b * n_q_lat + jnp.clip(i + off, 0, n_q_lat - 1)
        n_kv = kw // HEAD_DIM
        in_specs = ([qt_spec]
                    + [pl.BlockSpec((TQ, kw), functools.partial(lambda b, i, off: (nbr(b, i, off), 0), off=off))
                       for off in (-1, 0, 1)]
                    + [pl.BlockSpec((kw, TQ), functools.partial(lambda b, i, off: (0, nbr(b, i, off)), off=off))
                       for off in (-1, 0, 1)]
                    + ctx_specs
                    + [pl.BlockSpec((n_kv, 1, GQA_GROUP * TQ), lambda b, i: (0, 0, 0))])
        sink_rows = jnp.repeat(sink.astype(F32).reshape(n_kv, 1, GQA_GROUP) * LOG2_E, TQ, axis=2)
        args = [qt, k, k, k, vt, vt, vt, k, vt, sink_rows]
    return pl.pallas_call(
        body,
        grid=(nb, n_q_lat + (n_q_ctx if ctx_queries else 0)),
        in_specs=in_specs,
        out_specs=pl.BlockSpec((tq, qw), lambda b, i: (q_blk(b, i), 0)),
        out_shape=jax.ShapeDtypeStruct((rows if ctx_queries else nb * n, qw), BF16),
        compiler_params=_cparams(("parallel", "arbitrary")),
        name="gqa_global" if sink is None else "gqa_window",
    )(*args)


def _feat_kernel(f_ref, fp_ref, fn_ref, mu_ref, w0_ref, w2_ref, a0_ref, a2_ref, g2_ref, kk_ref, ka_ref, bd_ref,
                 r_o, k_o, v_o, kk_o, g_o, lw0_o, lw1_o, kd0_o, kd1_o, b0_o, b1_o,
                 *, n_lat_tiles, tiles_lat, tiles_ctx):
    i = pl.program_id(0)
    t = f_ref.shape[0]
    bw = B_WIDTH
    is_lat = i < n_lat_tiles
    pos = jnp.where(is_lat, i % tiles_lat, (i - n_lat_tiles) % tiles_ctx)
    last = jnp.where(is_lat, tiles_lat, tiles_ctx) - 1
    f = f_ref[...]
    prow = jnp.where(pos == 0, 0.0, fp_ref[SUBLANES - 1:SUBLANES, :])
    nrow = jnp.where(pos == last, 0.0, fn_ref[0:1, :])
    rid = lax.broadcasted_iota(jnp.int32, (SUBLANES, 1), 0)
    prev = pltpu.roll(f, 1, 0)
    prev = jnp.concatenate([jnp.where(rid == 0, prow, prev[:SUBLANES]), prev[SUBLANES:]], axis=0)
    nxt = pltpu.roll(f, t - 1, 0)
    nxt = jnp.concatenate([nxt[:t - SUBLANES], jnp.where(rid == SUBLANES - 1, nrow, nxt[t - SUBLANES:])], axis=0)
    mu_p, mu_n = mu_ref[0:1, :], mu_ref[1:2, :]
    fs = f * (1.0 - mu_p - mu_n) + mu_p * prev + mu_n * nxt

    r = fs[:, :bw]
    k = fs[:, bw:2 * bw]
    v = fs[:, 2 * bw:3 * bw]
    wl = jnp.tanh(fs[:, 3 * bw:3 * bw + 2 * DECAY_LORA])
    al = fs[:, 3 * bw + 2 * DECAY_LORA:3 * bw + 2 * DECAY_LORA + 2 * ICLR_LORA]
    gl = fs[:, 3 * bw + 2 * DECAY_LORA + 2 * ICLR_LORA:]

    kk = k * kk_ref[...]
    kk = kk * lax.rsqrt(_headsum(kk * kk, bd_ref[...]) + EPS)
    r_o[...] = r.astype(BF16)
    k_o[...] = k.astype(BF16)
    v_o[...] = v.astype(BF16)
    kk_o[...] = kk.astype(BF16)
    g_o[...] = _dot(_sigmoid(gl).astype(BF16), g2_ref[...]).astype(BF16)
    wl = wl.astype(BF16)
    al = al.astype(BF16)
    for d, (lw_o, kd_o, b_o) in enumerate(((lw0_o, kd0_o, b0_o), (lw1_o, kd1_o, b1_o))):
        w_log = w0_ref[d:d + 1, :] + _dot(wl, w2_ref[d])
        lw_o[...] = -EXP_NEG_HALF * _sigmoid(w_log)
        a = _sigmoid(a0_ref[d:d + 1, :] + _dot(al, a2_ref[d]))
        kd_o[...] = (k * ((1.0 - ka_ref[...]) + a * ka_ref[...])).astype(BF16)
        b_o[...] = (kk * a).astype(BF16)


def _rwkv_features(f, params, dims):
    rows, fw = f.shape
    t = dims["tm_feat"]
    n, lc, nb = dims["n"], dims["l"], dims["b"]
    tiles_lat, tiles_ctx = n // t, lc // t
    n_lat_tiles = nb * tiles_lat
    hb = t // SUBLANES
    n_halo = rows // SUBLANES
    mu, w0, w2p, a0, a2p, g2, k_k, k_a, bd = params
    bw = B_WIDTH
    const = lambda shape: pl.BlockSpec(shape, lambda i: (0,) * len(shape))
    in_specs = [
        pl.BlockSpec((t, fw), lambda i: (i, 0)),
        pl.BlockSpec((SUBLANES, fw), lambda i: (jnp.maximum(i * hb - 1, 0), 0)),
        pl.BlockSpec((SUBLANES, fw), lambda i: (jnp.minimum((i + 1) * hb, n_halo - 1), 0)),
        const((2, fw)), const((2, bw)), const((2, 2 * DECAY_LORA, bw)), const((2, bw)),
        const((2, 2 * ICLR_LORA, bw)), const((GATE_LORA, bw)), const((1, bw)), const((1, bw)), const((bw, bw)),
    ]
    out = lambda dtype: jax.ShapeDtypeStruct((rows, bw), dtype)
    return pl.pallas_call(
        functools.partial(_feat_kernel, n_lat_tiles=n_lat_tiles, tiles_lat=tiles_lat, tiles_ctx=tiles_ctx),
        grid=(rows // t,),
        in_specs=in_specs,
        out_specs=[pl.BlockSpec((t, bw), lambda i: (i, 0))] * 11,
        out_shape=[out(BF16)] * 5 + [out(F32)] * 2 + [out(BF16)] * 4,
        compiler_params=_cparams(("parallel",)),
        name="rwkv_features",
    )(f, f, f, mu, w0, w2p, a0, a2p, g2, k_k, k_a, bd)


def _stack_heads(x, head0):
    return jnp.concatenate([jnp.where(head0, x, 0.0), jnp.where(head0, 0.0, x)], axis=0).astype(BF16)


def _cumsum_rows(tri, x):
    hi = x.astype(BF16)
    r1 = x - hi.astype(F32)
    mid = r1.astype(BF16)
    lo = (r1 - mid.astype(F32)).astype(BF16)
    return _dot(tri, hi) + _dot(tri, mid) + _dot(tri, lo)


def _wkv_units(units, eye, lvl_ref):
    c = units[0]["v"].shape[0]
    p2 = 2 * c
    head0 = lax.broadcasted_iota(jnp.int32, (1, LANES), 1) < HEAD_DIM
    st = lambda x: _stack_heads(x, head0)
    fold = lambda x: x[:c] + x[c:]
    each = lambda fn: [fn(u) for u in units]

    for u in units:
        u["kq_s"], u["rq_s"], u["v_s"] = st(u["kq"]), st(u["rq"]), st(u["v"])
    a = each(lambda u: _dot_nt(jnp.concatenate([u["kq_s"], u["rq_s"]], axis=0),
                               jnp.concatenate([st(u["bi"]), st(u["ki"])], axis=0)))
    for u, au in zip(units, a):
        u["nkb"] = jnp.where(u["m_strict"], au[:p2, :p2], 0.0)
        u["akk"] = jnp.where(u["m_strict"], au[:p2, p2:], 0.0).astype(BF16)
        u["arb"] = jnp.where(u["m_incl"], au[p2:, :p2], 0.0).astype(BF16)
        u["ark"] = jnp.where(u["m_incl"], au[p2:, p2:], 0.0).astype(BF16)

    nkb_h = each(lambda u: u["nkb"].astype(BF16))
    tinv = each(lambda u: eye - u["nkb"] * lvl_ref[0].astype(F32))
    for lev in range(1, lvl_ref.shape[0]):
        t_h = [t.astype(BF16) for t in tinv]
        lt = [_dot(nh * lvl_ref[lev], th).astype(BF16) for nh, th in zip(nkb_h, t_h)]
        tinv = [t - _dot(th, x) for t, th, x in zip(tinv, t_h, lt)]
    t_h = [t.astype(BF16) for t in tinv]

    akkv = each(lambda u: _dot(u["akk"], u["v_s"]).astype(BF16))
    gp_h = [_dot(th, jnp.concatenate([u["kq_s"], x], axis=1)).astype(BF16)
            for u, th, x in zip(units, t_h, akkv)]
    corr = [_dot(u["arb"], g) for u, g in zip(units, gp_h)]
    arkv = each(lambda u: _dot(u["ark"], u["v_s"]))
    btg = [_dot_tn(st(u["bt"]), g) for u, g in zip(units, gp_h)]
    ktv = each(lambda u: _dot_tn(st(u["kt"]), u["v_s"]))

    for u, cr, av, bg, kv in zip(units, corr, arkv, btg, ktv):
        rq = u["rq"]
        rq2 = fold(jnp.concatenate([jnp.where(head0, rq, 0.0), jnp.where(head0, 0.0, rq)], axis=0) - cr[:, :LANES])
        u["rq2"] = rq2.astype(BF16)
        u["yl"] = fold(av - cr[:, LANES:])
        u["m_state"] = (jnp.where(eye > 0.0, u["decay"], 0.0) - bg[:, :LANES]).astype(BF16)
        u["n_state"] = kv - bg[:, LANES:]


def _scan_kernel(rf, vf, kkf, lwf, kdf, bf, rb, vb, kkb, lwb, kdb, bb, tri_ref, msk_ref, lvl_ref,
                 yf_o, yb_o, hf_s, hb_s):
    @pl.when(pl.program_id(1) == 0)
    def _():
        hf_s[...] = jnp.zeros_like(hf_s)
        hb_s[...] = jnp.zeros_like(hb_s)

    c = SCAN_CHUNK
    eye = msk_ref[0]
    units, chains = [], []
    for d, (refs, y_o, h_s) in enumerate((((rf, vf, kkf, lwf, kdf, bf), yf_o, hf_s),
                                          ((rb, vb, kkb, lwb, kdb, bb), yb_o, hb_s))):
        m_strict = msk_ref[1 + 2 * d] > 0.0
        m_incl = msk_ref[2 + 2 * d] > 0.0
        r, v, kk, lw, kd, b = (ref[...] for ref in refs)
        cw = _cumsum_rows(tri_ref[d], lw)
        tot = [jnp.sum(lw[j * c:(j + 1) * c], axis=0, keepdims=True) for j in range(SCAN_STEP_CHUNKS)]
        tot_rows = jnp.concatenate([jnp.broadcast_to(t, (c, t.shape[1])) for t in tot], axis=0)
        w_inv = jnp.exp(-cw)
        w_rest = jnp.exp(tot_rows - cw)
        facs = {"kq": kk * jnp.exp(cw - lw), "rq": r * jnp.exp(cw), "bi": b * w_inv, "ki": kd * w_inv,
                "bt": b * w_rest, "kt": kd * w_rest, "v": v}
        order = range(SCAN_STEP_CHUNKS) if d == 0 else range(SCAN_STEP_CHUNKS - 1, -1, -1)
        for p in range(B_WIDTH // LANES):
            ls = slice(p * LANES, (p + 1) * LANES)
            chain = []
            for j in order:
                unit = {name: x[j * c:(j + 1) * c, ls] for name, x in facs.items()}
                unit.update(decay=jnp.exp(tot[j][:, ls]), m_strict=m_strict, m_incl=m_incl, rows=slice(j * c, (j + 1) * c))
                units.append(unit)
                chain.append(unit)
            chains.append((chain, y_o, h_s, p, ls))
    _wkv_units(units, eye, lvl_ref)
    states = [h_s[p] for _, _, h_s, p, _ in chains]
    for pos in range(SCAN_STEP_CHUNKS):
        for ci, (chain, y_o, _, _, ls) in enumerate(chains):
            u = chain[pos]
            h_h = states[ci].astype(BF16)
            y_o[u["rows"], ls] = _dot(u["rq2"], h_h) + u["yl"]
            states[ci] = _dot(u["m_state"], h_h) + u["n_state"]
    for h, (_, _, h_s, p, _) in zip(states, chains):
        h_s[p] = h


def _scan_constants():
    c, p2 = SCAN_CHUNK, PAIR_ROWS
    t = np.arange(SCAN_STEP_CHUNKS * c)
    same_chunk = (t[None, :] // c) == (t[:, None] // c)
    tri = np.stack([same_chunk & (t[None, :] <= t[:, None]), same_chunk & (t[None, :] >= t[:, None])]).astype(np.float32)
    i = np.arange(p2)
    same_head = (i[:, None] // c) == (i[None, :] // c)
    ti, si = i[:, None] % c, i[None, :] % c
    msk = np.stack([
        np.eye(p2, dtype=bool),
        same_head & (si < ti), same_head & (si <= ti),
        same_head & (si > ti), same_head & (si >= ti),
    ]).astype(np.float32)
    n_lev = int(np.log2(c))
    lvl = np.stack([
        ((i[:, None] >> (k + 1)) == (i[None, :] >> (k + 1))) & ((i[:, None] >> k) != (i[None, :] >> k))
        for k in range(n_lev)
    ]).astype(np.float32)
    return jnp.asarray(tri, BF16), jnp.asarray(msk), jnp.asarray(lvl, BF16)


def _wkv_scan(feats, dims):
    r, v, kk, lw0, lw1, kd0, kd1, b0, b1 = feats
    rows, bw = r.shape
    c = SCAN_STEP_CHUNKS * SCAN_CHUNK
    n, lc, nb = dims["n"], dims["l"], dims["b"]
    n_c, l_c = n // c, lc // c
    ctx_base = nb * n_c
    tri, msk, lvl = _scan_constants()

    def fwd(b, s):
        return (jnp.where(s < l_c, ctx_base + b * l_c + s, b * n_c + (s - l_c)), 0)

    def bwd(b, s):
        return (jnp.where(s < l_c, ctx_base + b * l_c + (l_c - 1 - s), b * n_c + (n_c - 1 - (s - l_c))), 0)

    const = lambda a: pl.BlockSpec(a.shape, lambda b, s: (0,) * a.ndim)
    out = jax.ShapeDtypeStruct((rows, bw), F32)
    return pl.pallas_call(
        _scan_kernel,
        grid=(nb, l_c + n_c),
        in_specs=[pl.BlockSpec((c, bw), fwd)] * 6 + [pl.BlockSpec((c, bw), bwd)] * 6
                 + [const(tri), const(msk), const(lvl)],
        out_specs=[pl.BlockSpec((c, bw), fwd), pl.BlockSpec((c, bw), bwd)],
        out_shape=[out, out],
        scratch_shapes=[pltpu.VMEM((bw // LANES, LANES, LANES), F32)] * 2,
        compiler_params=_cparams(("parallel", "arbitrary")),
        name="wkv_scan",
    )(r, v, kk, lw0, kd0, b0, r, v, kk, lw1, kd1, b1, tri, msk, lvl)


def _rope_tables(n, tm):
    rows = n // GRID_W
    row = jnp.repeat(jnp.arange(rows, dtype=F32), GRID_W)
    col = jnp.tile(jnp.arange(GRID_W, dtype=F32), rows)
    n_freq = HEAD_DIM // 4
    inv_freq = ROPE_THETA ** (-jnp.arange(n_freq, dtype=F32) / n_freq)
    ang = jnp.concatenate([row[:, None] * inv_freq, col[:, None] * inv_freq], axis=-1)
    cos = jnp.repeat(jnp.cos(ang), 2, axis=-1)
    sin = jnp.repeat(jnp.sin(ang), 2, axis=-1)
    even_lane = (jnp.arange(HEAD_DIM) % 2 == 0)[None, :]
    sin_a = jnp.where(even_lane, -sin, 0.0)
    sin_b = jnp.where(even_lane, 0.0, sin)
    pad = lambda tab, fill: jnp.concatenate(
        [jnp.tile(tab, (1, LANES // HEAD_DIM)), jnp.full((tm, LANES), fill, F32)], axis=0)
    return pad(cos, 1.0), pad(sin_a, 0.0), pad(sin_b, 0.0)


def _block_diag(width, value, dtype):
    h = np.arange(width) // HEAD_DIM
    return jnp.asarray((h[:, None] == h[None, :]) * value, dtype)


def _pad_lora(w):
    r = w.shape[1]
    z = jnp.zeros_like(w[0])
    return jnp.stack([jnp.concatenate([w[0], z], axis=0), jnp.concatenate([z, w[1]], axis=0)])


def _largest_tile(limit, *sizes):
    t = limit
    while any(s % t for s in sizes):
        t //= 2
    return t


def kernel(x, c, ctx, c_ctx, w_mod, b_mod, ffn_in, ffn_out, even_w_in, even_w_out, q_gain, k_gain, rwkv_mu, rwkv_w0,
           rwkv_w2, rwkv_a0, rwkv_a2, rwkv_g2, rwkv_k_k, rwkv_k_a, rwkv_r_k, rwkv_gn_w, rwkv_gn_b, odd_w_in, odd_w_out,
           sink, final_gain):
    nb, n, d = x.shape
    lc = ctx.shape[1]
    depth = w_mod.shape[0]
    dff = ffn_out.shape[2]
    assert n % TQ == 0 and lc % TQ == 0 and n >= TQ + 2 * WINDOW and n % GRID_W == 0
    scan_rows = SCAN_STEP_CHUNKS * SCAN_CHUNK
    assert n % scan_rows == 0 and lc % scan_rows == 0 and nb + 1 <= MOD_ROWS
    dims = {
        "b": nb, "n": n, "l": lc,
        "tm_ffn": _largest_tile(512, n, nb * lc),
        "tm_proj": _largest_tile(512, n, nb * lc),
        "tm_feat": _largest_tile(256, n, lc),
        "key_chunk": _largest_tile(256, n),
        "tq_global": _largest_tile(256, n, lc),
    }

    cvec = jnp.zeros((MOD_ROWS, d), F32).at[:nb].set(c).at[nb].set(c_ctx)
    mods = _mod_table(cvec, w_mod, b_mod).reshape(depth, MOD_ROWS, N_MOD, d)

    ffn_in_h = ffn_in.astype(BF16)
    ffn_out_h = ffn_out.astype(BF16)
    even_in_h = even_w_in.astype(BF16)
    even_out_h = even_w_out.astype(BF16)
    odd_in_h = odd_w_in.astype(BF16)
    odd_out_h = odd_w_out.astype(BF16)

    rope_tabs = _rope_tables(n, dims["tm_proj"])
    a_kw = A_KV_HEADS * HEAD_DIM
    a_qw = A_HEADS * HEAD_DIM
    bd_q = _block_diag(a_qw, 1.0 / HEAD_DIM, BF16)
    bd_k = _block_diag(a_kw, 1.0 / HEAD_DIM, BF16)
    bd_ones = _block_diag(B_WIDTH, 1.0, BF16)
    fw = even_w_in.shape[2] - a_qw - 2 * a_kw

    xs = x.reshape(nb * n, d)
    for l in range(depth):
        last = l == depth - 1
        xs = _ffn(xs, mods, ffn_in_h, ffn_out_h, l, 0, 0, dims, ctx_rows=ctx.reshape(nb * lc, d) if l == 0 else None)
        if l % 2 == 0:
            e = l // 2
            tile_h = lambda g, reps: jnp.tile(g, reps).reshape(1, -1)
            norm_args = (tile_h(q_gain[e], A_HEADS), tile_h(k_gain[e], A_KV_HEADS), bd_q, bd_k)
            q, k, v, f = _proj(xs, mods, even_in_h, l, e, rope_tabs, dims, a_qw, a_kw, fw, norm_args)
            oa = _attention(q, k, v, dims, ctx_queries=not last)
            feat_params = (rwkv_mu[e], rwkv_w0[e], _pad_lora(rwkv_w2[e]).astype(BF16), rwkv_a0[e],
                           _pad_lora(rwkv_a2[e]).astype(BF16), rwkv_g2[e].astype(BF16),
                           rwkv_k_k[e].reshape(1, -1), rwkv_k_a[e].reshape(1, -1), bd_ones)
            r, kr, vr, kk, g, lw0, lw1, kd0, kd1, b0, b1 = _rwkv_features(f, feat_params, dims)
            yf, yb = _wkv_scan((r, vr, kk, lw0, lw1, kd0, kd1, b0, b1), dims)
            readout = (yf, yb, r, kr, vr, g, rwkv_r_k[e].reshape(1, -1), rwkv_gn_w[e].reshape(1, -1),
                       rwkv_gn_b[e].reshape(1, -1), bd_ones)
            mixer = (even_out_h, e, oa, readout)
        else:
            o = l // 2
            c_kw = C_KV_HEADS * HEAD_DIM
            c_qw = C_HEADS * HEAD_DIM
            q, k, v = _proj(xs, mods, odd_in_h, l, o, rope_tabs, dims, c_qw, c_kw, 0)
            oc = _attention(q, k, v, dims, sink=sink[o], ctx_queries=not last)
            mixer = (odd_out_h, o, oc, None)
        xs = _ffn(xs, mods, ffn_in_h, ffn_out_h, l, 1, 6, dims, final_gain=final_gain if last else None, mixer=mixer)
    return xs.reshape(nb, n, d)
```

```python
import functools

import jax
import jax.numpy as jnp
import numpy as np
from jax import lax
from jax.experimental import pallas as pl
from jax.experimental.pallas import tpu as pltpu

F32 = jnp.float32
BF16 = jnp.bfloat16

HEAD_DIM = 64
GRID_W = 64
A_HEADS, A_KV_HEADS = 8, 2
B_HEADS = 8
B_WIDTH = B_HEADS * HEAD_DIM
DECAY_LORA, ICLR_LORA, GATE_LORA = 64, 64, 128
C_HEADS, C_KV_HEADS = 16, 4
GQA_GROUP = 4
WINDOW = 128
N_MOD = 9
ROPE_THETA = 10000.0
EPS = 1e-6
GN_EPS = 64e-5
NEG_BIG = -1e30
LOG2_E = 1.4426950408889634
EXP_NEG_HALF = 0.6065306597126334

LANES = 128
SUBLANES = 8
VMEM_LIMIT_BYTES = 56 * 1024 * 1024

SCAN_CHUNK = 64
SCAN_STEP_CHUNKS = 4
PAIR_ROWS = 2 * SCAN_CHUNK
TQ = 128
AHEAD = 4
ONES_ROWS = 16
MOD_ROWS = 16


def _cparams(sem):
    return pltpu.CompilerParams(dimension_semantics=sem, vmem_limit_bytes=VMEM_LIMIT_BYTES)


def _dot(a, b, precision=None):
    return jnp.dot(a, b, preferred_element_type=F32, precision=precision)


def _dot_nt(a, b, precision=None):
    return lax.dot_general(a, b, (((1,), (1,)), ((), ())), preferred_element_type=F32, precision=precision)


def _dot_tn(a, b, precision=None):
    return lax.dot_general(a, b, (((0,), (0,)), ((), ())), preferred_element_type=F32, precision=precision)


def _rms(x):
    return x * lax.rsqrt(jnp.mean(x * x, axis=-1, keepdims=True) + EPS)


def _sigmoid(x):
    return 1.0 / (1.0 + jnp.exp(-x))


def _headsum(x, bd, split=False):
    hi = x.astype(BF16)
    if not split:
        return _dot(hi, bd)
    lo = (x - hi.astype(F32)).astype(BF16)
    return _dot(hi, bd) + _dot(lo, bd)


def _mod_kernel(c_ref, w_ref, b_ref, o_ref):
    c = c_ref[...]
    s = c * _sigmoid(c)
    hi = s.astype(BF16)
    lo = (s - hi.astype(F32)).astype(BF16)
    w = w_ref[...].astype(BF16)
    o_ref[...] = _dot(hi, w) + _dot(lo, w) + b_ref[...]


def _mod_table(cvec, w_mod, b_mod):
    depth, d, nd = w_mod.shape
    tn = 1536 if nd % 1536 == 0 else nd
    return pl.pallas_call(
        _mod_kernel,
        grid=(depth, nd // tn),
        in_specs=[
            pl.BlockSpec((MOD_ROWS, d), lambda l, n: (0, 0)),
            pl.BlockSpec((None, d, tn), lambda l, n: (l, 0, n)),
            pl.BlockSpec((None, 1, tn), lambda l, n: (l, 0, n)),
        ],
        out_specs=pl.BlockSpec((None, MOD_ROWS, tn), lambda l, n: (l, 0, n)),
        out_shape=jax.ShapeDtypeStruct((depth, MOD_ROWS, nd), F32),
        compiler_params=_cparams(("parallel", "parallel")),
        name="mod_table",
    )(cvec, w_mod, b_mod.reshape(depth, 1, nd))


def _mixer_out(oa_ref, wm_ref, readout):
    if readout is None:
        return _dot(oa_ref[...], wm_ref[...])
    yf, yb, r, k, v, g, rk, gnw, gnb, bd = readout
    y = yf[...] + yb[...]
    mean = _headsum(y, bd[...], split=True) * (1.0 / HEAD_DIM)
    yc = y - mean
    var = _headsum(yc * yc, bd[...]) * (1.0 / HEAD_DIM)
    yn = yc * lax.rsqrt(var + GN_EPS) * gnw[...] + gnb[...]
    up = lambda ref: ref[...].astype(F32)
    bonus = _headsum(up(r) * up(k) * rk[...], bd[...]) * up(v)
    ob = ((yn + bonus) * up(g)).astype(BF16)
    half = oa_ref.shape[1]
    return _dot(oa_ref[...], wm_ref[:half, :]) + _dot(ob, wm_ref[half:, :])


def _ffn_kernel(x_ref, mod_ref, wg_ref, wu_ref, wo_ref, *rest, i_shift, final, n_lat_tiles, mixer):
    o_ref = rest[-1]
    ctx_ref = None
    if n_lat_tiles is not None:
        ctx_ref, rest = rest[0], rest[1:]
    x = x_ref[...]
    if ctx_ref is not None:
        x = jnp.where(pl.program_id(0) < n_lat_tiles, x, ctx_ref[...])
    if mixer is not None:
        readout = rest[2:12] if mixer == "even" else None
        x = x + mod_ref[5:6, :] * _mixer_out(rest[1], rest[0], readout)
    half = x.shape[0] // 2
    shift, scale = mod_ref[i_shift:i_shift + 1, :], 1.0 + mod_ref[i_shift + 1:i_shift + 2, :]
    gate = 0.5 * mod_ref[i_shift + 2:i_shift + 3, :]
    xs, gu = [x[:half], x[half:]], []
    for xh in xs:
        h = (_rms(xh) * scale + shift).astype(BF16)
        gu.append((_dot(h, wg_ref[...]), _dot(h, wu_ref[...])))
    for idx, (x, (g, u)) in enumerate(zip(xs, gu)):
        a = (g * _sigmoid(g) * u).astype(BF16)
        y = x + gate * _dot(a, wo_ref[...])
        if final:
            y = _rms(y) * rest[-2][...]
        o_ref[idx * half:(idx + 1) * half, :] = y


def _ffn(xs, mods, w_in, w_out, l, j, i_shift, dims, final_gain=None, ctx_rows=None, mixer=None):
    rows, d = xs.shape
    dff = w_out.shape[2]
    tm = dims["tm_ffn"]
    tiles_per_batch = dims["n"] // tm
    nb = dims["b"]
    n_lat_tiles = nb * tiles_per_batch
    final = final_gain is not None
    if final:
        rows = nb * dims["n"]
    split = ctx_rows is not None
    if split:
        rows = rows + ctx_rows.shape[0]
    in_specs = [
        pl.BlockSpec((tm, d), (lambda i: (jnp.minimum(i, n_lat_tiles - 1), 0)) if split else (lambda i: (i, 0))),
        pl.BlockSpec((None, None, N_MOD, d), lambda i: (l, jnp.minimum(i // tiles_per_batch, nb), 0, 0)),
        pl.BlockSpec((None, None, d, dff), lambda i: (l, j, 0, 0)),
        pl.BlockSpec((None, None, d, dff), lambda i: (l, j, 0, 1)),
        pl.BlockSpec((None, None, dff, d), lambda i: (l, j, 0, 0)),
    ]
    args = [xs, mods, w_in, w_in, w_out]
    if split:
        in_specs.append(pl.BlockSpec((tm, d), lambda i: (jnp.maximum(i - n_lat_tiles, 0), 0)))
        args.append(ctx_rows)
    mixer_kind = None
    if mixer is not None:
        w_mix, e, o_att, readout = mixer
        mixer_kind = "odd" if readout is None else "even"
        in_specs += [pl.BlockSpec((None, d, d), lambda i: (e, 0, 0)),
                     pl.BlockSpec((tm, o_att.shape[1]), lambda i: (i, 0))]
        args += [w_mix, o_att]
        if readout is not None:
            bw = B_WIDTH
            in_specs += [pl.BlockSpec((tm, bw), lambda i: (i, 0))] * 6
            in_specs += [pl.BlockSpec((1, bw), lambda i: (0, 0))] * 3 + [pl.BlockSpec((bw, bw), lambda i: (0, 0))]
            args += list(readout)
    if final:
        in_specs.append(pl.BlockSpec((1, d), lambda i: (0, 0)))
        args.append(final_gain.reshape(1, d))
    return pl.pallas_call(
        functools.partial(_ffn_kernel, i_shift=i_shift, final=final, n_lat_tiles=n_lat_tiles if split else None,
                          mixer=mixer_kind),
        grid=(rows // tm,),
        in_specs=in_specs,
        out_specs=pl.BlockSpec((tm, d), lambda i: (i, 0)),
        out_shape=jax.ShapeDtypeStruct((rows, d), F32),
        compiler_params=_cparams(("parallel",)),
        name="ffn",
    )(*args)


def _rope(x, cos, sin_a, sin_b):
    outs = []
    for g in range(x.shape[1] // LANES):
        xg = x[:, g * LANES:(g + 1) * LANES]
        nxt = pltpu.roll(xg, LANES - 1, 1)
        prv = pltpu.roll(xg, 1, 1)
        outs.append(xg * cos + nxt * sin_a + prv * sin_b)
    return outs[0] if len(outs) == 1 else jnp.concatenate(outs, axis=1)


def _proj_kernel(x_ref, mod_ref, w_ref, cos_ref, sa_ref, sb_ref, *rest, qw, kw, fw, qk_norm):
    if qk_norm:
        qg_ref, kg_ref, bdq_ref, bdk_ref = rest[:4]
        rest = rest[4:]
    q_ref, k_ref, v_ref = rest[:3]
    half = x_ref.shape[0] // 2
    halves = [slice(0, half), slice(half, 2 * half)]
    hn, qkv = [], []
    for rows in halves:
        h = (_rms(x_ref[rows, :]) * (1.0 + mod_ref[4:5, :]) + mod_ref[3:4, :]).astype(BF16)
        hn.append(h)
        kv = _dot(h, w_ref[:, qw:qw + 2 * kw])
        qkv.append((_dot(h, w_ref[:, :qw]), kv[:, :kw], kv[:, kw:]))
    for rows, h, (q, k, v) in zip(halves, hn, qkv):
        if qk_norm:
            q = q * lax.rsqrt(_headsum(q * q, bdq_ref[...]) + EPS) * qg_ref[...]
            k = k * lax.rsqrt(_headsum(k * k, bdk_ref[...]) + EPS) * kg_ref[...]
        cos, sa, sb = cos_ref[rows, :], sa_ref[rows, :], sb_ref[rows, :]
        q_ref[:, rows] = (_rope(q, cos, sa, sb) * (HEAD_DIM ** -0.5 * LOG2_E)).T.astype(BF16)
        k_ref[rows, :] = _rope(k, cos, sa, sb).astype(BF16)
        v_ref[:, rows] = v.T.astype(BF16)
        if fw:
            rest[3][rows, :] = _dot(h, w_ref[:, qw + 2 * kw:])


def _proj(xs, mods, w_in, l, e, rope_tabs, dims, qw, kw, fw, norm_args=None):
    rows, d = xs.shape
    tm = dims["tm_proj"]
    cols = qw + 2 * kw + fw
    tiles_per_batch = dims["n"] // tm
    n_lat_tiles = dims["b"] * tiles_per_batch
    nb = dims["b"]

    def tab_idx(i):
        return (jnp.where(i < n_lat_tiles, i % tiles_per_batch, tiles_per_batch), 0)

    in_specs = [
        pl.BlockSpec((tm, d), lambda i: (i, 0)),
        pl.BlockSpec((None, None, N_MOD, d), lambda i: (l, jnp.minimum(i // tiles_per_batch, nb), 0, 0)),
        pl.BlockSpec((None, d, cols), lambda i: (e, 0, 0)),
        pl.BlockSpec((tm, LANES), tab_idx),
        pl.BlockSpec((tm, LANES), tab_idx),
        pl.BlockSpec((tm, LANES), tab_idx),
    ]
    args = [xs, mods, w_in, *rope_tabs]
    qk_norm = norm_args is not None
    if qk_norm:
        qg, kg, bdq, bdk = norm_args
        in_specs += [
            pl.BlockSpec((1, qw), lambda i: (0, 0)),
            pl.BlockSpec((1, kw), lambda i: (0, 0)),
            pl.BlockSpec((qw, qw), lambda i: (0, 0)),
            pl.BlockSpec((kw, kw), lambda i: (0, 0)),
        ]
        args += [qg, kg, bdq, bdk]
    out_specs = [
        pl.BlockSpec((qw, tm), lambda i: (0, i)),
        pl.BlockSpec((tm, kw), lambda i: (i, 0)),
        pl.BlockSpec((kw, tm), lambda i: (0, i)),
    ]
    out_shape = [
        jax.ShapeDtypeStruct((qw, rows), BF16),
        jax.ShapeDtypeStruct((rows, kw), BF16),
        jax.ShapeDtypeStruct((kw, rows), BF16),
    ]
    if fw:
        out_specs.append(pl.BlockSpec((tm, fw), lambda i: (i, 0)))
        out_shape.append(jax.ShapeDtypeStruct((rows, fw), F32))
    return pl.pallas_call(
        functools.partial(_proj_kernel, qw=qw, kw=kw, fw=fw, qk_norm=qk_norm),
        grid=(rows // tm,),
        in_specs=in_specs,
        out_specs=out_specs,
        out_shape=out_shape,
        compiler_params=_cparams(("parallel",)),
        name="mixer_proj",
    )(*args)


def _attend(qt_ref, segs, sink_ref, n_pad, o_ref):
    tq = qt_ref.shape[1]
    cols = GQA_GROUP * tq
    heads = range(qt_ref.shape[0] // (GQA_GROUP * HEAD_DIM))
    pair = lambda h: slice((h // 2) * LANES, (h // 2 + 1) * LANES)
    qpad, m, acc = [], [], []
    for h in heads:
        q4 = jnp.concatenate([qt_ref[(h * GQA_GROUP + g) * HEAD_DIM:(h * GQA_GROUP + g + 1) * HEAD_DIM, :]
                              for g in range(GQA_GROUP)], axis=1)
        z = jnp.zeros_like(q4)
        qpad.append(jnp.concatenate([q4, z] if h % 2 == 0 else [z, q4], axis=0))
        if sink_ref is not None:
            mh, lh = sink_ref[h], jnp.ones((1, cols), F32)
        else:
            mh, lh = jnp.full((1, cols), NEG_BIG, F32), jnp.zeros((1, cols), F32)
        if n_pad is not None:
            m_new = jnp.maximum(mh, jnp.where(n_pad > 0.0, 0.0, NEG_BIG))
            lh = lh * jnp.exp2(mh - m_new) + jnp.where(n_pad > 0.0, n_pad * jnp.exp2(-m_new), 0.0)
            mh = m_new
        m.append(mh)
        acc.append(jnp.concatenate([jnp.zeros((HEAD_DIM, cols), F32), jnp.broadcast_to(lh, (ONES_ROWS, cols))], axis=0))
    def scores(item):
        (k, _, valid), h = item
        s = _dot(k[:, pair(h)], qpad[h])
        if valid is not None:
            s = jnp.where(jnp.concatenate([valid] * GQA_GROUP, axis=1), s, NEG_BIG)
        return s

    items = [(seg, h) for seg in segs for h in heads]
    pending = [scores(it) for it in items[:AHEAD]]
    for idx, ((_, vt, _), h) in enumerate(items):
        s = pending.pop(0)
        if idx + AHEAD < len(items):
            pending.append(scores(items[idx + AHEAD]))
        m_new = jnp.maximum(m[h], jnp.max(s, axis=0, keepdims=True))
        alpha = jnp.exp2(m[h] - m_new)
        p = jnp.exp2(s - m_new).astype(BF16)
        v1 =jnp.concatenate([vt[h * HEAD_DIM:(h + 1) * HEAD_DIM, :], jnp.ones((ONES_ROWS, vt.shape[1]), BF16)], axis=0)
        acc[h] = alpha * acc[h] + _dot(v1, p)
        m[h] = m_new
    blocks = []
    for h in heads:
        ot = acc[h][:HEAD_DIM, :] * (1.0 / acc[h][HEAD_DIM:HEAD_DIM + 1, :])
        for g in range(0, GQA_GROUP, 2):
            blocks.append(jnp.concatenate([ot[:, g * tq:(g + 1) * tq], ot[:, (g + 1) * tq:(g + 2) * tq]], axis=0).T)
    o_ref[...] = jnp.concatenate(blocks, axis=1).astype(o_ref.dtype)


def _attn_global_kernel(qt_ref, kl_ref, vl_ref, kc_ref, vc_ref, o_ref, *, n_q_lat, key_chunk):
    i = pl.program_id(1)
    ctx_seg = (kc_ref[...], vc_ref[...], None)

    @pl.when(i < n_q_lat)
    def _():
        segs = [(kl_ref[c * key_chunk:(c + 1) * key_chunk, :], vl_ref[:, c * key_chunk:(c + 1) * key_chunk], None)
                for c in range(kl_ref.shape[0] // key_chunk)]
        _attend(qt_ref, segs + [ctx_seg], None, None, o_ref)

    @pl.when(i >= n_q_lat)
    def _():
        _attend(qt_ref, [ctx_seg], None, None, o_ref)


def _attn_window_kernel(qt_ref, k0_ref, k1_ref, k2_ref, v0_ref, v1_ref, v2_ref, kc_ref, vc_ref, sink_ref, o_ref,
                        *, n, n_q_lat):
    i = pl.program_id(1)
    tq = qt_ref.shape[1]
    ctx_seg = (kc_ref[...], vc_ref[...], None)

    @pl.when(i < n_q_lat)
    def _():
        k = jnp.concatenate([k0_ref[...], k1_ref[...], k2_ref[...]], axis=0)
        vt = jnp.concatenate([v0_ref[...], v1_ref[...], v2_ref[...]], axis=1)
        kpos = (i - 1) * tq + lax.broadcasted_iota(jnp.int32, (3 * tq, tq), 0)
        qpos = i * tq + lax.broadcasted_iota(jnp.int32, (3 * tq, tq), 1)
        valid = (jnp.abs(kpos - qpos) <= WINDOW) & (kpos >= WINDOW) & (kpos < n)
        q_row = i * tq + lax.broadcasted_iota(jnp.int32, (1, tq), 1)
        n_pad = jnp.maximum(q_row + WINDOW - n + 1, 0).astype(F32)
        n_pad = jnp.concatenate([n_pad] * GQA_GROUP, axis=1)
        _attend(qt_ref, [(k, vt, valid), ctx_seg], sink_ref, n_pad, o_ref)

    @pl.when(i >= n_q_lat)
    def _():
        _attend(qt_ref, [ctx_seg], sink_ref, None, o_ref)


def _attention(qt, k, vt, dims, sink=None, ctx_queries=True):
    qw, rows = qt.shape
    kw = k.shape[1]
    n, lc, nb = dims["n"], dims["l"], dims["b"]
    tq = TQ if sink is not None else dims["tq_global"]
    n_q_lat = n // tq
    n_q_ctx = lc // tq
    lat_blocks = nb * n_q_lat
    ctx_block = nb * n // lc

    def q_blk(b, i):
        return jnp.where(i < n_q_lat, b * n_q_lat + i, lat_blocks + b * n_q_ctx + (i - n_q_lat))

    ctx_specs = [pl.BlockSpec((lc, kw), lambda b, i: (ctx_block + b, 0)),
                 pl.BlockSpec((kw, lc), lambda b, i: (0, ctx_block + b))]
    qt_spec = pl.BlockSpec((qw, tq), lambda b, i: (0, q_blk(b, i)))
    if sink is None:
        body = functools.partial(_attn_global_kernel, n_q_lat=n_q_lat, key_chunk=dims["key_chunk"])
        in_specs = [qt_spec,
                    pl.BlockSpec((n, kw), lambda b, i: (b, 0)),
                    pl.BlockSpec((kw, n), lambda b, i: (0, b))] + ctx_specs
        args = [qt, k, vt, k, vt]
    else:
        assert WINDOW == TQ
        body = functools.partial(_attn_window_kernel, n=n, n_q_lat=n_q_lat)
        nbr = lambda b, i, off: b * n_q_lat + jnp.clip(i + off, 0, n_q_lat - 1)
        n_kv = kw // HEAD_DIM
        in_specs = ([qt_spec]
                    + [pl.BlockSpec((TQ, kw), functools.partial(lambda b, i, off: (nbr(b, i, off), 0), off=off))
                       for off in (-1, 0, 1)]
                    + [pl.BlockSpec((kw, TQ), functools.partial(lambda b, i, off: (0, nbr(b, i, off)), off=off))
                       for off in (-1, 0, 1)]
                    + ctx_specs
                    + [pl.BlockSpec((n_kv, 1, GQA_GROUP * TQ), lambda b, i: (0, 0, 0))])
        sink_rows = jnp.repeat(sink.astype(F32).reshape(n_kv, 1, GQA_GROUP) * LOG2_E, TQ, axis=2)
        args = [qt, k, k, k, vt, vt, vt, k, vt, sink_rows]
    return pl.pallas_call(
        body,
        grid=(nb, n_q_lat + (n_q_ctx if ctx_queries else 0)),
        in_specs=in_specs,
        out_specs=pl.BlockSpec((tq, qw), lambda b, i: (q_blk(b, i), 0)),
        out_shape=jax.ShapeDtypeStruct((rows if ctx_queries else nb * n, qw), BF16),
        compiler_params=_cparams(("parallel", "arbitrary")),
        name="gqa_global" if sink is None else "gqa_window",
    )(*args)


def _feat_kernel(f_ref, fp_ref, fn_ref, mu_ref, w0_ref, w2_ref, a0_ref, a2_ref, g2_ref, kk_ref, ka_ref, bd_ref,
                 r_o, k_o, v_o, kk_o, g_o, lw0_o, lw1_o, kd0_o, kd1_o, b0_o, b1_o,
                 *, n_lat_tiles, tiles_lat, tiles_ctx):
    i = pl.program_id(0)
    t = f_ref.shape[0]
    bw = B_WIDTH
    is_lat = i < n_lat_tiles
    pos = jnp.where(is_lat, i % tiles_lat, (i - n_lat_tiles) % tiles_ctx)
    last = jnp.where(is_lat, tiles_lat, tiles_ctx) - 1
    f = f_ref[...]
    prow = jnp.where(pos == 0, 0.0, fp_ref[SUBLANES - 1:SUBLANES, :])
    nrow = jnp.where(pos == last, 0.0, fn_ref[0:1, :])
    rid = lax.broadcasted_iota(jnp.int32, (SUBLANES, 1), 0)
    prev = pltpu.roll(f, 1, 0)
    prev = jnp.concatenate([jnp.where(rid == 0, prow, prev[:SUBLANES]), prev[SUBLANES:]], axis=0)
    nxt = pltpu.roll(f, t - 1, 0)
    nxt = jnp.concatenate([nxt[:t - SUBLANES], jnp.where(rid == SUBLANES - 1, nrow, nxt[t - SUBLANES:])], axis=0)
    mu_p, mu_n = mu_ref[0:1, :], mu_ref[1:2, :]
    fs = f * (1.0 - mu_p - mu_n) + mu_p * prev + mu_n * nxt

    r = fs[:, :bw]
    k = fs[:, bw:2 * bw]
    v = fs[:, 2 * bw:3 * bw]
    wl = jnp.tanh(fs[:, 3 * bw:3 * bw + 2 * DECAY_LORA])
    al = fs[:, 3 * bw + 2 * DECAY_LORA:3 * bw + 2 * DECAY_LORA + 2 * ICLR_LORA]
    gl = fs[:, 3 * bw + 2 * DECAY_LORA + 2 * ICLR_LORA:]

    kk = k * kk_ref[...]
    kk = kk * lax.rsqrt(_headsum(kk * kk, bd_ref[...]) + EPS)
    r_o[...] = r.astype(BF16)
    k_o[...] = k.astype(BF16)
    v_o[...] = v.astype(BF16)
    kk_o[...] = kk.astype(BF16)
    g_o[...] = _dot(_sigmoid(gl).astype(BF16), g2_ref[...]).astype(BF16)
    wl = wl.astype(BF16)
    al = al.astype(BF16)
    for d, (lw_o, kd_o, b_o) in enumerate(((lw0_o, kd0_o, b0_o), (lw1_o, kd1_o, b1_o))):
        w_log = w0_ref[d:d + 1, :] + _dot(wl, w2_ref[d])
        lw_o[...] = -EXP_NEG_HALF * _sigmoid(w_log)
        a = _sigmoid(a0_ref[d:d + 1, :] + _dot(al, a2_ref[d]))
        kd_o[...] = (k * ((1.0 - ka_ref[...]) + a * ka_ref[...])).astype(BF16)
        b_o[...] = (kk * a).astype(BF16)


def _rwkv_features(f, params, dims):
    rows, fw = f.shape
    t = dims["tm_feat"]
    n, lc, nb = dims["n"], dims["l"], dims["b"]
    tiles_lat, tiles_ctx = n // t, lc // t
    n_lat_tiles = nb * tiles_lat
    hb = t // SUBLANES
    n_halo = rows // SUBLANES
    mu, w0, w2p, a0, a2p, g2, k_k, k_a, bd = params
    bw = B_WIDTH
    const = lambda shape: pl.BlockSpec(shape, lambda i: (0,) * len(shape))
    in_specs = [
        pl.BlockSpec((t, fw), lambda i: (i, 0)),
        pl.BlockSpec((SUBLANES, fw), lambda i: (jnp.maximum(i * hb - 1, 0), 0)),
        pl.BlockSpec((SUBLANES, fw), lambda i: (jnp.minimum((i + 1) * hb, n_halo - 1), 0)),
        const((2, fw)), const((2, bw)), const((2, 2 * DECAY_LORA, bw)), const((2, bw)),
        const((2, 2 * ICLR_LORA, bw)), const((GATE_LORA, bw)), const((1, bw)), const((1, bw)), const((bw, bw)),
    ]
    out = lambda dtype: jax.ShapeDtypeStruct((rows, bw), dtype)
    return pl.pallas_call(
        functools.partial(_feat_kernel, n_lat_tiles=n_lat_tiles, tiles_lat=tiles_lat, tiles_ctx=tiles_ctx),
        grid=(rows // t,),
        in_specs=in_specs,
        out_specs=[pl.BlockSpec((t, bw), lambda i: (i, 0))] * 11,
        out_shape=[out(BF16)] * 5 + [out(F32)] * 2 + [out(BF16)] * 4,
        compiler_params=_cparams(("parallel",)),
        name="rwkv_features",
    )(f, f, f, mu, w0, w2p, a0, a2p, g2, k_k, k_a, bd)


def _stack_heads(x, head0):
    return jnp.concatenate([jnp.where(head0, x, 0.0), jnp.where(head0, 0.0, x)], axis=0).astype(BF16)


def _cumsum_rows(tri, x):
    hi = x.astype(BF16)
    r1 = x - hi.astype(F32)
    mid = r1.astype(BF16)
    lo = (r1 - mid.astype(F32)).astype(BF16)
    return _dot(tri, hi) + _dot(tri, mid) + _dot(tri, lo)


def _wkv_units(units, eye, lvl_ref):
    c = units[0]["v"].shape[0]
    p2 = 2 * c
    head0 = lax.broadcasted_iota(jnp.int32, (1, LANES), 1) < HEAD_DIM
    st = lambda x: _stack_heads(x, head0)
    fold = lambda x: x[:c] + x[c:]
    each = lambda fn: [fn(u) for u in units]

    for u in units:
        u["kq_s"], u["rq_s"], u["v_s"] = st(u["kq"]), st(u["rq"]), st(u["v"])
    a = each(lambda u: _dot_nt(jnp.concatenate([u["kq_s"], u["rq_s"]], axis=0),
                               jnp.concatenate([st(u["bi"]), st(u["ki"])], axis=0)))
    for u, au in zip(units, a):
        u["nkb"] = jnp.where(u["m_strict"], au[:p2, :p2], 0.0)
        u["akk"] = jnp.where(u["m_strict"], au[:p2, p2:], 0.0).astype(BF16)
        u["arb"] = jnp.where(u["m_incl"], au[p2:, :p2], 0.0).astype(BF16)
        u["ark"] = jnp.where(u["m_incl"], au[p2:, p2:], 0.0).astype(BF16)

    nkb_h = each(lambda u: u["nkb"].astype(BF16))
    tinv = each(lambda u: eye - u["nkb"] * lvl_ref[0].astype(F32))
    for lev in range(1, lvl_ref.shape[0]):
        t_h = [t.astype(BF16) for t in tinv]
        lt = [_dot(nh * lvl_ref[lev], th).astype(BF16) for nh, th in zip(nkb_h, t_h)]
        tinv = [t - _dot(th, x) for t, th, x in zip(tinv, t_h, lt)]
    t_h = [t.astype(BF16) for t in tinv]

    akkv = each(lambda u: _dot(u["akk"], u["v_s"]).astype(BF16))
    gp_h = [_dot(th, jnp.concatenate([u["kq_s"], x], axis=1)).astype(BF16)
            for u, th, x in zip(units, t_h, akkv)]
    corr = [_dot(u["arb"], g) for u, g in zip(units, gp_h)]
    arkv = each(lambda u: _dot(u["ark"], u["v_s"]))
    btg = [_dot_tn(st(u["bt"]), g) for u, g in zip(units, gp_h)]
    ktv = each(lambda u: _dot_tn(st(u["kt"]), u["v_s"]))

    for u, cr, av, bg, kv in zip(units, corr, arkv, btg, ktv):
        rq = u["rq"]
        rq2 = fold(jnp.concatenate([jnp.where(head0, rq, 0.0), jnp.where(head0, 0.0, rq)], axis=0) - cr[:, :LANES])
        u["rq2"] = rq2.astype(BF16)
        u["yl"] = fold(av - cr[:, LANES:])
        u["m_state"] = (jnp.where(eye > 0.0, u["decay"], 0.0) - bg[:, :LANES]).astype(BF16)
        u["n_state"] = kv - bg[:, LANES:]


def _scan_kernel(rf, vf, kkf, lwf, kdf, bf, rb, vb, kkb, lwb, kdb, bb, tri_ref, msk_ref, lvl_ref,
                 yf_o, yb_o, hf_s, hb_s):
    @pl.when(pl.program_id(1) == 0)
    def _():
        hf_s[...] = jnp.zeros_like(hf_s)
        hb_s[...] = jnp.zeros_like(hb_s)

    c = SCAN_CHUNK
    eye = msk_ref[0]
    units, chains = [], []
    for d, (refs, y_o, h_s) in enumerate((((rf, vf, kkf, lwf, kdf, bf), yf_o, hf_s),
                                          ((rb, vb, kkb, lwb, kdb, bb), yb_o, hb_s))):
        m_strict = msk_ref[1 + 2 * d] > 0.0
        m_incl = msk_ref[2 + 2 * d] > 0.0
        r, v, kk, lw, kd, b = (ref[...] for ref in refs)
        cw = _cumsum_rows(tri_ref[d], lw)
        tot = [jnp.sum(lw[j * c:(j + 1) * c], axis=0, keepdims=True) for j in range(SCAN_STEP_CHUNKS)]
        tot_rows = jnp.concatenate([jnp.broadcast_to(t, (c, t.shape[1])) for t in tot], axis=0)
        w_inv = jnp.exp(-cw)
        w_rest = jnp.exp(tot_rows - cw)
        facs = {"kq": kk * jnp.exp(cw - lw), "rq": r * jnp.exp(cw), "bi": b * w_inv, "ki": kd * w_inv,
                "bt": b * w_rest, "kt": kd * w_rest, "v": v}
        order = range(SCAN_STEP_CHUNKS) if d == 0 else range(SCAN_STEP_CHUNKS - 1, -1, -1)
        for p in range(B_WIDTH // LANES):
            ls = slice(p * LANES, (p + 1) * LANES)
            chain = []
            for j in order:
                unit = {name: x[j * c:(j + 1) * c, ls] for name, x in facs.items()}
                unit.update(decay=jnp.exp(tot[j][:, ls]), m_strict=m_strict, m_incl=m_incl, rows=slice(j * c, (j + 1) * c))
                units.append(unit)
                chain.append(unit)
            chains.append((chain, y_o, h_s, p, ls))
    _wkv_units(units, eye, lvl_ref)
    states = [h_s[p] for _, _, h_s, p, _ in chains]
    for pos in range(SCAN_STEP_CHUNKS):
        for ci, (chain, y_o, _, _, ls) in enumerate(chains):
            u = chain[pos]
            h_h = states[ci].astype(BF16)
            y_o[u["rows"], ls] = _dot(u["rq2"], h_h) + u["yl"]
            states[ci] = _dot(u["m_state"], h_h) + u["n_state"]
    for h, (_, _, h_s, p, _) in zip(states, chains):
        h_s[p] = h


def _scan_constants():
    c, p2 = SCAN_CHUNK, PAIR_ROWS
    t = np.arange(SCAN_STEP_CHUNKS * c)
    same_chunk = (t[None, :] // c) == (t[:, None] // c)
    tri = np.stack([same_chunk & (t[None, :] <= t[:, None]), same_chunk & (t[None, :] >= t[:, None])]).astype(np.float32)
    i = np.arange(p2)
    same_head = (i[:, None] // c) == (i[None, :] // c)
    ti, si = i[:, None] % c, i[None, :] % c
    msk = np.stack([
        np.eye(p2, dtype=bool),
        same_head & (si < ti), same_head & (si <= ti),
        same_head & (si > ti), same_head & (si >= ti),
    ]).astype(np.float32)
    n_lev = int(np.log2(c))
    lvl = np.stack([
        ((i[:, None] >> (k + 1)) == (i[None, :] >> (k + 1))) & ((i[:, None] >> k) != (i[None, :] >> k))
        for k in range(n_lev)
    ]).astype(np.float32)
    return jnp.asarray(tri, BF16), jnp.asarray(msk), jnp.asarray(lvl, BF16)


def _wkv_scan(feats, dims):
    r, v, kk, lw0, lw1, kd0, kd1, b0, b1 = feats
    rows, bw = r.shape
    c = SCAN_STEP_CHUNKS * SCAN_CHUNK
    n, lc, nb = dims["n"], dims["l"], dims["b"]
    n_c, l_c = n // c, lc // c
    ctx_base = nb * n_c
    tri, msk, lvl = _scan_constants()

    def fwd(b, s):
        return (jnp.where(s < l_c, ctx_base + b * l_c + s, b * n_c + (s - l_c)), 0)

    def bwd(b, s):
        return (jnp.where(s < l_c, ctx_base + b * l_c + (l_c - 1 - s), b * n_c + (n_c - 1 - (s - l_c))), 0)

    const = lambda a: pl.BlockSpec(a.shape, lambda b, s: (0,) * a.ndim)
    out = jax.ShapeDtypeStruct((rows, bw), F32)
    return pl.pallas_call(
        _scan_kernel,
        grid=(nb, l_c + n_c),
        in_specs=[pl.BlockSpec((c, bw), fwd)] * 6 + [pl.BlockSpec((c, bw), bwd)] * 6
                 + [const(tri), const(msk), const(lvl)],
        out_specs=[pl.BlockSpec((c, bw), fwd), pl.BlockSpec((c, bw), bwd)],
        out_shape=[out, out],
        scratch_shapes=[pltpu.VMEM((bw // LANES, LANES, LANES), F32)] * 2,
        compiler_params=_cparams(("parallel", "arbitrary")),
        name="wkv_scan",
    )(r, v, kk, lw0, kd0, b0, r, v, kk, lw1, kd1, b1, tri, msk, lvl)


def _rope_tables(n, tm):
    rows = n // GRID_W
    row = jnp.repeat(jnp.arange(rows, dtype=F32), GRID_W)
    col = jnp.tile(jnp.arange(GRID_W, dtype=F32), rows)
    n_freq = HEAD_DIM // 4
    inv_freq = ROPE_THETA ** (-jnp.arange(n_freq, dtype=F32) / n_freq)
    ang = jnp.concatenate([row[:, None] * inv_freq, col[:, None] * inv_freq], axis=-1)
    cos = jnp.repeat(jnp.cos(ang), 2, axis=-1)
    sin = jnp.repeat(jnp.sin(ang), 2, axis=-1)
    even_lane = (jnp.arange(HEAD_DIM) % 2 == 0)[None, :]
    sin_a = jnp.where(even_lane, -sin, 0.0)
    sin_b = jnp.where(even_lane, 0.0, sin)
    pad = lambda tab, fill: jnp.concatenate(
        [jnp.tile(tab, (1, LANES // HEAD_DIM)), jnp.full((tm, LANES), fill, F32)], axis=0)
    return pad(cos, 1.0), pad(sin_a, 0.0), pad(sin_b, 0.0)


def _block_diag(width, value, dtype):
    h = np.arange(width) // HEAD_DIM
    return jnp.asarray((h[:, None] == h[None, :]) * value, dtype)


def _pad_lora(w):
    r = w.shape[1]
    z = jnp.zeros_like(w[0])
    return jnp.stack([jnp.concatenate([w[0], z], axis=0), jnp.concatenate([z, w[1]], axis=0)])


def _largest_tile(limit, *sizes):
    t = limit
    while any(s % t for s in sizes):
        t //= 2
    return t


def kernel(x, c, ctx, c_ctx, w_mod, b_mod, ffn_in, ffn_out, even_w_in, even_w_out, q_gain, k_gain, rwkv_mu, rwkv_w0,
           rwkv_w2, rwkv_a0, rwkv_a2, rwkv_g2, rwkv_k_k, rwkv_k_a, rwkv_r_k, rwkv_gn_w, rwkv_gn_b, odd_w_in, odd_w_out,
           sink, final_gain):
    nb, n, d = x.shape
    lc = ctx.shape[1]
    depth = w_mod.shape[0]
    dff = ffn_out.shape[2]
    assert n % TQ == 0 and lc % TQ == 0 and n >= TQ + 2 * WINDOW and n % GRID_W == 0
    scan_rows = SCAN_STEP_CHUNKS * SCAN_CHUNK
    assert n % scan_rows == 0 and lc % scan_rows == 0 and nb + 1 <= MOD_ROWS
    dims = {
        "b": nb, "n": n, "l": lc,
        "tm_ffn": _largest_tile(512, n, nb * lc),
        "tm_proj": _largest_tile(512, n, nb * lc),
        "tm_feat": _largest_tile(256, n, lc),
        "key_chunk": _largest_tile(256, n),
        "tq_global": _largest_tile(256, n, lc),
    }

    cvec = jnp.zeros((MOD_ROWS, d), F32).at[:nb].set(c).at[nb].set(c_ctx)
    mods = _mod_table(cvec, w_mod, b_mod).reshape(depth, MOD_ROWS, N_MOD, d)

    ffn_in_h = ffn_in.astype(BF16)
    ffn_out_h = ffn_out.astype(BF16)
    even_in_h = even_w_in.astype(BF16)
    even_out_h = even_w_out.astype(BF16)
    odd_in_h = odd_w_in.astype(BF16)
    odd_out_h = odd_w_out.astype(BF16)

    rope_tabs = _rope_tables(n, dims["tm_proj"])
    a_kw = A_KV_HEADS * HEAD_DIM
    a_qw = A_HEADS * HEAD_DIM
    bd_q = _block_diag(a_qw, 1.0 / HEAD_DIM, BF16)
    bd_k = _block_diag(a_kw, 1.0 / HEAD_DIM, BF16)
    bd_ones = _block_diag(B_WIDTH, 1.0, BF16)
    fw = even_w_in.shape[2] - a_qw - 2 * a_kw

    xs = x.reshape(nb * n, d)
    for l in range(depth):
        last = l == depth - 1
        xs = _ffn(xs, mods, ffn_in_h, ffn_out_h, l, 0, 0, dims, ctx_rows=ctx.reshape(nb * lc, d) if l == 0 else None)
        if l % 2 == 0:
            e = l // 2
            tile_h = lambda g, reps: jnp.tile(g, reps).reshape(1, -1)
            norm_args = (tile_h(q_gain[e], A_HEADS), tile_h(k_gain[e], A_KV_HEADS), bd_q, bd_k)
            q, k, v, f = _proj(xs, mods, even_in_h, l, e, rope_tabs, dims, a_qw, a_kw, fw, norm_args)
            oa = _attention(q, k, v, dims, ctx_queries=not last)
            feat_params = (rwkv_mu[e], rwkv_w0[e], _pad_lora(rwkv_w2[e]).astype(BF16), rwkv_a0[e],
                           _pad_lora(rwkv_a2[e]).astype(BF16), rwkv_g2[e].astype(BF16),
                           rwkv_k_k[e].reshape(1, -1), rwkv_k_a[e].reshape(1, -1), bd_ones)
            r, kr, vr, kk, g, lw0, lw1, kd0, kd1, b0, b1 = _rwkv_features(f, feat_params, dims)
            yf, yb = _wkv_scan((r, vr, kk, lw0, lw1, kd0, kd1, b0, b1), dims)
            readout = (yf, yb, r, kr, vr, g, rwkv_r_k[e].reshape(1, -1), rwkv_gn_w[e].reshape(1, -1),
                       rwkv_gn_b[e].reshape(1, -1), bd_ones)
            mixer = (even_out_h, e, oa, readout)
        else:
            o = l // 2
            c_kw = C_KV_HEADS * HEAD_DIM
            c_qw = C_HEADS * HEAD_DIM
            q, k, v = _proj(xs, mods, odd_in_h, l, o, rope_tabs, dims, c_qw, c_kw, 0)
            oc = _attention(q, k, v, dims, sink=sink[o], ctx_queries=not last)
            mixer = (odd_out_h, o, oc, None)
        xs = _ffn(xs, mods, ffn_in_h, ffn_out_h, l, 1, 6, dims, final_gain=final_gain if last else None, mixer=mixer)
    return xs.reshape(nb, n, d)
```

```python
import functools

import jax
import jax.numpy as jnp
import numpy as np
from jax import lax
from jax.experimental import pallas as pl
from jax.experimental.pallas import tpu as pltpu

F32 = jnp.float32
BF16 = jnp.bfloat16

HEAD_DIM = 64
GRID_W = 64
A_HEADS, A_KV_HEADS = 8, 2
B_HEADS = 8
B_WIDTH = B_HEADS * HEAD_DIM
DECAY_LORA, ICLR_LORA, GATE_LORA = 64, 64, 128
C_HEADS, C_KV_HEADS = 16, 4
GQA_GROUP = 4
WINDOW = 128
N_MOD = 9
ROPE_THETA = 10000.0
EPS = 1e-6
GN_EPS = 64e-5
NEG_BIG = -1e30
LOG2_E = 1.4426950408889634
EXP_NEG_HALF = 0.6065306597126334

LANES = 128
SUBLANES = 8
VMEM_LIMIT_BYTES = 56 * 1024 * 1024

SCAN_CHUNK = 64
SCAN_STEP_CHUNKS = 4
PAIR_ROWS = 2 * SCAN_CHUNK
TQ = 128
AHEAD = 4
ONES_ROWS = 16
MOD_ROWS = 16


def _cparams(sem):
    return pltpu.CompilerParams(dimension_semantics=sem, vmem_limit_bytes=VMEM_LIMIT_BYTES)


def _dot(a, b, precision=None):
    return jnp.dot(a, b, preferred_element_type=F32, precision=precision)


def _dot_nt(a, b, precision=None):
    return lax.dot_general(a, b, (((1,), (1,)), ((), ())), preferred_element_type=F32, precision=precision)


def _dot_tn(a, b, precision=None):
    return lax.dot_general(a, b, (((0,), (0,)), ((), ())), preferred_element_type=F32, precision=precision)


def _rms(x):
    return x * lax.rsqrt(jnp.mean(x * x, axis=-1, keepdims=True) + EPS)


def _sigmoid(x):
    return 1.0 / (1.0 + jnp.exp(-x))


def _headsum(x, bd, split=False):
    hi = x.astype(BF16)
    if not split:
        return _dot(hi, bd)
    lo = (x - hi.astype(F32)).astype(BF16)
    return _dot(hi, bd) + _dot(lo, bd)


def _mod_kernel(c_ref, w_ref, b_ref, o_ref):
    c = c_ref[...]
    s = c * _sigmoid(c)
    hi = s.astype(BF16)
    lo = (s - hi.astype(F32)).astype(BF16)
    w = w_ref[...].astype(BF16)
    o_ref[...] = _dot(hi, w) + _dot(lo, w) + b_ref[...]


def _mod_table(cvec, w_mod, b_mod):
    depth, d, nd = w_mod.shape
    tn = 1536 if nd % 1536 == 0 else nd
    return pl.pallas_call(
        _mod_kernel,
        grid=(depth, nd // tn),
        in_specs=[
            pl.BlockSpec((MOD_ROWS, d), lambda l, n: (0, 0)),
            pl.BlockSpec((None, d, tn), lambda l, n: (l, 0, n)),
            pl.BlockSpec((None, 1, tn), lambda l, n: (l, 0, n)),
        ],
        out_specs=pl.BlockSpec((None, MOD_ROWS, tn), lambda l, n: (l, 0, n)),
        out_shape=jax.ShapeDtypeStruct((depth, MOD_ROWS, nd), F32),
        compiler_params=_cparams(("parallel", "parallel")),
        name="mod_table",
    )(cvec, w_mod, b_mod.reshape(depth, 1, nd))


def _mixer_out(oa_ref, wm_ref, readout):
    if readout is None:
        return _dot(oa_ref[...], wm_ref[...])
    yf, yb, r, k, v, g, rk, gnw, gnb, bd = readout
    y = yf[...] + yb[...]
    mean = _headsum(y, bd[...], split=True) * (1.0 / HEAD_DIM)
    yc = y - mean
    var = _headsum(yc * yc, bd[...]) * (1.0 / HEAD_DIM)
    yn = yc * lax.rsqrt(var + GN_EPS) * gnw[...] + gnb[...]
    up = lambda ref: ref[...].astype(F32)
    bonus = _headsum(up(r) * up(k) * rk[...], bd[...]) * up(v)
    ob = ((yn + bonus) * up(g)).astype(BF16)
    half = oa_ref.shape[1]
    return _dot(oa_ref[...], wm_ref[:half, :]) + _dot(ob, wm_ref[half:, :])


def _ffn_kernel(x_ref, mod_ref, wg_ref, wu_ref, wo_ref, *rest, i_shift, final, n_lat_tiles, mixer, proj):
    n_out = 1 if proj is None else (5 if proj["fw"] else 4)
    rest, outs = rest[:len(rest) - n_out], rest[len(rest) - n_out:]
    if proj is not None:
        n_proj = 8 if proj["qk_norm"] else 4
        rest, proj_refs = rest[:len(rest) - n_proj], rest[len(rest) - n_proj:]
    o_ref = outs[0]
    ctx_ref = None
    if n_lat_tiles is not None:
        ctx_ref, rest = rest[0], rest[1:]
    x = x_ref[...]
    if ctx_ref is not None:
        x = jnp.where(pl.program_id(0) < n_lat_tiles, x, ctx_ref[...])
    if mixer is not None:
        readout = rest[2:12] if mixer == "even" else None
        x = x + mod_ref[5:6, :] * _mixer_out(rest[1], rest[0], readout)
    half = x.shape[0] // 2
    shift, scale = mod_ref[i_shift:i_shift + 1, :], 1.0 + mod_ref[i_shift + 1:i_shift + 2, :]
    gate = 0.5 * mod_ref[i_shift + 2:i_shift + 3, :]
    xs, gu = [x[:half], x[half:]], []
    for xh in xs:
        h = (_rms(xh) * scale + shift).astype(BF16)
        gu.append((_dot(h, wg_ref[...]), _dot(h, wu_ref[...])))
    ys = []
    for idx, (x, (g, u)) in enumerate(zip(xs, gu)):
        a = (g * _sigmoid(g) * u).astype(BF16)
        y = x + gate * _dot(a, wo_ref[...])
        if final:
            y = _rms(y) * rest[-1][...]
        o_ref[idx * half:(idx + 1) * half, :] = y
        ys.append(y)
    if proj is not None:
        _proj_halves(ys, mod_ref, proj_refs, outs[1:], **proj)


def _ffn(xs, mods, w_in, w_out, l, j, i_shift, dims, final_gain=None, ctx_rows=None, mixer=None, proj=None):
    rows, d = xs.shape
    dff = w_out.shape[2]
    tm = dims["tm_ffn"]
    tiles_per_batch = dims["n"] // tm
    nb = dims["b"]
    n_lat_tiles = nb * tiles_per_batch
    final = final_gain is not None
    if final:
        rows = nb * dims["n"]
    split = ctx_rows is not None
    if split:
        rows = rows + ctx_rows.shape[0]
    in_specs = [
        pl.BlockSpec((tm, d), (lambda i: (jnp.minimum(i, n_lat_tiles - 1), 0)) if split else (lambda i: (i, 0))),
        pl.BlockSpec((None, None, N_MOD, d), lambda i: (l, jnp.minimum(i // tiles_per_batch, nb), 0, 0)),
        pl.BlockSpec((None, None, d, dff), lambda i: (l, j, 0, 0)),
        pl.BlockSpec((None, None, d, dff), lambda i: (l, j, 0, 1)),
        pl.BlockSpec((None, None, dff, d), lambda i: (l, j, 0, 0)),
    ]
    args = [xs, mods, w_in, w_in, w_out]
    if split:
        in_specs.append(pl.BlockSpec((tm, d), lambda i: (jnp.maximum(i - n_lat_tiles, 0), 0)))
        args.append(ctx_rows)
    mixer_kind = None
    if mixer is not None:
        w_mix, e, o_att, readout = mixer
        mixer_kind = "odd" if readout is None else "even"
        in_specs += [pl.BlockSpec((None, d, d), lambda i: (e, 0, 0)),
                     pl.BlockSpec((tm, o_att.shape[1]), lambda i: (i, 0))]
        args += [w_mix, o_att]
        if readout is not None:
            bw = B_WIDTH
            in_specs += [pl.BlockSpec((tm, bw), lambda i: (i, 0))] * 6
            in_specs += [pl.BlockSpec((1, bw), lambda i: (0, 0))] * 3 + [pl.BlockSpec((bw, bw), lambda i: (0, 0))]
            args += list(readout)
    if final:
        in_specs.append(pl.BlockSpec((1, d), lambda i: (0, 0)))
        args.append(final_gain.reshape(1, d))
    out_specs = [pl.BlockSpec((tm, d), lambda i: (i, 0))]
    out_shape = [jax.ShapeDtypeStruct((rows, d), F32)]
    proj_cfg = None
    if proj is not None:
        w_pin, e, rope_tabs, qw, kw, fw, norm_args = proj
        proj_cfg = {"qw": qw, "kw": kw, "fw": fw, "qk_norm": norm_args is not None}

        def tab_idx(i):
            return (jnp.where(i < n_lat_tiles, i % tiles_per_batch, tiles_per_batch), 0)

        in_specs += [pl.BlockSpec((None, d, qw + 2 * kw + fw), lambda i: (e, 0, 0))]
        in_specs += [pl.BlockSpec((tm, LANES), tab_idx)] * 3
        args += [w_pin, *rope_tabs]
        if norm_args is not None:
            in_specs += [pl.BlockSpec((1, qw), lambda i: (0, 0)), pl.BlockSpec((1, kw), lambda i: (0, 0)),
                         pl.BlockSpec((qw, qw), lambda i: (0, 0)), pl.BlockSpec((kw, kw), lambda i: (0, 0))]
            args += list(norm_args)
        out_specs += [pl.BlockSpec((qw, tm), lambda i: (0, i)), pl.BlockSpec((tm, kw), lambda i: (i, 0)),
                      pl.BlockSpec((kw, tm), lambda i: (0, i))]
        out_shape += [jax.ShapeDtypeStruct((qw, rows), BF16), jax.ShapeDtypeStruct((rows, kw), BF16),
                      jax.ShapeDtypeStruct((kw, rows), BF16)]
        if fw:
            out_specs.append(pl.BlockSpec((tm, fw), lambda i: (i, 0)))
            out_shape.append(jax.ShapeDtypeStruct((rows, fw), F32))
    outs = pl.pallas_call(
        functools.partial(_ffn_kernel, i_shift=i_shift, final=final, n_lat_tiles=n_lat_tiles if split else None,
                          mixer=mixer_kind, proj=proj_cfg),
        grid=(rows // tm,),
        in_specs=in_specs,
        out_specs=out_specs,
        out_shape=out_shape,
        compiler_params=_cparams(("parallel",)),
        name="ffn",
    )(*args)
    return outs[0] if proj is None else outs


def _rope(x, cos, sin_a, sin_b):
    outs = []
    for g in range(x.shape[1] // LANES):
        xg = x[:, g * LANES:(g + 1) * LANES]
        nxt = pltpu.roll(xg, LANES - 1, 1)
        prv = pltpu.roll(xg, 1, 1)
        outs.append(xg * cos + nxt * sin_a + prv * sin_b)
    return outs[0] if len(outs) == 1 else jnp.concatenate(outs, axis=1)


def _proj_halves(xs, mod_ref, in_refs, out_refs, *, qw, kw, fw, qk_norm):
    w_ref, cos_ref, sa_ref, sb_ref = in_refs[:4]
    if qk_norm:
        qg_ref, kg_ref, bdq_ref, bdk_ref = in_refs[4:8]
    q_ref, k_ref, v_ref = out_refs[:3]
    half = xs[0].shape[0]
    halves = [slice(0, half), slice(half, 2 * half)]
    hn, qkv = [], []
    for x in xs:
        h = (_rms(x) * (1.0 + mod_ref[4:5, :]) + mod_ref[3:4, :]).astype(BF16)
        hn.append(h)
        kv = _dot(h, w_ref[:, qw:qw + 2 * kw])
        qkv.append((_dot(h, w_ref[:, :qw]), kv[:, :kw], kv[:, kw:]))
    for rows, h, (q, k, v) in zip(halves, hn, qkv):
        if qk_norm:
            q = q * lax.rsqrt(_headsum(q * q, bdq_ref[...]) + EPS) * qg_ref[...]
            k = k * lax.rsqrt(_headsum(k * k, bdk_ref[...]) + EPS) * kg_ref[...]
        cos, sa, sb = cos_ref[rows, :], sa_ref[rows, :], sb_ref[rows, :]
        q_ref[:, rows] = (_rope(q, cos, sa, sb) * (HEAD_DIM ** -0.5 * LOG2_E)).T.astype(BF16)
        k_ref[rows, :] = _rope(k, cos, sa, sb).astype(BF16)
        v_ref[:, rows] = v.T.astype(BF16)
        if fw:
            out_refs[3][rows, :] = _dot(h, w_ref[:, qw + 2 * kw:])


def _attend(qt_ref, segs, sink_ref, n_pad, o_ref):
    tq = qt_ref.shape[1]
    cols = GQA_GROUP * tq
    heads = range(qt_ref.shape[0] // (GQA_GROUP * HEAD_DIM))
    pair = lambda h: slice((h // 2) * LANES, (h // 2 + 1) * LANES)
    qpad, m, acc = [], [], []
    for h in heads:
        q4 = jnp.concatenate([qt_ref[(h * GQA_GROUP + g) * HEAD_DIM:(h * GQA_GROUP + g + 1) * HEAD_DIM, :]
                              for g in range(GQA_GROUP)], axis=1)
        z = jnp.zeros_like(q4)
        qpad.append(jnp.concatenate([q4, z] if h % 2 == 0 else [z, q4], axis=0))
        if sink_ref is not None:
            mh, lh = sink_ref[h], jnp.ones((1, cols), F32)
        else:
            mh, lh = jnp.full((1, cols), NEG_BIG, F32), jnp.zeros((1, cols), F32)
        if n_pad is not None:
            m_new = jnp.maximum(mh, jnp.where(n_pad > 0.0, 0.0, NEG_BIG))
            lh = lh * jnp.exp2(mh - m_new) + jnp.where(n_pad > 0.0, n_pad * jnp.exp2(-m_new), 0.0)
            mh = m_new
        m.append(mh)
        acc.append(jnp.concatenate([jnp.zeros((HEAD_DIM, cols), F32), jnp.broadcast_to(lh, (ONES_ROWS, cols))], axis=0))
    def scores(item):
        (k, _, valid), h = item
        s = _dot(k[:, pair(h)], qpad[h])
        if valid is not None:
            s = jnp.where(jnp.concatenate([valid] * GQA_GROUP, axis=1), s, NEG_BIG)
        return s

    items = [(seg, h) for seg in segs for h in heads]
    pending = [scores(it) for it in items[:AHEAD]]
    for idx, ((_, vt, _), h) in enumerate(items):
        s = pending.pop(0)
        if idx + AHEAD < len(items):
            pending.append(scores(items[idx + AHEAD]))
        m_new = jnp.maximum(m[h], jnp.max(s, axis=0, keepdims=True))
        alpha = jnp.exp2(m[h] - m_new)
        p = jnp.exp2(s - m_new).astype(BF16)
        v1 =jnp.concatenate([vt[h * HEAD_DIM:(h + 1) * HEAD_DIM, :], jnp.ones((ONES_ROWS, vt.shape[1]), BF16)], axis=0)
        acc[h] = alpha * acc[h] + _dot(v1, p)
        m[h] = m_new
    blocks = []
    for h in heads:
        ot = acc[h][:HEAD_DIM, :] * (1.0 / acc[h][HEAD_DIM:HEAD_DIM + 1, :])
        for g in range(0, GQA_GROUP, 2):
            blocks.append(jnp.concatenate([ot[:, g * tq:(g + 1) * tq], ot[:, (g + 1) * tq:(g + 2) * tq]], axis=0).T)
    o_ref[...] = jnp.concatenate(blocks, axis=1).astype(o_ref.dtype)


def _attn_global_kernel(qt_ref, kl_ref, vl_ref, kc_ref, vc_ref, o_ref, *, n_q_lat, key_chunk):
    i = pl.program_id(1)
    ctx_seg = (kc_ref[...], vc_ref[...], None)

    @pl.when(i < n_q_lat)
    def _():
        segs = [(kl_ref[c * key_chunk:(c + 1) * key_chunk, :], vl_ref[:, c * key_chunk:(c + 1) * key_chunk], None)
                for c in range(kl_ref.shape[0] // key_chunk)]
        _attend(qt_ref, segs + [ctx_seg], None, None, o_ref)

    @pl.when(i >= n_q_lat)
    def _():
        _attend(qt_ref, [ctx_seg], None, None, o_ref)


def _attn_window_kernel(qt_ref, k0_ref, k1_ref, k2_ref, v0_ref, v1_ref, v2_ref, kc_ref, vc_ref, sink_ref, o_ref,
                        *, n, n_q_lat):
    i = pl.program_id(1)
    tq = qt_ref.shape[1]
    ctx_seg = (kc_ref[...], vc_ref[...], None)

    @pl.when(i < n_q_lat)
    def _():
        k = jnp.concatenate([k0_ref[...], k1_ref[...], k2_ref[...]], axis=0)
        vt = jnp.concatenate([v0_ref[...], v1_ref[...], v2_ref[...]], axis=1)
        kpos = (i - 1) * tq + lax.broadcasted_iota(jnp.int32, (3 * tq, tq), 0)
        qpos = i * tq + lax.broadcasted_iota(jnp.int32, (3 * tq, tq), 1)
        valid = (jnp.abs(kpos - qpos) <= WINDOW) & (kpos >= WINDOW) & (kpos < n)
        q_row = i * tq + lax.broadcasted_iota(jnp.int32, (1, tq), 1)
        n_pad = jnp.maximum(q_row + WINDOW - n + 1, 0).astype(F32)
        n_pad = jnp.concatenate([n_pad] * GQA_GROUP, axis=1)
        _attend(qt_ref, [(k, vt, valid), ctx_seg], sink_ref, n_pad, o_ref)

    @pl.when(i >= n_q_lat)
    def _():
        _attend(qt_ref, [ctx_seg], sink_ref, None, o_ref)


def _attention(qt, k, vt, dims, sink=None, ctx_queries=True):
    qw, rows = qt.shape
    kw = k.shape[1]
    n, lc, nb = dims["n"], dims["l"], dims["b"]
    tq = TQ if sink is not None else dims["tq_global"]
    n_q_lat = n // tq
    n_q_ctx = lc // tq
    lat_blocks = nb * n_q_lat
    ctx_block = nb * n // lc

    def q_blk(b, i):
        return jnp.where(i < n_q_lat, b * n_q_lat + i, lat_blocks + b * n_q_ctx + (i - n_q_lat))

    ctx_specs = [pl.BlockSpec((lc, kw), lambda b, i: (ctx_block + b, 0)),
                 pl.BlockSpec((kw, lc), lambda b, i: (0, ctx_block + b))]
    qt_spec = pl.BlockSpec((qw, tq), lambda b, i: (0, q_blk(b, i)))
    if sink is None:
        body = functools.partial(_attn_global_kernel, n_q_lat=n_q_lat, key_chunk=dims["key_chunk"])
        in_specs = [qt_spec,
                    pl.BlockSpec((n, kw), lambda b, i: (b, 0)),
                    pl.BlockSpec((kw, n), lambda b, i: (0, b))] + ctx_specs
        args = [qt, k, vt, k, vt]
    else:
        assert WINDOW == TQ
        body = functools.partial(_attn_window_kernel, n=n, n_q_lat=n_q_lat)
        nbr = lambda b, i, off: b * n_q_lat + jnp.clip(i + off, 0, n_q_lat - 1)
        n_kv = kw // HEAD_DIM
        in_specs = ([qt_spec]
                    + [pl.BlockSpec((TQ, kw), functools.partial(lambda b, i, off: (nbr(b, i, off), 0), off=off))
                       for off in (-1, 0, 1)]
                    + [pl.BlockSpec((kw, TQ), functools.partial(lambda b, i, off: (0, nbr(b, i, off)), off=off))
                       for off in (-1, 0, 1)]
                    + ctx_specs
                    + [pl.BlockSpec((n_kv, 1, GQA_GROUP * TQ), lambda b, i: (0, 0, 0))])
        sink_rows = jnp.repeat(sink.astype(F32).reshape(n_kv, 1, GQA_GROUP) * LOG2_E, TQ, axis=2)
        args = [qt, k, k, k, vt, vt, vt, k, vt, sink_rows]
    return pl.pallas_call(
        body,
        grid=(nb, n_q_lat + (n_q_ctx if ctx_queries else 0)),
        in_specs=in_specs,
        out_specs=pl.BlockSpec((tq, qw), lambda b, i: (q_blk(b, i), 0)),
        out_shape=jax.ShapeDtypeStruct((rows if ctx_queries else nb * n, qw), BF16),
        compiler_params=_cparams(("parallel", "arbitrary")),
        name="gqa_global" if sink is None else "gqa_window",
    )(*args)


def _feat_kernel(f_ref, fp_ref, fn_ref, mu_ref, w0_ref, w2_ref, a0_ref, a2_ref, g2_ref, kk_ref, ka_ref, bd_ref,
                 r_o, k_o, v_o, kk_o, g_o, lw0_o, lw1_o, kd0_o, kd1_o, b0_o, b1_o,
                 *, n_lat_tiles, tiles_lat, tiles_ctx):
    i = pl.program_id(0)
    t = f_ref.shape[0]
    bw = B_WIDTH
    is_lat = i < n_lat_tiles
    pos = jnp.where(is_lat, i % tiles_lat, (i - n_lat_tiles) % tiles_ctx)
    last = jnp.where(is_lat, tiles_lat, tiles_ctx) - 1
    f = f_ref[...]
    prow = jnp.where(pos == 0, 0.0, fp_ref[SUBLANES - 1:SUBLANES, :])
    nrow = jnp.where(pos == last, 0.0, fn_ref[0:1, :])
    rid = lax.broadcasted_iota(jnp.int32, (SUBLANES, 1), 0)
    prev = pltpu.roll(f, 1, 0)
    prev = jnp.concatenate([jnp.where(rid == 0, prow, prev[:SUBLANES]), prev[SUBLANES:]], axis=0)
    nxt = pltpu.roll(f, t - 1, 0)
    nxt = jnp.concatenate([nxt[:t - SUBLANES], jnp.where(rid == SUBLANES - 1, nrow, nxt[t - SUBLANES:])], axis=0)
    mu_p, mu_n = mu_ref[0:1, :], mu_ref[1:2, :]
    fs = f * (1.0 - mu_p - mu_n) + mu_p * prev + mu_n * nxt

    r = fs[:, :bw]
    k = fs[:, bw:2 * bw]
    v = fs[:, 2 * bw:3 * bw]
    wl = jnp.tanh(fs[:, 3 * bw:3 * bw + 2 * DECAY_LORA])
    al = fs[:, 3 * bw + 2 * DECAY_LORA:3 * bw + 2 * DECAY_LORA + 2 * ICLR_LORA]
    gl = fs[:, 3 * bw + 2 * DECAY_LORA + 2 * ICLR_LORA:]

    kk = k * kk_ref[...]
    kk = kk * lax.rsqrt(_headsum(kk * kk, bd_ref[...]) + EPS)
    r_o[...] = r.astype(BF16)
    k_o[...] = k.astype(BF16)
    v_o[...] = v.astype(BF16)
    kk_o[...] = kk.astype(BF16)
    g_o[...] = _dot(_sigmoid(gl).astype(BF16), g2_ref[...]).astype(BF16)
    wl = wl.astype(BF16)
    al = al.astype(BF16)
    for d, (lw_o, kd_o, b_o) in enumerate(((lw0_o, kd0_o, b0_o), (lw1_o, kd1_o, b1_o))):
        w_log = w0_ref[d:d + 1, :] + _dot(wl, w2_ref[d])
        lw_o[...] = -EXP_NEG_HALF * _sigmoid(w_log)
        a = _sigmoid(a0_ref[d:d + 1, :] + _dot(al, a2_ref[d]))
        kd_o[...] = (k * ((1.0 - ka_ref[...]) + a * ka_ref[...])).astype(BF16)
        b_o[...] = (kk * a).astype(BF16)


def _rwkv_features(f, params, dims):
    rows, fw = f.shape
    t = dims["tm_feat"]
    n, lc, nb = dims["n"], dims["l"], dims["b"]
    tiles_lat, tiles_ctx = n // t, lc // t
    n_lat_tiles = nb * tiles_lat
    hb = t // SUBLANES
    n_halo = rows // SUBLANES
    mu, w0, w2p, a0, a2p, g2, k_k, k_a, bd = params
    bw = B_WIDTH
    const = lambda shape: pl.BlockSpec(shape, lambda i: (0,) * len(shape))
    in_specs = [
        pl.BlockSpec((t, fw), lambda i: (i, 0)),
        pl.BlockSpec((SUBLANES, fw), lambda i: (jnp.maximum(i * hb - 1, 0), 0)),
        pl.BlockSpec((SUBLANES, fw), lambda i: (jnp.minimum((i + 1) * hb, n_halo - 1), 0)),
        const((2, fw)), const((2, bw)), const((2, 2 * DECAY_LORA, bw)), const((2, bw)),
        const((2, 2 * ICLR_LORA, bw)), const((GATE_LORA, bw)), const((1, bw)), const((1, bw)), const((bw, bw)),
    ]
    out = lambda dtype: jax.ShapeDtypeStruct((rows, bw), dtype)
    return pl.pallas_call(
        functools.partial(_feat_kernel, n_lat_tiles=n_lat_tiles, tiles_lat=tiles_lat, tiles_ctx=tiles_ctx),
        grid=(rows // t,),
        in_specs=in_specs,
        out_specs=[pl.BlockSpec((t, bw), lambda i: (i, 0))] * 11,
        out_shape=[out(BF16)] * 5 + [out(F32)] * 2 + [out(BF16)] * 4,
        compiler_params=_cparams(("parallel",)),
        name="rwkv_features",
    )(f, f, f, mu, w0, w2p, a0, a2p, g2, k_k, k_a, bd)


def _stack_heads(x, head0):
    return jnp.concatenate([jnp.where(head0, x, 0.0), jnp.where(head0, 0.0, x)], axis=0).astype(BF16)


def _cumsum_rows(tri, x):
    hi = x.astype(BF16)
    r1 = x - hi.astype(F32)
    mid = r1.astype(BF16)
    lo = (r1 - mid.astype(F32)).astype(BF16)
    return _dot(tri, hi) + _dot(tri, mid) + _dot(tri, lo)


def _wkv_units(units, eye, lvl_ref):
    c = units[0]["v"].shape[0]
    p2 = 2 * c
    head0 = lax.broadcasted_iota(jnp.int32, (1, LANES), 1) < HEAD_DIM
    st = lambda x: _stack_heads(x, head0)
    fold = lambda x: x[:c] + x[c:]
    each = lambda fn: [fn(u) for u in units]

    for u in units:
        u["kq_s"], u["rq_s"], u["v_s"] = st(u["kq"]), st(u["rq"]), st(u["v"])
    a = each(lambda u: _dot_nt(jnp.concatenate([u["kq_s"], u["rq_s"]], axis=0),
                               jnp.concatenate([st(u["bi"]), st(u["ki"])], axis=0)))
    for u, au in zip(units, a):
        u["nkb"] = jnp.where(u["m_strict"], au[:p2, :p2], 0.0)
        u["akk"] = jnp.where(u["m_strict"], au[:p2, p2:], 0.0).astype(BF16)
        u["arb"] = jnp.where(u["m_incl"], au[p2:, :p2], 0.0).astype(BF16)
        u["ark"] = jnp.where(u["m_incl"], au[p2:, p2:], 0.0).astype(BF16)

    nkb_h = each(lambda u: u["nkb"].astype(BF16))
    tinv = each(lambda u: eye - u["nkb"] * lvl_ref[0].astype(F32))
    for lev in range(1, lvl_ref.shape[0]):
        t_h = [t.astype(BF16) for t in tinv]
        lt = [_dot(nh * lvl_ref[lev], th).astype(BF16) for nh, th in zip(nkb_h, t_h)]
        tinv = [t - _dot(th, x) for t, th, x in zip(tinv, t_h, lt)]
    t_h = [t.astype(BF16) for t in tinv]

    akkv = each(lambda u: _dot(u["akk"], u["v_s"]).astype(BF16))
    gp_h = [_dot(th, jnp.concatenate([u["kq_s"], x], axis=1)).astype(BF16)
            for u, th, x in zip(units, t_h, akkv)]
    corr = [_dot(u["arb"], g) for u, g in zip(units, gp_h)]
    arkv = each(lambda u: _dot(u["ark"], u["v_s"]))
    btg = [_dot_tn(st(u["bt"]), g) for u, g in zip(units, gp_h)]
    ktv = each(lambda u: _dot_tn(st(u["kt"]), u["v_s"]))

    for u, cr, av, bg, kv in zip(units, corr, arkv, btg, ktv):
        rq = u["rq"]
        rq2 = fold(jnp.concatenate([jnp.where(head0, rq, 0.0), jnp.where(head0, 0.0, rq)], axis=0) - cr[:, :LANES])
        u["rq2"] = rq2.astype(BF16)
        u["yl"] = fold(av - cr[:, LANES:])
        u["m_state"] = (jnp.where(eye > 0.0, u["decay"], 0.0) - bg[:, :LANES]).astype(BF16)
        u["n_state"] = kv - bg[:, LANES:]


def _scan_kernel(rf, vf, kkf, lwf, kdf, bf, rb, vb, kkb, lwb, kdb, bb, tri_ref, msk_ref, lvl_ref,
                 yf_o, yb_o, hf_s, hb_s):
    @pl.when(pl.program_id(1) == 0)
    def _():
        hf_s[...] = jnp.zeros_like(hf_s)
        hb_s[...] = jnp.zeros_like(hb_s)

    c = SCAN_CHUNK
    eye = msk_ref[0]
    units, chains = [], []
    for d, (refs, y_o, h_s) in enumerate((((rf, vf, kkf, lwf, kdf, bf), yf_o, hf_s),
                                          ((rb, vb, kkb, lwb, kdb, bb), yb_o, hb_s))):
        m_strict = msk_ref[1 + 2 * d] > 0.0
        m_incl = msk_ref[2 + 2 * d] > 0.0
        r, v, kk, lw, kd, b = (ref[...] for ref in refs)
        cw = _cumsum_rows(tri_ref[d], lw)
        tot = [jnp.sum(lw[j * c:(j + 1) * c], axis=0, keepdims=True) for j in range(SCAN_STEP_CHUNKS)]
        tot_rows = jnp.concatenate([jnp.broadcast_to(t, (c, t.shape[1])) for t in tot], axis=0)
        w_inv = jnp.exp(-cw)
        w_rest = jnp.exp(tot_rows - cw)
        facs = {"kq": kk * jnp.exp(cw - lw), "rq": r * jnp.exp(cw), "bi": b * w_inv, "ki": kd * w_inv,
                "bt": b * w_rest, "kt": kd * w_rest, "v": v}
        order = range(SCAN_STEP_CHUNKS) if d == 0 else range(SCAN_STEP_CHUNKS - 1, -1, -1)
        for p in range(B_WIDTH // LANES):
            ls = slice(p * LANES, (p + 1) * LANES)
            chain = []
            for j in order:
                unit = {name: x[j * c:(j + 1) * c, ls] for name, x in facs.items()}
                unit.update(decay=jnp.exp(tot[j][:, ls]), m_strict=m_strict, m_incl=m_incl, rows=slice(j * c, (j + 1) * c))
                units.append(unit)
                chain.append(unit)
            chains.append((chain, y_o, h_s, p, ls))
    _wkv_units(units, eye, lvl_ref)
    states = [h_s[p] for _, _, h_s, p, _ in chains]
    for pos in range(SCAN_STEP_CHUNKS):
        for ci, (chain, y_o, _, _, ls) in enumerate(chains):
            u = chain[pos]
            h_h = states[ci].astype(BF16)
            y_o[u["rows"], ls] = _dot(u["rq2"], h_h) + u["yl"]
            states[ci] = _dot(u["m_state"], h_h) + u["n_state"]
    for h, (_, _, h_s, p, _) in zip(states, chains):
        h_s[p] = h


def _scan_constants():
    c, p2 = SCAN_CHUNK, PAIR_ROWS
    t = np.arange(SCAN_STEP_CHUNKS * c)
    same_chunk = (t[None, :] // c) == (t[:, None] // c)
    tri = np.stack([same_chunk & (t[None, :] <= t[:, None]), same_chunk & (t[None, :] >= t[:, None])]).astype(np.float32)
    i = np.arange(p2)
    same_head = (i[:, None] // c) == (i[None, :] // c)
    ti, si = i[:, None] % c, i[None, :] % c
    msk = np.stack([
        np.eye(p2, dtype=bool),
        same_head & (si < ti), same_head & (si <= ti),
        same_head & (si > ti), same_head & (si >= ti),
    ]).astype(np.float32)
    n_lev = int(np.log2(c))
    lvl = np.stack([
        ((i[:, None] >> (k + 1)) == (i[None, :] >> (k + 1))) & ((i[:, None] >> k) != (i[None, :] >> k))
        for k in range(n_lev)
    ]).astype(np.float32)
    return jnp.asarray(tri, BF16), jnp.asarray(msk), jnp.asarray(lvl, BF16)


def _wkv_scan(feats, dims):
    r, v, kk, lw0, lw1, kd0, kd1, b0, b1 = feats
    rows, bw = r.shape
    c = SCAN_STEP_CHUNKS * SCAN_CHUNK
    n, lc, nb = dims["n"], dims["l"], dims["b"]
    n_c, l_c = n // c, lc // c
    ctx_base = nb * n_c
    tri, msk, lvl = _scan_constants()

    def fwd(b, s):
        return (jnp.where(s < l_c, ctx_base + b * l_c + s, b * n_c + (s - l_c)), 0)

    def bwd(b, s):
        return (jnp.where(s < l_c, ctx_base + b * l_c + (l_c - 1 - s), b * n_c + (n_c - 1 - (s - l_c))), 0)

    const = lambda a: pl.BlockSpec(a.shape, lambda b, s: (0,) * a.ndim)
    out = jax.ShapeDtypeStruct((rows, bw), F32)
    return pl.pallas_call(
        _scan_kernel,
        grid=(nb, l_c + n_c),
        in_specs=[pl.BlockSpec((c, bw), fwd)] * 6 + [pl.BlockSpec((c, bw), bwd)] * 6
                 + [const(tri), const(msk), const(lvl)],
        out_specs=[pl.BlockSpec((c, bw), fwd), pl.BlockSpec((c, bw), bwd)],
        out_shape=[out, out],
        scratch_shapes=[pltpu.VMEM((bw // LANES, LANES, LANES), F32)] * 2,
        compiler_params=_cparams(("parallel", "arbitrary")),
        name="wkv_scan",
    )(r, v, kk, lw0, kd0, b0, r, v, kk, lw1, kd1, b1, tri, msk, lvl)


def _rope_tables(n, tm):
    rows = n // GRID_W
    row = jnp.repeat(jnp.arange(rows, dtype=F32), GRID_W)
    col = jnp.tile(jnp.arange(GRID_W, dtype=F32), rows)
    n_freq = HEAD_DIM // 4
    inv_freq = ROPE_THETA ** (-jnp.arange(n_freq, dtype=F32) / n_freq)
    ang = jnp.concatenate([row[:, None] * inv_freq, col[:, None] * inv_freq], axis=-1)
    cos = jnp.repeat(jnp.cos(ang), 2, axis=-1)
    sin = jnp.repeat(jnp.sin(ang), 2, axis=-1)
    even_lane = (jnp.arange(HEAD_DIM) % 2 == 0)[None, :]
    sin_a = jnp.where(even_lane, -sin, 0.0)
    sin_b = jnp.where(even_lane, 0.0, sin)
    pad = lambda tab, fill: jnp.concatenate(
        [jnp.tile(tab, (1, LANES // HEAD_DIM)), jnp.full((tm, LANES), fill, F32)], axis=0)
    return pad(cos, 1.0), pad(sin_a, 0.0), pad(sin_b, 0.0)


def _block_diag(width, value, dtype):
    h = np.arange(width) // HEAD_DIM
    return jnp.asarray((h[:, None] == h[None, :]) * value, dtype)


def _pad_lora(w):
    r = w.shape[1]
    z = jnp.zeros_like(w[0])
    return jnp.stack([jnp.concatenate([w[0], z], axis=0), jnp.concatenate([z, w[1]], axis=0)])


def _largest_tile(limit, *sizes):
    t = limit
    while any(s % t for s in sizes):
        t //= 2
    return t


def kernel(x, c, ctx, c_ctx, w_mod, b_mod, ffn_in, ffn_out, even_w_in, even_w_out, q_gain, k_gain, rwkv_mu, rwkv_w0,
           rwkv_w2, rwkv_a0, rwkv_a2, rwkv_g2, rwkv_k_k, rwkv_k_a, rwkv_r_k, rwkv_gn_w, rwkv_gn_b, odd_w_in, odd_w_out,
           sink, final_gain):
    nb, n, d = x.shape
    lc = ctx.shape[1]
    depth = w_mod.shape[0]
    dff = ffn_out.shape[2]
    assert n % TQ == 0 and lc % TQ == 0 and n >= TQ + 2 * WINDOW and n % GRID_W == 0
    scan_rows = SCAN_STEP_CHUNKS * SCAN_CHUNK
    assert n % scan_rows == 0 and lc % scan_rows == 0 and nb + 1 <= MOD_ROWS
    dims = {
        "b": nb, "n": n, "l": lc,
        "tm_ffn": _largest_tile(512, n, nb * lc),
        "tm_feat": _largest_tile(256, n, lc),
        "key_chunk": _largest_tile(256, n),
        "tq_global": _largest_tile(256, n, lc),
    }

    cvec = jnp.zeros((MOD_ROWS, d), F32).at[:nb].set(c).at[nb].set(c_ctx)
    mods = _mod_table(cvec, w_mod, b_mod).reshape(depth, MOD_ROWS, N_MOD, d)

    ffn_in_h = ffn_in.astype(BF16)
    ffn_out_h = ffn_out.astype(BF16)
    even_in_h = even_w_in.astype(BF16)
    even_out_h = even_w_out.astype(BF16)
    odd_in_h = odd_w_in.astype(BF16)
    odd_out_h = odd_w_out.astype(BF16)

    rope_tabs = _rope_tables(n, dims["tm_ffn"])
    a_kw = A_KV_HEADS * HEAD_DIM
    a_qw = A_HEADS * HEAD_DIM
    bd_q = _block_diag(a_qw, 1.0 / HEAD_DIM, BF16)
    bd_k = _block_diag(a_kw, 1.0 / HEAD_DIM, BF16)
    bd_ones = _block_diag(B_WIDTH, 1.0, BF16)
    fw = even_w_in.shape[2] - a_qw - 2 * a_kw

    xs = x.reshape(nb * n, d)
    for l in range(depth):
        last = l == depth - 1
        ctx_rows = ctx.reshape(nb * lc, d) if l == 0 else None
        if l % 2 == 0:
            e = l // 2
            tile_h = lambda g, reps: jnp.tile(g, reps).reshape(1, -1)
            norm_args = (tile_h(q_gain[e], A_HEADS), tile_h(k_gain[e], A_KV_HEADS), bd_q, bd_k)
            xs, q, k, v, f = _ffn(xs, mods, ffn_in_h, ffn_out_h, l, 0, 0, dims, ctx_rows=ctx_rows,
                                  proj=(even_in_h, e, rope_tabs, a_qw, a_kw, fw, norm_args))
            oa = _attention(q, k, v, dims, ctx_queries=not last)
            feat_params = (rwkv_mu[e], rwkv_w0[e], _pad_lora(rwkv_w2[e]).astype(BF16), rwkv_a0[e],
                           _pad_lora(rwkv_a2[e]).astype(BF16), rwkv_g2[e].astype(BF16),
                           rwkv_k_k[e].reshape(1, -1), rwkv_k_a[e].reshape(1, -1), bd_ones)
            r, kr, vr, kk, g, lw0, lw1, kd0, kd1, b0, b1 = _rwkv_features(f, feat_params, dims)
            yf, yb = _wkv_scan((r, vr, kk, lw0, lw1, kd0, kd1, b0, b1), dims)
            readout = (yf, yb, r, kr, vr, g, rwkv_r_k[e].reshape(1, -1), rwkv_gn_w[e].reshape(1, -1),
                       rwkv_gn_b[e].reshape(1, -1), bd_ones)
            mixer = (even_out_h, e, oa, readout)
        else:
            o = l // 2
            c_kw = C_KV_HEADS * HEAD_DIM
            c_qw = C_HEADS * HEAD_DIM
            xs, q, k, v = _ffn(xs, mods, ffn_in_h, ffn_out_h, l, 0, 0, dims, ctx_rows=ctx_rows,
                               proj=(odd_in_h, o, rope_tabs, c_qw, c_kw, 0, None))
            oc = _attention(q, k, v, dims, sink=sink[o], ctx_queries=not last)
            mixer = (odd_out_h, o, oc, None)
        xs = _ffn(xs, mods, ffn_in_h, ffn_out_h, l, 1, 6, dims, final_gain=final_gain if last else None, mixer=mixer)
    return xs.reshape(nb, n, d)
```

```python
import functools

import jax
import jax.numpy as jnp
import numpy as np
from jax import lax
from jax.experimental import pallas as pl
from jax.experimental.pallas import tpu as pltpu

F32 = jnp.float32
BF16 = jnp.bfloat16

HEAD_DIM = 64
GRID_W = 64
A_HEADS, A_KV_HEADS = 8, 2
B_HEADS = 8
B_WIDTH = B_HEADS * HEAD_DIM
DECAY_LORA, ICLR_LORA, GATE_LORA = 64, 64, 128
C_HEADS, C_KV_HEADS = 16, 4
GQA_GROUP = 4
WINDOW = 128
N_MOD = 9
ROPE_THETA = 10000.0
EPS = 1e-6
GN_EPS = 64e-5
NEG_BIG = -1e30
LOG2_E = 1.4426950408889634
EXP_NEG_HALF = 0.6065306597126334

LANES = 128
SUBLANES = 8
VMEM_LIMIT_BYTES = 56 * 1024 * 1024

SCAN_CHUNK = 64
SCAN_STEP_CHUNKS = 4
PAIR_ROWS = 2 * SCAN_CHUNK
TQ = 128
AHEAD = 4
ONES_ROWS = 16
MOD_ROWS = 16
MOD_COLS_TILE = 1536


def _cparams(sem):
    return pltpu.CompilerParams(dimension_semantics=sem, vmem_limit_bytes=VMEM_LIMIT_BYTES)


def _dot(a, b):
    return jnp.dot(a, b, preferred_element_type=F32)


def _dot_nt(a, b):
    return lax.dot_general(a, b, (((1,), (1,)), ((), ())), preferred_element_type=F32)


def _dot_tn(a, b):
    return lax.dot_general(a, b, (((0,), (0,)), ((), ())), preferred_element_type=F32)


def _rms(x):
    return x * lax.rsqrt(jnp.mean(x * x, axis=-1, keepdims=True) + EPS)


def _sigmoid(x):
    return 1.0 / (1.0 + jnp.exp(-x))


def _headsum(x, bd, split=False):
    hi = x.astype(BF16)
    if not split:
        return _dot(hi, bd)
    lo = (x - hi.astype(F32)).astype(BF16)
    return _dot(hi, bd) + _dot(lo, bd)


def _mod_kernel(c_ref, w_ref, b_ref, o_ref):
    c = c_ref[...]
    s = c * _sigmoid(c)
    hi = s.astype(BF16)
    lo = (s - hi.astype(F32)).astype(BF16)
    w = w_ref[...].astype(BF16)
    o_ref[...] = _dot(hi, w) + _dot(lo, w) + b_ref[...]


def _mod_table(cvec, w_mod, b_mod):
    depth, d, nd = w_mod.shape
    tn = MOD_COLS_TILE if nd % MOD_COLS_TILE == 0 else nd
    return pl.pallas_call(
        _mod_kernel,
        grid=(depth, nd // tn),
        in_specs=[
            pl.BlockSpec((MOD_ROWS, d), lambda l, n: (0, 0)),
            pl.BlockSpec((None, d, tn), lambda l, n: (l, 0, n)),
            pl.BlockSpec((None, 1, tn), lambda l, n: (l, 0, n)),
        ],
        out_specs=pl.BlockSpec((None, MOD_ROWS, tn), lambda l, n: (l, 0, n)),
        out_shape=jax.ShapeDtypeStruct((depth, MOD_ROWS, nd), F32),
        compiler_params=_cparams(("parallel", "parallel")),
        name="mod_table",
    )(cvec, w_mod, b_mod.reshape(depth, 1, nd))


def _mixer_out(oa_ref, wm_ref, readout):
    if readout is None:
        return _dot(oa_ref[...], wm_ref[...])
    yf, yb, r, k, v, g, rk, gnw, gnb, bd = readout
    y = yf[...] + yb[...]
    mean = _headsum(y, bd[...], split=True) * (1.0 / HEAD_DIM)
    yc = y - mean
    var = _headsum(yc * yc, bd[...]) * (1.0 / HEAD_DIM)
    yn = yc * lax.rsqrt(var + GN_EPS) * gnw[...] + gnb[...]
    up = lambda ref: ref[...].astype(F32)
    bonus = _headsum(up(r) * up(k) * rk[...], bd[...]) * up(v)
    ob = ((yn + bonus) * up(g)).astype(BF16)
    half = oa_ref.shape[1]
    return _dot(oa_ref[...], wm_ref[:half, :]) + _dot(ob, wm_ref[half:, :])


def _ffn_kernel(x_ref, mod_ref, wg_ref, wu_ref, wo_ref, *rest, i_shift, final, n_lat_tiles, mixer, proj):
    n_out = 1 if proj is None else (5 if proj["fw"] else 4)
    rest, outs = rest[:len(rest) - n_out], rest[len(rest) - n_out:]
    if proj is not None:
        n_proj = 8 if proj["qk_norm"] else 4
        rest, proj_refs = rest[:len(rest) - n_proj], rest[len(rest) - n_proj:]
    o_ref = outs[0]
    ctx_ref = None
    if n_lat_tiles is not None:
        ctx_ref, rest = rest[0], rest[1:]
    x = x_ref[...]
    if ctx_ref is not None:
        x = jnp.where(pl.program_id(0) < n_lat_tiles, x, ctx_ref[...])
    if mixer is not None:
        readout = rest[2:12] if mixer == "even" else None
        x = x + mod_ref[5:6, :] * _mixer_out(rest[1], rest[0], readout)
    half = x.shape[0] // 2
    shift, scale = mod_ref[i_shift:i_shift + 1, :], 1.0 + mod_ref[i_shift + 1:i_shift + 2, :]
    gate = 0.5 * mod_ref[i_shift + 2:i_shift + 3, :]
    xs, gu = [x[:half], x[half:]], []
    for xh in xs:
        h = (_rms(xh) * scale + shift).astype(BF16)
        gu.append((_dot(h, wg_ref[...]), _dot(h, wu_ref[...])))
    ys = []
    for idx, (x, (g, u)) in enumerate(zip(xs, gu)):
        a = (g * _sigmoid(g) * u).astype(BF16)
        y = x + gate * _dot(a, wo_ref[...])
        if final:
            y = _rms(y) * rest[-1][...]
        o_ref[idx * half:(idx + 1) * half, :] = y
        ys.append(y)
    if proj is not None:
        _proj_halves(ys, mod_ref, proj_refs, outs[1:], **proj)


def _ffn(xs, mods, w_in, w_out, l, j, i_shift, dims, final_gain=None, ctx_rows=None, mixer=None, proj=None):
    rows, d = xs.shape
    dff = w_out.shape[2]
    tm = dims["tm_ffn"]
    tiles_per_batch = dims["n"] // tm
    nb = dims["b"]
    n_lat_tiles = nb * tiles_per_batch
    final = final_gain is not None
    if final:
        rows = nb * dims["n"]
    split = ctx_rows is not None
    if split:
        rows = rows + ctx_rows.shape[0]
    in_specs = [
        pl.BlockSpec((tm, d), (lambda i: (jnp.minimum(i, n_lat_tiles - 1), 0)) if split else (lambda i: (i, 0))),
        pl.BlockSpec((None, None, N_MOD, d), lambda i: (l, jnp.minimum(i // tiles_per_batch, nb), 0, 0)),
        pl.BlockSpec((None, None, d, dff), lambda i: (l, j, 0, 0)),
        pl.BlockSpec((None, None, d, dff), lambda i: (l, j, 0, 1)),
        pl.BlockSpec((None, None, dff, d), lambda i: (l, j, 0, 0)),
    ]
    args = [xs, mods, w_in, w_in, w_out]
    if split:
        in_specs.append(pl.BlockSpec((tm, d), lambda i: (jnp.maximum(i - n_lat_tiles, 0), 0)))
        args.append(ctx_rows)
    mixer_kind = None
    if mixer is not None:
        w_mix, e, o_att, readout = mixer
        mixer_kind = "odd" if readout is None else "even"
        in_specs += [pl.BlockSpec((None, d, d), lambda i: (e, 0, 0)),
                     pl.BlockSpec((tm, o_att.shape[1]), lambda i: (i, 0))]
        args += [w_mix, o_att]
        if readout is not None:
            bw = B_WIDTH
            in_specs += [pl.BlockSpec((tm, bw), lambda i: (i, 0))] * 6
            in_specs += [pl.BlockSpec((1, bw), lambda i: (0, 0))] * 3 + [pl.BlockSpec((bw, bw), lambda i: (0, 0))]
            args += list(readout)
    if final:
        in_specs.append(pl.BlockSpec((1, d), lambda i: (0, 0)))
        args.append(final_gain.reshape(1, d))
    out_specs = [pl.BlockSpec((tm, d), lambda i: (i, 0))]
    out_shape = [jax.ShapeDtypeStruct((rows, d), F32)]
    proj_cfg = None
    if proj is not None:
        w_pin, e, rope_tabs, qw, kw, fw, norm_args = proj
        proj_cfg = {"qw": qw, "kw": kw, "fw": fw, "qk_norm": norm_args is not None}

        def tab_idx(i):
            return (jnp.where(i < n_lat_tiles, i % tiles_per_batch, tiles_per_batch), 0)

        in_specs += [pl.BlockSpec((None, d, qw + 2 * kw + fw), lambda i: (e, 0, 0))]
        in_specs += [pl.BlockSpec((tm, LANES), tab_idx)] * 3
        args += [w_pin, *rope_tabs]
        if norm_args is not None:
            in_specs += [pl.BlockSpec((1, qw), lambda i: (0, 0)), pl.BlockSpec((1, kw), lambda i: (0, 0)),
                         pl.BlockSpec((qw, qw), lambda i: (0, 0)), pl.BlockSpec((kw, kw), lambda i: (0, 0))]
            args += list(norm_args)
        out_specs += [pl.BlockSpec((qw, tm), lambda i: (0, i)), pl.BlockSpec((tm, kw), lambda i: (i, 0)),
                      pl.BlockSpec((kw, tm), lambda i: (0, i))]
        out_shape += [jax.ShapeDtypeStruct((qw, rows), BF16), jax.ShapeDtypeStruct((rows, kw), BF16),
                      jax.ShapeDtypeStruct((kw, rows), BF16)]
        if fw:
            out_specs.append(pl.BlockSpec((tm, fw), lambda i: (i, 0)))
            out_shape.append(jax.ShapeDtypeStruct((rows, fw), F32))
    outs = pl.pallas_call(
        functools.partial(_ffn_kernel, i_shift=i_shift, final=final, n_lat_tiles=n_lat_tiles if split else None,
                          mixer=mixer_kind, proj=proj_cfg),
        grid=(rows // tm,),
        in_specs=in_specs,
        out_specs=out_specs,
        out_shape=out_shape,
        compiler_params=_cparams(("parallel",)),
        name="ffn",
    )(*args)
    return outs[0] if proj is None else outs


def _rope(x, cos, sin_a, sin_b):
    outs = []
    for g in range(x.shape[1] // LANES):
        xg = x[:, g * LANES:(g + 1) * LANES]
        nxt = pltpu.roll(xg, LANES - 1, 1)
        prv = pltpu.roll(xg, 1, 1)
        outs.append(xg * cos + nxt * sin_a + prv * sin_b)
    return outs[0] if len(outs) == 1 else jnp.concatenate(outs, axis=1)


def _proj_halves(xs, mod_ref, in_refs, out_refs, *, qw, kw, fw, qk_norm):
    w_ref, cos_ref, sa_ref, sb_ref = in_refs[:4]
    if qk_norm:
        qg_ref, kg_ref, bdq_ref, bdk_ref = in_refs[4:8]
    q_ref, k_ref, v_ref = out_refs[:3]
    half = xs[0].shape[0]
    halves = [slice(0, half), slice(half, 2 * half)]
    hn, qkv = [], []
    for x in xs:
        h = (_rms(x) * (1.0 + mod_ref[4:5, :]) + mod_ref[3:4, :]).astype(BF16)
        hn.append(h)
        kv = _dot(h, w_ref[:, qw:qw + 2 * kw])
        qkv.append((_dot(h, w_ref[:, :qw]), kv[:, :kw], kv[:, kw:]))
    for rows, h, (q, k, v) in zip(halves, hn, qkv):
        if qk_norm:
            q = q * lax.rsqrt(_headsum(q * q, bdq_ref[...]) + EPS) * qg_ref[...]
            k = k * lax.rsqrt(_headsum(k * k, bdk_ref[...]) + EPS) * kg_ref[...]
        cos, sa, sb = cos_ref[rows, :], sa_ref[rows, :], sb_ref[rows, :]
        q_ref[:, rows] = (_rope(q, cos, sa, sb) * (HEAD_DIM ** -0.5 * LOG2_E)).T.astype(BF16)
        k_ref[rows, :] = _rope(k, cos, sa, sb).astype(BF16)
        v_ref[:, rows] = v.T.astype(BF16)
        if fw:
            out_refs[3][rows, :] = _dot(h, w_ref[:, qw + 2 * kw:])


def _attend(qt_ref, segs, sink_ref, n_pad, o_ref):
    tq = qt_ref.shape[1]
    cols = GQA_GROUP * tq
    heads = range(qt_ref.shape[0] // (GQA_GROUP * HEAD_DIM))
    pair = lambda h: slice((h // 2) * LANES, (h // 2 + 1) * LANES)
    qpad, m, acc = [], [], []
    for h in heads:
        q4 = jnp.concatenate([qt_ref[(h * GQA_GROUP + g) * HEAD_DIM:(h * GQA_GROUP + g + 1) * HEAD_DIM, :]
                              for g in range(GQA_GROUP)], axis=1)
        z = jnp.zeros_like(q4)
        qpad.append(jnp.concatenate([q4, z] if h % 2 == 0 else [z, q4], axis=0))
        if sink_ref is not None:
            mh, lh = sink_ref[h], jnp.ones((1, cols), F32)
        else:
            mh, lh = jnp.full((1, cols), NEG_BIG, F32), jnp.zeros((1, cols), F32)
        if n_pad is not None:
            m_new = jnp.maximum(mh, jnp.where(n_pad > 0.0, 0.0, NEG_BIG))
            lh = lh * jnp.exp2(mh - m_new) + jnp.where(n_pad > 0.0, n_pad * jnp.exp2(-m_new), 0.0)
            mh = m_new
        m.append(mh)
        acc.append(jnp.concatenate([jnp.zeros((HEAD_DIM, cols), F32), jnp.broadcast_to(lh, (ONES_ROWS, cols))], axis=0))
    def scores(item):
        (k, _, valid), h = item
        s = _dot(k[:, pair(h)], qpad[h])
        if valid is not None:
            s = jnp.where(jnp.concatenate([valid] * GQA_GROUP, axis=1), s, NEG_BIG)
        return s

    items = [(seg, h) for seg in segs for h in heads]
    pending = [scores(it) for it in items[:AHEAD]]
    for idx, ((_, vt, _), h) in enumerate(items):
        s = pending.pop(0)
        if idx + AHEAD < len(items):
            pending.append(scores(items[idx + AHEAD]))
        m_new = jnp.maximum(m[h], jnp.max(s, axis=0, keepdims=True))
        alpha = jnp.exp2(m[h] - m_new)
        p = jnp.exp2(s - m_new).astype(BF16)
        v1 =jnp.concatenate([vt[h * HEAD_DIM:(h + 1) * HEAD_DIM, :], jnp.ones((ONES_ROWS, vt.shape[1]), BF16)], axis=0)
        acc[h] = alpha * acc[h] + _dot(v1, p)
        m[h] = m_new
    blocks = []
    for h in heads:
        ot = acc[h][:HEAD_DIM, :] * (1.0 / acc[h][HEAD_DIM:HEAD_DIM + 1, :])
        for g in range(0, GQA_GROUP, 2):
            blocks.append(jnp.concatenate([ot[:, g * tq:(g + 1) * tq], ot[:, (g + 1) * tq:(g + 2) * tq]], axis=0).T)
    o_ref[...] = jnp.concatenate(blocks, axis=1).astype(o_ref.dtype)


def _attn_global_kernel(qt_ref, kl_ref, vl_ref, kc_ref, vc_ref, o_ref, *, n_q_lat, key_chunk):
    i = pl.program_id(1)
    ctx_seg = (kc_ref[...], vc_ref[...], None)

    @pl.when(i < n_q_lat)
    def _():
        segs = [(kl_ref[c * key_chunk:(c + 1) * key_chunk, :], vl_ref[:, c * key_chunk:(c + 1) * key_chunk], None)
                for c in range(kl_ref.shape[0] // key_chunk)]
        _attend(qt_ref, segs + [ctx_seg], None, None, o_ref)

    @pl.when(i >= n_q_lat)
    def _():
        _attend(qt_ref, [ctx_seg], None, None, o_ref)


def _attn_window_kernel(qt_ref, k0_ref, k1_ref, k2_ref, v0_ref, v1_ref, v2_ref, kc_ref, vc_ref, sink_ref, o_ref,
                        *, n, n_q_lat):
    i = pl.program_id(1)
    tq = qt_ref.shape[1]
    ctx_seg = (kc_ref[...], vc_ref[...], None)

    @pl.when(i < n_q_lat)
    def _():
        k = jnp.concatenate([k0_ref[...], k1_ref[...], k2_ref[...]], axis=0)
        vt = jnp.concatenate([v0_ref[...], v1_ref[...], v2_ref[...]], axis=1)
        kpos = (i - 1) * tq + lax.broadcasted_iota(jnp.int32, (3 * tq, tq), 0)
        qpos = i * tq + lax.broadcasted_iota(jnp.int32, (3 * tq, tq), 1)
        valid = (jnp.abs(kpos - qpos) <= WINDOW) & (kpos >= WINDOW) & (kpos < n)
        q_row = i * tq + lax.broadcasted_iota(jnp.int32, (1, tq), 1)
        n_pad = jnp.maximum(q_row + WINDOW - n + 1, 0).astype(F32)
        n_pad = jnp.concatenate([n_pad] * GQA_GROUP, axis=1)
        _attend(qt_ref, [(k, vt, valid), ctx_seg], sink_ref, n_pad, o_ref)

    @pl.when(i >= n_q_lat)
    def _():
        _attend(qt_ref, [ctx_seg], sink_ref, None, o_ref)


def _attention(qt, k, vt, dims, sink=None, ctx_queries=True):
    qw, rows = qt.shape
    kw = k.shape[1]
    n, lc, nb = dims["n"], dims["l"], dims["b"]
    tq = TQ if sink is not None else dims["tq_global"]
    n_q_lat = n // tq
    n_q_ctx = lc // tq
    lat_blocks = nb * n_q_lat
    ctx_block = nb * n // lc

    def q_blk(b, i):
        return jnp.where(i < n_q_lat, b * n_q_lat + i, lat_blocks + b * n_q_ctx + (i - n_q_lat))

    ctx_specs = [pl.BlockSpec((lc, kw), lambda b, i: (ctx_block + b, 0)),
                 pl.BlockSpec((kw, lc), lambda b, i: (0, ctx_block + b))]
    qt_spec = pl.BlockSpec((qw, tq), lambda b, i: (0, q_blk(b, i)))
    if sink is None:
        body = functools.partial(_attn_global_kernel, n_q_lat=n_q_lat, key_chunk=dims["key_chunk"])
        in_specs = [qt_spec,
                    pl.BlockSpec((n, kw), lambda b, i: (b, 0)),
                    pl.BlockSpec((kw, n), lambda b, i: (0, b))] + ctx_specs
        args = [qt, k, vt, k, vt]
    else:
        assert WINDOW == TQ
        body = functools.partial(_attn_window_kernel, n=n, n_q_lat=n_q_lat)
        nbr = lambda b, i, off: b * n_q_lat + jnp.clip(i + off, 0, n_q_lat - 1)
        n_kv = kw // HEAD_DIM
        in_specs = ([qt_spec]
                    + [pl.BlockSpec((TQ, kw), functools.partial(lambda b, i, off: (nbr(b, i, off), 0), off=off))
                       for off in (-1, 0, 1)]
                    + [pl.BlockSpec((kw, TQ), functools.partial(lambda b, i, off: (0, nbr(b, i, off)), off=off))
                       for off in (-1, 0, 1)]
                    + ctx_specs
                    + [pl.BlockSpec((n_kv, 1, GQA_GROUP * TQ), lambda b, i: (0, 0, 0))])
        sink_rows = jnp.repeat(sink.astype(F32).reshape(n_kv, 1, GQA_GROUP) * LOG2_E, TQ, axis=2)
        args = [qt, k, k, k, vt, vt, vt, k, vt, sink_rows]
    return pl.pallas_call(
        body,
        grid=(nb, n_q_lat + (n_q_ctx if ctx_queries else 0)),
        in_specs=in_specs,
        out_specs=pl.BlockSpec((tq, qw), lambda b, i: (q_blk(b, i), 0)),
        out_shape=jax.ShapeDtypeStruct((rows if ctx_queries else nb * n, qw), BF16),
        compiler_params=_cparams(("parallel", "arbitrary")),
        name="gqa_global" if sink is None else "gqa_window",
    )(*args)


def _feat_kernel(f_ref, fp_ref, fn_ref, mu_ref, w0_ref, w2_ref, a0_ref, a2_ref, g2_ref, kk_ref, ka_ref, bd_ref,
                 r_o, k_o, v_o, kk_o, g_o, lw0_o, lw1_o, kd0_o, kd1_o, b0_o, b1_o,
                 *, n_lat_tiles, tiles_lat, tiles_ctx):
    i = pl.program_id(0)
    t = f_ref.shape[0]
    bw = B_WIDTH
    is_lat = i < n_lat_tiles
    pos = jnp.where(is_lat, i % tiles_lat, (i - n_lat_tiles) % tiles_ctx)
    last = jnp.where(is_lat, tiles_lat, tiles_ctx) - 1
    f = f_ref[...]
    prow = jnp.where(pos == 0, 0.0, fp_ref[SUBLANES - 1:SUBLANES, :])
    nrow = jnp.where(pos == last, 0.0, fn_ref[0:1, :])
    rid = lax.broadcasted_iota(jnp.int32, (SUBLANES, 1), 0)
    prev = pltpu.roll(f, 1, 0)
    prev = jnp.concatenate([jnp.where(rid == 0, prow, prev[:SUBLANES]), prev[SUBLANES:]], axis=0)
    nxt = pltpu.roll(f, t - 1, 0)
    nxt = jnp.concatenate([nxt[:t - SUBLANES], jnp.where(rid == SUBLANES - 1, nrow, nxt[t - SUBLANES:])], axis=0)
    mu_p, mu_n = mu_ref[0:1, :], mu_ref[1:2, :]
    fs = f * (1.0 - mu_p - mu_n) + mu_p * prev + mu_n * nxt

    r = fs[:, :bw]
    k = fs[:, bw:2 * bw]
    v = fs[:, 2 * bw:3 * bw]
    wl = jnp.tanh(fs[:, 3 * bw:3 * bw + 2 * DECAY_LORA])
    al = fs[:, 3 * bw + 2 * DECAY_LORA:3 * bw + 2 * DECAY_LORA + 2 * ICLR_LORA]
    gl = fs[:, 3 * bw + 2 * DECAY_LORA + 2 * ICLR_LORA:]

    kk = k * kk_ref[...]
    kk = kk * lax.rsqrt(_headsum(kk * kk, bd_ref[...]) + EPS)
    r_o[...] = r.astype(BF16)
    k_o[...] = k.astype(BF16)
    v_o[...] = v.astype(BF16)
    kk_o[...] = kk.astype(BF16)
    g_o[...] = _dot(_sigmoid(gl).astype(BF16), g2_ref[...]).astype(BF16)
    wl = wl.astype(BF16)
    al = al.astype(BF16)
    for d, (lw_o, kd_o, b_o) in enumerate(((lw0_o, kd0_o, b0_o), (lw1_o, kd1_o, b1_o))):
        w_log = w0_ref[d:d + 1, :] + _dot(wl, w2_ref[d])
        lw_o[...] = -EXP_NEG_HALF * _sigmoid(w_log)
        a = _sigmoid(a0_ref[d:d + 1, :] + _dot(al, a2_ref[d]))
        kd_o[...] = (k * ((1.0 - ka_ref[...]) + a * ka_ref[...])).astype(BF16)
        b_o[...] = (kk * a).astype(BF16)


def _rwkv_features(f, params, dims):
    rows, fw = f.shape
    t = dims["tm_feat"]
    n, lc, nb = dims["n"], dims["l"], dims["b"]
    tiles_lat, tiles_ctx = n // t, lc // t
    n_lat_tiles = nb * tiles_lat
    hb = t // SUBLANES
    n_halo = rows // SUBLANES
    mu, w0, w2p, a0, a2p, g2, k_k, k_a, bd = params
    bw = B_WIDTH
    const = lambda shape: pl.BlockSpec(shape, lambda i: (0,) * len(shape))
    in_specs = [
        pl.BlockSpec((t, fw), lambda i: (i, 0)),
        pl.BlockSpec((SUBLANES, fw), lambda i: (jnp.maximum(i * hb - 1, 0), 0)),
        pl.BlockSpec((SUBLANES, fw), lambda i: (jnp.minimum((i + 1) * hb, n_halo - 1), 0)),
        const((2, fw)), const((2, bw)), const((2, 2 * DECAY_LORA, bw)), const((2, bw)),
        const((2, 2 * ICLR_LORA, bw)), const((GATE_LORA, bw)), const((1, bw)), const((1, bw)), const((bw, bw)),
    ]
    out = lambda dtype: jax.ShapeDtypeStruct((rows, bw), dtype)
    return pl.pallas_call(
        functools.partial(_feat_kernel, n_lat_tiles=n_lat_tiles, tiles_lat=tiles_lat, tiles_ctx=tiles_ctx),
        grid=(rows // t,),
        in_specs=in_specs,
        out_specs=[pl.BlockSpec((t, bw), lambda i: (i, 0))] * 11,
        out_shape=[out(BF16)] * 5 + [out(F32)] * 2 + [out(BF16)] * 4,
        compiler_params=_cparams(("parallel",)),
        name="rwkv_features",
    )(f, f, f, mu, w0, w2p, a0, a2p, g2, k_k, k_a, bd)


def _stack_heads(x, head0):
    return jnp.concatenate([jnp.where(head0, x, 0.0), jnp.where(head0, 0.0, x)], axis=0).astype(BF16)


def _cumsum_rows(tri, x):
    hi = x.astype(BF16)
    r1 = x - hi.astype(F32)
    mid = r1.astype(BF16)
    lo = (r1 - mid.astype(F32)).astype(BF16)
    return _dot(tri, hi) + _dot(tri, mid) + _dot(tri, lo)


def _wkv_units(units, eye, lvl_ref):
    c = units[0]["v"].shape[0]
    p2 = 2 * c
    head0 = lax.broadcasted_iota(jnp.int32, (1, LANES), 1) < HEAD_DIM
    st = lambda x: _stack_heads(x, head0)
    fold = lambda x: x[:c] + x[c:]
    each = lambda fn: [fn(u) for u in units]

    for u in units:
        u["kq_s"], u["rq_s"], u["v_s"] = st(u["kq"]), st(u["rq"]), st(u["v"])
    a = each(lambda u: _dot_nt(jnp.concatenate([u["kq_s"], u["rq_s"]], axis=0),
                               jnp.concatenate([st(u["bi"]), st(u["ki"])], axis=0)))
    for u, au in zip(units, a):
        u["nkb"] = jnp.where(u["m_strict"], au[:p2, :p2], 0.0)
        u["akk"] = jnp.where(u["m_strict"], au[:p2, p2:], 0.0).astype(BF16)
        u["arb"] = jnp.where(u["m_incl"], au[p2:, :p2], 0.0).astype(BF16)
        u["ark"] = jnp.where(u["m_incl"], au[p2:, p2:], 0.0).astype(BF16)

    nkb_h = each(lambda u: u["nkb"].astype(BF16))
    tinv = each(lambda u: eye - u["nkb"] * lvl_ref[0].astype(F32))
    for lev in range(1, lvl_ref.shape[0]):
        t_h = [t.astype(BF16) for t in tinv]
        lt = [_dot(nh * lvl_ref[lev], th).astype(BF16) for nh, th in zip(nkb_h, t_h)]
        tinv = [t - _dot(th, x) for t, th, x in zip(tinv, t_h, lt)]
    t_h = [t.astype(BF16) for t in tinv]

    akkv = each(lambda u: _dot(u["akk"], u["v_s"]).astype(BF16))
    gp_h = [_dot(th, jnp.concatenate([u["kq_s"], x], axis=1)).astype(BF16)
            for u, th, x in zip(units, t_h, akkv)]
    corr = [_dot(u["arb"], g) for u, g in zip(units, gp_h)]
    arkv = each(lambda u: _dot(u["ark"], u["v_s"]))
    btg = [_dot_tn(st(u["bt"]), g) for u, g in zip(units, gp_h)]
    ktv = each(lambda u: _dot_tn(st(u["kt"]), u["v_s"]))

    for u, cr, av, bg, kv in zip(units, corr, arkv, btg, ktv):
        rq = u["rq"]
        rq2 = fold(jnp.concatenate([jnp.where(head0, rq, 0.0), jnp.where(head0, 0.0, rq)], axis=0) - cr[:, :LANES])
        u["rq2"] = rq2.astype(BF16)
        u["yl"] = fold(av - cr[:, LANES:])
        u["m_state"] = (jnp.where(eye > 0.0, u["decay"], 0.0) - bg[:, :LANES]).astype(BF16)
        u["n_state"] = kv - bg[:, LANES:]


def _scan_kernel(rf, vf, kkf, lwf, kdf, bf, rb, vb, kkb, lwb, kdb, bb, tri_ref, msk_ref, lvl_ref,
                 yf_o, yb_o, hf_s, hb_s):
    @pl.when(pl.program_id(1) == 0)
    def _():
        hf_s[...] = jnp.zeros_like(hf_s)
        hb_s[...] = jnp.zeros_like(hb_s)

    c = SCAN_CHUNK
    eye = msk_ref[0]
    units, chains = [], []
    for d, (refs, y_o, h_s) in enumerate((((rf, vf, kkf, lwf, kdf, bf), yf_o, hf_s),
                                          ((rb, vb, kkb, lwb, kdb, bb), yb_o, hb_s))):
        m_strict = msk_ref[1 + 2 * d] > 0.0
        m_incl = msk_ref[2 + 2 * d] > 0.0
        r, v, kk, lw, kd, b = (ref[...] for ref in refs)
        cw = _cumsum_rows(tri_ref[d], lw)
        tot = [jnp.sum(lw[j * c:(j + 1) * c], axis=0, keepdims=True) for j in range(SCAN_STEP_CHUNKS)]
        tot_rows = jnp.concatenate([jnp.broadcast_to(t, (c, t.shape[1])) for t in tot], axis=0)
        w_inv = jnp.exp(-cw)
        w_rest = jnp.exp(tot_rows - cw)
        facs = {"kq": kk * jnp.exp(cw - lw), "rq": r * jnp.exp(cw), "bi": b * w_inv, "ki": kd * w_inv,
                "bt": b * w_rest, "kt": kd * w_rest, "v": v}
        order = range(SCAN_STEP_CHUNKS) if d == 0 else range(SCAN_STEP_CHUNKS - 1, -1, -1)
        for p in range(B_WIDTH // LANES):
            ls = slice(p * LANES, (p + 1) * LANES)
            chain = []
            for j in order:
                unit = {name: x[j * c:(j + 1) * c, ls] for name, x in facs.items()}
                unit.update(decay=jnp.exp(tot[j][:, ls]), m_strict=m_strict, m_incl=m_incl, rows=slice(j * c, (j + 1) * c))
                units.append(unit)
                chain.append(unit)
            chains.append((chain, y_o, h_s, p, ls))
    _wkv_units(units, eye, lvl_ref)
    states = [h_s[p] for _, _, h_s, p, _ in chains]
    for pos in range(SCAN_STEP_CHUNKS):
        for ci, (chain, y_o, _, _, ls) in enumerate(chains):
            u = chain[pos]
            h_h = states[ci].astype(BF16)
            y_o[u["rows"], ls] = _dot(u["rq2"], h_h) + u["yl"]
            states[ci] = _dot(u["m_state"], h_h) + u["n_state"]
    for h, (_, _, h_s, p, _) in zip(states, chains):
        h_s[p] = h


def _scan_constants():
    c, p2 = SCAN_CHUNK, PAIR_ROWS
    t = np.arange(SCAN_STEP_CHUNKS * c)
    same_chunk = (t[None, :] // c) == (t[:, None] // c)
    tri = np.stack([same_chunk & (t[None, :] <= t[:, None]), same_chunk & (t[None, :] >= t[:, None])]).astype(np.float32)
    i = np.arange(p2)
    same_head = (i[:, None] // c) == (i[None, :] // c)
    ti, si = i[:, None] % c, i[None, :] % c
    msk = np.stack([
        np.eye(p2, dtype=bool),
        same_head & (si < ti), same_head & (si <= ti),
        same_head & (si > ti), same_head & (si >= ti),
    ]).astype(np.float32)
    n_lev = int(np.log2(c))
    lvl = np.stack([
        ((i[:, None] >> (k + 1)) == (i[None, :] >> (k + 1))) & ((i[:, None] >> k) != (i[None, :] >> k))
        for k in range(n_lev)
    ]).astype(np.float32)
    return jnp.asarray(tri, BF16), jnp.asarray(msk), jnp.asarray(lvl, BF16)


def _wkv_scan(feats, dims):
    r, v, kk, lw0, lw1, kd0, kd1, b0, b1 = feats
    rows, bw = r.shape
    c = SCAN_STEP_CHUNKS * SCAN_CHUNK
    n, lc, nb = dims["n"], dims["l"], dims["b"]
    n_c, l_c = n // c, lc // c
    ctx_base = nb * n_c
    tri, msk, lvl = _scan_constants()

    def fwd(b, s):
        return (jnp.where(s < l_c, ctx_base + b * l_c + s, b * n_c + (s - l_c)), 0)

    def bwd(b, s):
        return (jnp.where(s < l_c, ctx_base + b * l_c + (l_c - 1 - s), b * n_c + (n_c - 1 - (s - l_c))), 0)

    const = lambda a: pl.BlockSpec(a.shape, lambda b, s: (0,) * a.ndim)
    out = jax.ShapeDtypeStruct((rows, bw), F32)
    return pl.pallas_call(
        _scan_kernel,
        grid=(nb, l_c + n_c),
        in_specs=[pl.BlockSpec((c, bw), fwd)] * 6 + [pl.BlockSpec((c, bw), bwd)] * 6
                 + [const(tri), const(msk), const(lvl)],
        out_specs=[pl.BlockSpec((c, bw), fwd), pl.BlockSpec((c, bw), bwd)],
        out_shape=[out, out],
        scratch_shapes=[pltpu.VMEM((bw // LANES, LANES, LANES), F32)] * 2,
        compiler_params=_cparams(("parallel", "arbitrary")),
        name="wkv_scan",
    )(r, v, kk, lw0, kd0, b0, r, v, kk, lw1, kd1, b1, tri, msk, lvl)


def _rope_tables(n, tm):
    rows = n // GRID_W
    row = jnp.repeat(jnp.arange(rows, dtype=F32), GRID_W)
    col = jnp.tile(jnp.arange(GRID_W, dtype=F32), rows)
    n_freq = HEAD_DIM // 4
    inv_freq = ROPE_THETA ** (-jnp.arange(n_freq, dtype=F32) / n_freq)
    ang = jnp.concatenate([row[:, None] * inv_freq, col[:, None] * inv_freq], axis=-1)
    cos = jnp.repeat(jnp.cos(ang), 2, axis=-1)
    sin = jnp.repeat(jnp.sin(ang), 2, axis=-1)
    even_lane = (jnp.arange(HEAD_DIM) % 2 == 0)[None, :]
    sin_a = jnp.where(even_lane, -sin, 0.0)
    sin_b = jnp.where(even_lane, 0.0, sin)
    pad = lambda tab, fill: jnp.concatenate(
        [jnp.tile(tab, (1, LANES // HEAD_DIM)), jnp.full((tm, LANES), fill, F32)], axis=0)
    return pad(cos, 1.0), pad(sin_a, 0.0), pad(sin_b, 0.0)


def _block_diag(width, value, dtype):
    h = np.arange(width) // HEAD_DIM
    return jnp.asarray((h[:, None] == h[None, :]) * value, dtype)


def _pad_lora(w):
    r = w.shape[1]
    z = jnp.zeros_like(w[0])
    return jnp.stack([jnp.concatenate([w[0], z], axis=0), jnp.concatenate([z, w[1]], axis=0)])


def _largest_tile(limit, *sizes):
    t = limit
    while any(s % t for s in sizes):
        t //= 2
    return t


def kernel(x, c, ctx, c_ctx, w_mod, b_mod, ffn_in, ffn_out, even_w_in, even_w_out, q_gain, k_gain, rwkv_mu, rwkv_w0,
           rwkv_w2, rwkv_a0, rwkv_a2, rwkv_g2, rwkv_k_k, rwkv_k_a, rwkv_r_k, rwkv_gn_w, rwkv_gn_b, odd_w_in, odd_w_out,
           sink, final_gain):
    nb, n, d = x.shape
    lc = ctx.shape[1]
    depth = w_mod.shape[0]
    dff = ffn_out.shape[2]
    assert n % TQ == 0 and lc % TQ == 0 and n >= TQ + 2 * WINDOW and n % GRID_W == 0
    scan_rows = SCAN_STEP_CHUNKS * SCAN_CHUNK
    assert n % scan_rows == 0 and lc % scan_rows == 0 and nb + 1 <= MOD_ROWS
    dims = {
        "b": nb, "n": n, "l": lc,
        "tm_ffn": _largest_tile(512, n, nb * lc),
        "tm_feat": _largest_tile(256, n, lc),
        "key_chunk": _largest_tile(256, n),
        "tq_global": _largest_tile(256, n, lc),
    }

    cvec = jnp.zeros((MOD_ROWS, d), F32).at[:nb].set(c).at[nb].set(c_ctx)
    mods = _mod_table(cvec, w_mod, b_mod).reshape(depth, MOD_ROWS, N_MOD, d)

    ffn_in_h = ffn_in.astype(BF16)
    ffn_out_h = ffn_out.astype(BF16)
    even_in_h = even_w_in.astype(BF16)
    even_out_h = even_w_out.astype(BF16)
    odd_in_h = odd_w_in.astype(BF16)
    odd_out_h = odd_w_out.astype(BF16)

    rope_tabs = _rope_tables(n, dims["tm_ffn"])
    a_kw = A_KV_HEADS * HEAD_DIM
    a_qw = A_HEADS * HEAD_DIM
    bd_q = _block_diag(a_qw, 1.0 / HEAD_DIM, BF16)
    bd_k = _block_diag(a_kw, 1.0 / HEAD_DIM, BF16)
    bd_ones = _block_diag(B_WIDTH, 1.0, BF16)
    fw = even_w_in.shape[2] - a_qw - 2 * a_kw

    xs = x.reshape(nb * n, d)
    for l in range(depth):
        last = l == depth - 1
        ctx_rows = ctx.reshape(nb * lc, d) if l == 0 else None
        if l % 2 == 0:
            e = l // 2
            tile_h = lambda g, reps: jnp.tile(g, reps).reshape(1, -1)
            norm_args = (tile_h(q_gain[e], A_HEADS), tile_h(k_gain[e], A_KV_HEADS), bd_q, bd_k)
            xs, q, k, v, f = _ffn(xs, mods, ffn_in_h, ffn_out_h, l, 0, 0, dims, ctx_rows=ctx_rows,
                                  proj=(even_in_h, e, rope_tabs, a_qw, a_kw, fw, norm_args))
            oa = _attention(q, k, v, dims, ctx_queries=not last)
            feat_params = (rwkv_mu[e], rwkv_w0[e], _pad_lora(rwkv_w2[e]).astype(BF16), rwkv_a0[e],
                           _pad_lora(rwkv_a2[e]).astype(BF16), rwkv_g2[e].astype(BF16),
                           rwkv_k_k[e].reshape(1, -1), rwkv_k_a[e].reshape(1, -1), bd_ones)
            r, kr, vr, kk, g, lw0, lw1, kd0, kd1, b0, b1 = _rwkv_features(f, feat_params, dims)
            yf, yb = _wkv_scan((r, vr, kk, lw0, lw1, kd0, kd1, b0, b1), dims)
            readout = (yf, yb, r, kr, vr, g, rwkv_r_k[e].reshape(1, -1), rwkv_gn_w[e].reshape(1, -1),
                       rwkv_gn_b[e].reshape(1, -1), bd_ones)
            mixer = (even_out_h, e, oa, readout)
        else:
            o = l // 2
            c_kw = C_KV_HEADS * HEAD_DIM
            c_qw = C_HEADS * HEAD_DIM
            xs, q, k, v = _ffn(xs, mods, ffn_in_h, ffn_out_h, l, 0, 0, dims, ctx_rows=ctx_rows,
                               proj=(odd_in_h, o, rope_tabs, c_qw, c_kw, 0, None))
            oc = _attention(q, k, v, dims, sink=sink[o], ctx_queries=not last)
            mixer = (odd_out_h, o, oc, None)
        xs = _ffn(xs, mods, ffn_in_h, ffn_out_h, l, 1, 6, dims, final_gain=final_gain if last else None, mixer=mixer)
    return xs.reshape(nb, n, d)
```
